```python
import jax
import jax.numpy as jnp
from jax import lax
import numpy as np

D_MODEL = 2048
BATCH = 2
SEQ = 8192
DEPTH = 2

GRID_W = 64
CTX_LEN = 256
HEAD_DIM = 64
GROUP_W = D_MODEL // 4
NA_HEADS = GROUP_W // HEAD_DIM
NA_KH = 8
NA_KW = 16
NA_KBW = 2 * NA_KW
SC_W = GROUP_W
SC_KSIZE = 3
GLA_HEADS = 4
GLA_DK = GROUP_W // (2 * GLA_HEADS)
GLA_DV = GROUP_W // GLA_HEADS
GLA_RANK = 16
GLA_TAU = 16.0
GLA_CHUNK = 32
SWA_HEADS = GROUP_W // HEAD_DIM
SWA_KV_HEADS = 2
SWA_WINDOW = 128
SWA_BLOCK = 128
ROPE_BASE = 10000.0
N_EXPERTS = 64
TOP_K = 8
N_GROUPS = 8
TOPK_GROUPS = 4
D_EXPERT = D_MODEL // 4
D_SHARED = D_EXPERT
ROUTED_SCALE = 2.5
MOE_BLOCK = 256
DN_ALPHA = (2 * DEPTH) ** 0.25
DN_BETA = (8 * DEPTH) ** -0.25
LN_EPS = 1e-6
PROJ_SPLITS = (GROUP_W, GROUP_W, GROUP_W,
               SC_W, SC_W, SC_W,
               GLA_HEADS * GLA_DK, GLA_HEADS * GLA_DK, GLA_HEADS * GLA_DV, GLA_HEADS * GLA_DV, 2 * GLA_RANK,
               SWA_HEADS * HEAD_DIM, SWA_KV_HEADS * HEAD_DIM, SWA_KV_HEADS * HEAD_DIM)
PROJ_W = sum(PROJ_SPLITS)

kernel_name = 'hybrid_diffusion_na_conv_gla_swa_moe'


def _layer_norm(x, gain=None, bias=None):
    xf = x.astype(jnp.float32)
    mu = jnp.mean(xf, axis=-1, keepdims=True)
    var = jnp.mean(jnp.square(xf - mu), axis=-1, keepdims=True)
    y = (xf - mu) * lax.rsqrt(var + LN_EPS)
    if gain is not None:
        y = y * gain.astype(jnp.float32) + bias.astype(jnp.float32)
    return y.astype(x.dtype)


def _modulate(h, shift, scale):
    return _layer_norm(h) * (1 + scale) + shift


def _heads(t, n):
    return t.reshape(*t.shape[:-1], n, t.shape[-1] // n)


def _split_proj(p):
    cuts = np.cumsum(PROJ_SPLITS)[:-1].tolist()
    return jnp.split(p, cuts, axis=-1)


def _rope_1d(x, pos):
    half = x.shape[-1] // 2
    inv = ROPE_BASE ** (-2.0 * jnp.arange(half, dtype=jnp.float32) / x.shape[-1])
    ang = pos[:, None] * inv[None, :]
    cos = jnp.cos(ang)[None, :, None, :].astype(x.dtype)
    sin = jnp.sin(ang)[None, :, None, :].astype(x.dtype)
    x1, x2 = x[..., :half], x[..., half:]
    return jnp.concatenate([x1 * cos - x2 * sin, x2 * cos + x1 * sin], axis=-1)


def _axial_rope(x, row, col):
    half = x.shape[-1] // 2
    return jnp.concatenate([_rope_1d(x[..., :half], row), _rope_1d(x[..., half:], col)], axis=-1)


def _context_attention(q, k, v, sink=None):
    bsz, n, hq, dh = q.shape
    hkv = k.shape[2]
    grp = hq // hkv
    qg = q.reshape(bsz, n, hkv, grp, dh)
    s = jnp.einsum('bnkgd,bmkd->bkgnm', qg, k).astype(jnp.float32) * dh ** -0.5
    if sink is not None:
        sk = jnp.broadcast_to(sink.astype(jnp.float32).reshape(1, hkv, grp, 1, 1), (*s.shape[:-1], 1))
        p = jax.nn.softmax(jnp.concatenate([s, sk], axis=-1), axis=-1)[..., :-1]
    else:
        p = jax.nn.softmax(s, axis=-1)
    o = jnp.einsum('bkgnm,bmkd->bnkgd', p.astype(v.dtype), v)
    return o.reshape(bsz, n, hq * dh)


def _neighbourhood_attention(q, k, v, kc, vc, rpb):
    bsz, seq, nh, dh = q.shape
    rows = seq // GRID_W
    kh = min(NA_KH, rows)
    ncb = GRID_W // NA_KW
    scale = dh ** -0.5
    r = jnp.arange(rows)
    row_idx = jnp.clip(r - kh // 2, 0, rows - kh)[:, None] + jnp.arange(kh)[None, :]
    qcol = jnp.arange(ncb)[:, None] * NA_KW + jnp.arange(NA_KW)[None, :]
    kcol = jnp.clip(jnp.arange(ncb) * NA_KW - NA_KW // 2, 0, GRID_W - NA_KBW)[:, None] + jnp.arange(NA_KBW)[None, :]
    cstart = jnp.clip(qcol - NA_KW // 2, 0, GRID_W - NA_KW)
    valid = (kcol[:, None, :] >= cstart[:, :, None]) & (kcol[:, None, :] < cstart[:, :, None] + NA_KW)
    coff = jnp.clip(kcol[:, None, :] - qcol[:, :, None], 1 - NA_KW, NA_KW - 1) + NA_KW - 1
    roff = row_idx - r[:, None] + NA_KH - 1
    bias = rpb[:, roff[:, None, None, :, None], coff[None, :, :, None, :]].astype(jnp.float32)
    bias = jnp.where(valid[None, None, :, :, None, :], bias, -jnp.inf).transpose(1, 2, 3, 0, 4, 5)
    ri = row_idx[:, :, None, None]
    ci = kcol[None, None]
    kg = k.reshape(bsz, rows, GRID_W, nh, dh)[:, ri, ci]
    vg = v.reshape(bsz, rows, GRID_W, nh, dh)[:, ri, ci]
    qg = q.reshape(bsz, rows, ncb, NA_KW, nh, dh)
    s_nb = jnp.einsum('brmqhd,brimjhd->brmqhij', qg, kg).astype(jnp.float32) * scale + bias
    s_ctx = jnp.einsum('brmqhd,bnhd->brmqhn', qg, kc).astype(jnp.float32) * scale
    nb = kh * NA_KBW
    logits = jnp.concatenate([s_nb.reshape(*s_nb.shape[:5], nb), s_ctx], axis=-1)
    p = jax.nn.softmax(logits, axis=-1).astype(v.dtype)
    o = (jnp.einsum('brmqhij,brimjhd->brmqhd', p[..., :nb].reshape(s_nb.shape), vg)
         + jnp.einsum('brmqhn,bnhd->brmqhd', p[..., nb:], vc))
    return o.reshape(bsz, seq, nh * dh)


def _window_attention(q, k, v, kc, vc, sink):
    bsz, seq, hq, dh = q.shape
    hkv = k.shape[2]
    grp = hq // hkv
    nblk = seq // SWA_BLOCK
    qb = q.reshape(bsz, nblk, SWA_BLOCK, hkv, grp, dh)

    def band(t):
        tb = jnp.pad(t.reshape(bsz, nblk, SWA_BLOCK, hkv, dh), ((0, 0), (1, 1), (0, 0), (0, 0), (0, 0)))
        return jnp.concatenate([tb[:, :-2], tb[:, 1:-1], tb[:, 2:]], axis=2)

    kb, vb = band(k), band(v)
    qpos = jnp.arange(nblk)[:, None] * SWA_BLOCK + jnp.arange(SWA_BLOCK)[None, :]
    kpos = (jnp.arange(nblk)[:, None] - 1) * SWA_BLOCK + jnp.arange(3 * SWA_BLOCK)[None, :]
    valid = ((jnp.abs(qpos[:, :, None] - kpos[:, None, :]) <= SWA_WINDOW)
             & (kpos[:, None, :] >= 0) & (kpos[:, None, :] < seq))
    scale = dh ** -0.5
    s_win = jnp.einsum('bnqkgd,bnjkd->bnkgqj', qb, kb).astype(jnp.float32) * scale
    s_win = jnp.where(valid[None, :, None, None], s_win, -jnp.inf)
    s_ctx = jnp.einsum('bnqkgd,bmkd->bnkgqm', qb, kc).astype(jnp.float32) * scale
    s_sink = jnp.broadcast_to(sink.astype(jnp.float32).reshape(1, 1, hkv, grp, 1, 1), (*s_win.shape[:-1], 1))
    p = jax.nn.softmax(jnp.concatenate([s_win, s_ctx, s_sink], axis=-1), axis=-1).astype(v.dtype)
    nw = 3 * SWA_BLOCK
    nc = kc.shape[1]
    o = (jnp.einsum('bnkgqj,bnjkd->bnqkgd', p[..., :nw], vb)
         + jnp.einsum('bnkgqm,bmkd->bnqkgd', p[..., nw:nw + nc], vc))
    return o.reshape(bsz, seq, hq * dh)


def _short_conv(u, gate_b, gate_c, w):
    ch = u.shape[-1]
    z = lax.conv_general_dilated(gate_c * u, w[:, None, :], window_strides=(1,),
                                 padding=((SC_KSIZE // 2, SC_KSIZE // 2),),
                                 dimension_numbers=('NWC', 'WIO', 'NWC'), feature_group_count=ch)
    return gate_b * z


def _gla_log_decay(z, w2, b):
    zf, zb = jnp.split(z, 2, axis=-1)
    lf = jax.nn.log_sigmoid((zf @ w2[0] + b[0]).astype(jnp.float32)) / GLA_TAU
    lb = jax.nn.log_sigmoid((zb @ w2[1] + b[1]).astype(jnp.float32)) / GLA_TAU
    return _heads(lf, GLA_HEADS), _heads(lb, GLA_HEADS)


def _gla_scan(q, k, v, log_a, s0):
    bsz, seq, nh, dk = q.shape
    dv = v.shape[-1]
    nc = seq // GLA_CHUNK

    def chunks(t):
        return t.astype(jnp.float32).reshape(bsz, nc, GLA_CHUNK, nh, t.shape[-1])

    qf = chunks(q) * dk ** -0.5
    kf = chunks(k)
    vf = chunks(v)
    cum = jnp.cumsum(chunks(log_a), axis=2)
    last = cum[:, :, -1:]
    lower = jnp.tril(jnp.ones((GLA_CHUNK, GLA_CHUNK), dtype=bool))
    decay = jnp.exp(jnp.where(lower[:, :, None, None], cum[:, :, :, None] - cum[:, :, None, :], -jnp.inf))
    attn = jnp.einsum('bcthd,bcshd,bctshd->bchts', qf, kf, decay)
    o_intra = jnp.einsum('bchts,bcshv->bcthv', attn, vf)
    upd = jnp.einsum('bcshk,bcshv->bchkv', kf * jnp.exp(last - cum), vf)
    gdec = jnp.exp(last[:, :, 0])

    def step(state, inp):
        g, u = inp
        return g[..., None] * state + u, state

    s_fin, s_in = lax.scan(step, s0, (jnp.moveaxis(gdec, 1, 0), jnp.moveaxis(upd, 1, 0)))
    o_inter = jnp.einsum('bcthk,cbhkv->bcthv', qf * jnp.exp(cum), s_in)
    return (o_intra + o_inter).reshape(bsz, seq, nh, dv), s_fin


def _gla_bidirectional(q, k, v, la_f, la_b, qc, kc, vc, lac_f, lac_b):
    bsz, _, nh, dk = q.shape
    s0 = jnp.zeros((bsz, nh, dk, v.shape[-1]), jnp.float32)

    def flip(t):
        return jnp.flip(t, axis=1)

    oc_f, sc_f = _gla_scan(qc, kc, vc, lac_f, s0)
    oc_b, sc_b = _gla_scan(flip(qc), flip(kc), flip(vc), flip(lac_b), s0)
    o_f, _ = _gla_scan(q, k, v, la_f, sc_f)
    o_b, _ = _gla_scan(flip(q), flip(k), flip(v), flip(la_b), sc_b)
    return o_f + flip(o_b), oc_f + flip(oc_b)


def _gla_out(o, g, gain):
    on = o * lax.rsqrt(jnp.mean(jnp.square(o), axis=-1, keepdims=True) + LN_EPS) * gain.astype(jnp.float32)
    return (on.astype(g.dtype) * jax.nn.silu(g.reshape(o.shape))).reshape(g.shape)


def _token_mixers(p_lat, p_ctx, rpb, conv_w, gla_w2, gla_b, gla_norm_g, sink, with_ctx_out):
    seq = p_lat.shape[1]
    na_q, na_k, na_v, sc_u, sc_b, sc_c, gl_q, gl_k, gl_v, gl_g, gl_z, sw_q, sw_k, sw_v = _split_proj(p_lat)
    cna_q, cna_k, cna_v, csc_u, csc_b, csc_c, cgl_q, cgl_k, cgl_v, cgl_g, cgl_z, csw_q, csw_k, csw_v = _split_proj(p_ctx)
    y_na = _neighbourhood_attention(_heads(na_q, NA_HEADS), _heads(na_k, NA_HEADS), _heads(na_v, NA_HEADS),
                                    _heads(cna_k, NA_HEADS), _heads(cna_v, NA_HEADS), rpb)
    y_sc = _short_conv(sc_u, sc_b, sc_c, conv_w)
    la_f, la_b = _gla_log_decay(gl_z, gla_w2, gla_b)
    lac_f, lac_b = _gla_log_decay(cgl_z, gla_w2, gla_b)
    o_gl, oc_gl = _gla_bidirectional(_heads(gl_q, GLA_HEADS), _heads(gl_k, GLA_HEADS), _heads(gl_v, GLA_HEADS), la_f, la_b,
                                     _heads(cgl_q, GLA_HEADS), _heads(cgl_k, GLA_HEADS), _heads(cgl_v, GLA_HEADS), lac_f, lac_b)
    y_gl = _gla_out(o_gl, gl_g, gla_norm_g)
    t = jnp.arange(seq)
    row = (t // GRID_W).astype(jnp.float32)
    col = (t % GRID_W).astype(jnp.float32)
    y_sw = _window_attention(_axial_rope(_heads(sw_q, SWA_HEADS), row, col),
                             _axial_rope(_heads(sw_k, SWA_KV_HEADS), row, col),
                             _heads(sw_v, SWA_KV_HEADS), _heads(csw_k, SWA_KV_HEADS), _heads(csw_v, SWA_KV_HEADS), sink)
    y_lat = jnp.concatenate([y_na, y_sc, y_gl, y_sw], axis=-1)
    if not with_ctx_out:
        return y_lat, None
    y_ctx = jnp.concatenate([
        _context_attention(_heads(cna_q, NA_HEADS), _heads(cna_k, NA_HEADS), _heads(cna_v, NA_HEADS)),
        _short_conv(csc_u, csc_b, csc_c, conv_w),
        _gla_out(oc_gl, cgl_g, gla_norm_g),
        _context_attention(_heads(csw_q, SWA_HEADS), _heads(csw_k, SWA_KV_HEADS), _heads(csw_v, SWA_KV_HEADS), sink),
    ], axis=-1)
    return y_lat, y_ctx


def _swiglu(h, wg, wu, wd):
    return (jax.nn.silu(h @ wg) * (h @ wu)) @ wd


def _routed_experts(h, eidx, wts, w_eg, w_eu, w_ed):
    n, d = h.shape
    nk = n * TOP_K
    flat_e = eidx.reshape(-1)
    flat_tok = jnp.repeat(jnp.arange(n, dtype=jnp.int32), TOP_K)
    flat_w = wts.reshape(-1)
    order = jnp.argsort(flat_e)
    sorted_e = flat_e[order]
    counts = jnp.bincount(flat_e, length=N_EXPERTS)
    padded = (counts + MOE_BLOCK - 1) // MOE_BLOCK * MOE_BLOCK
    pad_end = jnp.cumsum(padded)
    pad_start = pad_end - padded
    start = jnp.cumsum(counts) - counts
    dest = pad_start[sorted_e] + jnp.arange(nk) - start[sorted_e]
    n_blocks = (nk + N_EXPERTS * (MOE_BLOCK - 1) + MOE_BLOCK - 1) // MOE_BLOCK
    n_rows = n_blocks * MOE_BLOCK
    tok_buf = jnp.full((n_rows,), n, jnp.int32).at[dest].set(flat_tok[order])
    gate_buf = jnp.zeros((n_rows,), h.dtype).at[dest].set(flat_w[order])
    block_e = jnp.minimum(jnp.searchsorted(pad_end, jnp.arange(n_blocks) * MOE_BLOCK, side='right'), N_EXPERTS - 1)
    hp = jnp.concatenate([h, jnp.zeros((1, d), h.dtype)], axis=0)

    def expert_block(args):
        ids, e = args
        xb = hp[ids]
        return (jax.nn.silu(xb @ w_eg[e]) * (xb @ w_eu[e])) @ w_ed[e]

    y = lax.map(expert_block, (tok_buf.reshape(n_blocks, MOE_BLOCK), block_e)).reshape(n_rows, d)
    return jax.ops.segment_sum(y * gate_buf[:, None], tok_buf, num_segments=n + 1)[:n]


def _moe(h, w_router, router_bias, w_eg, w_eu, w_ed, w_sg, w_su, w_sd):
    n, _ = h.shape
    scores = jax.nn.sigmoid((h @ w_router).astype(jnp.float32))
    sel = scores + router_bias.astype(jnp.float32)
    grp = sel.reshape(n, N_GROUPS, N_EXPERTS // N_GROUPS)
    grp_score = jnp.sum(lax.top_k(grp, 2)[0], axis=-1)
    _, gidx = lax.top_k(grp_score, TOPK_GROUPS)
    gmask = jnp.sum(jax.nn.one_hot(gidx, N_GROUPS, dtype=jnp.float32), axis=-2) > 0
    emask = jnp.repeat(gmask, N_EXPERTS // N_GROUPS, axis=-1)
    _, eidx = lax.top_k(jnp.where(emask, sel, -jnp.inf), TOP_K)
    wts = jnp.take_along_axis(scores, eidx, axis=-1)
    wts = wts / jnp.sum(wts, axis=-1, keepdims=True) * ROUTED_SCALE
    routed = _routed_experts(h, eidx, wts.astype(h.dtype), w_eg, w_eu, w_ed)
    return routed + _swiglu(h, w_sg, w_su, w_sd)


def setup_inputs(seed: int = 0) -> dict:
    key = jax.random.key(seed)
    ks = jax.random.split(key, 26)
    f32 = jnp.float32
    D = D_MODEL

    def nrm(k, shape, s):
        return jax.random.normal(k, shape, f32) * s

    return {
        'x': nrm(ks[0], (BATCH, SEQ, D), 1.0),
        'c': nrm(ks[1], (BATCH, D), 1.0),
        'ctx': nrm(ks[2], (BATCH, CTX_LEN, D), 1.0),
        'c_ctx': nrm(ks[3], (D,), 1.0),
        'w_ada': nrm(ks[4], (DEPTH, D, 6 * D), 0.5 * D ** -0.5),
        'b_ada': nrm(ks[5], (DEPTH, 6 * D), 0.02),
        'w_in': nrm(ks[6], (DEPTH, D, PROJ_W), D ** -0.5),
        'na_rpb': nrm(ks[7], (DEPTH, NA_HEADS, 2 * NA_KH - 1, 2 * NA_KW - 1), 0.1),
        'conv_w': nrm(ks[8], (DEPTH, SC_KSIZE, SC_W), SC_KSIZE ** -0.5),
        'gla_w2': nrm(ks[9], (DEPTH, 2, GLA_RANK, GLA_HEADS * GLA_DK), GLA_RANK ** -0.5),
        'gla_b': nrm(ks[10], (DEPTH, 2, GLA_HEADS * GLA_DK), 0.1),
        'gla_norm_g': 1.0 + nrm(ks[11], (DEPTH, GLA_DV), 0.01),
        'swa_sink': nrm(ks[12], (DEPTH, SWA_HEADS), 0.5),
        'w_out': nrm(ks[13], (DEPTH, D, D), DN_BETA * D ** -0.5),
        'ln1_g': 1.0 + nrm(ks[14], (DEPTH, D), 0.01),
        'ln1_b': nrm(ks[15], (DEPTH, D), 0.01),
        'w_router': nrm(ks[16], (DEPTH, D, N_EXPERTS), D ** -0.5),
        'router_bias': nrm(ks[17], (DEPTH, N_EXPERTS), 0.01),
        'w_exp_gate': nrm(ks[18], (DEPTH, N_EXPERTS, D, D_EXPERT), D ** -0.5),
        'w_exp_up': nrm(ks[19], (DEPTH, N_EXPERTS, D, D_EXPERT), D ** -0.5),
        'w_exp_down': nrm(ks[20], (DEPTH, N_EXPERTS, D_EXPERT, D), DN_BETA * D_EXPERT ** -0.5),
        'w_sh_gate': nrm(ks[21], (DEPTH, D, D_SHARED), D ** -0.5),
        'w_sh_up': nrm(ks[22], (DEPTH, D, D_SHARED), D ** -0.5),
        'w_sh_down': nrm(ks[23], (DEPTH, D_SHARED, D), DN_BETA * D_SHARED ** -0.5),
        'ln2_g': 1.0 + nrm(ks[24], (DEPTH, D), 0.01),
        'ln2_b': nrm(ks[25], (DEPTH, D), 0.01),
    }


def reference(x, c, ctx, c_ctx, w_ada, b_ada, w_in, na_rpb, conv_w, gla_w2, gla_b, gla_norm_g, swa_sink, w_out,
              ln1_g, ln1_b, w_router, router_bias, w_exp_gate, w_exp_up, w_exp_down, w_sh_gate, w_sh_up, w_sh_down,
              ln2_g, ln2_b):
    bsz, seq, d = x.shape
    h_ctx = ctx
    for layer in range(DEPTH):
        last = layer == DEPTH - 1
        mod = jax.nn.silu(c) @ w_ada[layer] + b_ada[layer]
        mod_c = jax.nn.silu(c_ctx) @ w_ada[layer] + b_ada[layer]
        sh1, sc1, g1, sh2, sc2, g2 = [t[:, None, :] for t in jnp.split(mod, 6, axis=-1)]
        sh1c, sc1c, g1c, sh2c, sc2c, g2c = jnp.split(mod_c, 6, axis=-1)
        p_lat = _modulate(x, sh1, sc1) @ w_in[layer]
        p_ctx = _modulate(h_ctx, sh1c, sc1c) @ w_in[layer]
        y_lat, y_ctx = _token_mixers(p_lat, p_ctx, na_rpb[layer], conv_w[layer], gla_w2[layer], gla_b[layer],
                                     gla_norm_g[layer], swa_sink[layer], not last)
        x = _layer_norm(DN_ALPHA * x + g1 * (y_lat @ w_out[layer]), ln1_g[layer], ln1_b[layer])
        moe_w = (w_router[layer], router_bias[layer], w_exp_gate[layer], w_exp_up[layer], w_exp_down[layer],
                 w_sh_gate[layer], w_sh_up[layer], w_sh_down[layer])
        f_in = _modulate(x, sh2, sc2).reshape(-1, d)
        if last:
            f_lat = _moe(f_in, *moe_w).reshape(bsz, seq, d)
        else:
            h_ctx = _layer_norm(DN_ALPHA * h_ctx + g1c * (y_ctx @ w_out[layer]), ln1_g[layer], ln1_b[layer])
            n_ctx = bsz * h_ctx.shape[1]
            fc_in = _modulate(h_ctx, sh2c, sc2c).reshape(-1, d)
            f_all = _moe(jnp.concatenate([fc_in, f_in], axis=0), *moe_w)
            h_ctx = _layer_norm(DN_ALPHA * h_ctx + g2c * f_all[:n_ctx].reshape(h_ctx.shape), ln2_g[layer], ln2_b[layer])
            f_lat = f_all[n_ctx:].reshape(bsz, seq, d)
        x = _layer_norm(DN_ALPHA * x + g2 * f_lat, ln2_g[layer], ln2_b[layer])
    return x
```

```python
import functools

import jax
import jax.numpy as jnp
import numpy as np
from jax import lax
from jax.experimental import pallas as pl
from jax.experimental.pallas import tpu as pltpu

GRID_W = 64
HEAD_DIM = 64
NA_KH = 8
NA_KW = 16
NA_KBW = 2 * NA_KW
SC_KSIZE = 3
GLA_HEADS = 4
GLA_RANK = 16
GLA_TAU = 16.0
GLA_CHUNK = 32
SWA_KV_HEADS = 2
SWA_WINDOW = 128
SWA_BLOCK = 128
ROPE_BASE = 10000.0
N_EXPERTS = 64
TOP_K = 8
N_GROUPS = 8
TOPK_GROUPS = 4
ROUTED_SCALE = 2.5
LN_EPS = 1e-6

V7X_LANES = 128
V7X_VMEM_LIMIT_BYTES = 48 * 1024 * 1024

PROJ_TM = 1024
PROJ_TN = 512
OUT_TM = 256
MOE_TB = 256
FFN_TM = 512

F32 = jnp.float32
BF16 = jnp.bfloat16


def _cparams(sem):
    return pltpu.CompilerParams(dimension_semantics=sem, vmem_limit_bytes=V7X_VMEM_LIMIT_BYTES)


def _pick_tile(m, preferred):
    t = preferred
    while t > 8 and m % t:
        t //= 2
    assert m % t == 0, (m, preferred)
    return t


def _ln_rows(x):
    mu = jnp.mean(x, axis=-1, keepdims=True)
    xc = x - mu
    var = jnp.mean(xc * xc, axis=-1, keepdims=True)
    return xc * lax.rsqrt(var + LN_EPS)


def _proj_kernel(x_ref, sh_ref, sc_ref, w_ref, o16_ref, o32_ref, xn_ref, *, nb16):
    j = pl.program_id(1)

    @pl.when(j == 0)
    def _():
        y = _ln_rows(x_ref[...]) * (1.0 + sc_ref[0]) + sh_ref[0]
        xn_ref[...] = y.astype(BF16)

    acc = jnp.dot(xn_ref[...], w_ref[...], preferred_element_type=F32)

    @pl.when(j < nb16)
    def _():
        o16_ref[...] = acc.astype(BF16)

    @pl.when(j >= nb16)
    def _():
        o32_ref[...] = acc


def _proj_call(x2d, shift, scale, w_packed, *, rows_per_group, n16):
    m, d = x2d.shape
    ntot = w_packed.shape[1]
    tm = min(PROJ_TM, rows_per_group)
    assert m % tm == 0 and rows_per_group % tm == 0 and ntot % PROJ_TN == 0 and n16 % PROJ_TN == 0
    nb16 = n16 // PROJ_TN
    nb = ntot // PROJ_TN
    grp = lambda i, j: ((i * tm) // rows_per_group, 0, 0)
    return pl.pallas_call(
        functools.partial(_proj_kernel, nb16=nb16),
        grid=(m // tm, nb),
        in_specs=[
            pl.BlockSpec((tm, d), lambda i, j: (i, 0)),
            pl.BlockSpec((1, 1, d), grp),
            pl.BlockSpec((1, 1, d), grp),
            pl.BlockSpec((d, PROJ_TN), lambda i, j: (0, j)),
        ],
        out_specs=[
            pl.BlockSpec((tm, PROJ_TN), lambda i, j: (i, jnp.minimum(j, nb16 - 1))),
            pl.BlockSpec((tm, PROJ_TN), lambda i, j: (i, jnp.maximum(j - nb16, 0))),
        ],
        out_shape=[jax.ShapeDtypeStruct((m, n16), BF16), jax.ShapeDtypeStruct((m, ntot - n16), F32)],
        scratch_shapes=[pltpu.VMEM((tm, d), BF16)],
        compiler_params=_cparams(("parallel", "arbitrary")),
        name="proj",
    )(x2d, shift, scale, w_packed)


def _outproj_kernel(y0_ref, y1_ref, y2_ref, y3_ref, x_ref, g1_ref, w_ref, lng_ref, lnb_ref,
                    sh2_ref, sc2_ref, xo_ref, h_ref, *, alpha):
    gw = y0_ref.shape[1]
    acc = jnp.dot(y0_ref[...].astype(BF16), w_ref[0:gw, :], preferred_element_type=F32)
    acc += jnp.dot(y1_ref[...].astype(BF16), w_ref[gw:2 * gw, :], preferred_element_type=F32)
    acc += jnp.dot(y2_ref[...].astype(BF16), w_ref[2 * gw:3 * gw, :], preferred_element_type=F32)
    acc += jnp.dot(y3_ref[...].astype(BF16), w_ref[3 * gw:4 * gw, :], preferred_element_type=F32)
    r = alpha * x_ref[...] + g1_ref[0] * acc
    xn = _ln_rows(r) * lng_ref[...] + lnb_ref[...]
    xo_ref[...] = xn
    h_ref[...] = (_ln_rows(xn) * (1.0 + sc2_ref[0]) + sh2_ref[0]).astype(BF16)


def _outproj_call(ys, x2d, gate1, w_out_bf16, ln_g, ln_b, shift2, scale2, *, rows_per_group, alpha):
    m, d = x2d.shape
    gw = d // 4
    tm = min(OUT_TM, rows_per_group)
    assert m % tm == 0 and rows_per_group % tm == 0
    grp = lambda i: ((i * tm) // rows_per_group, 0, 0)
    if len(ys) == 1:
        y_args = [ys[0]] * 4
        y_specs = [pl.BlockSpec((tm, gw), functools.partial(lambda i, k: (i, k), k=k)) for k in range(4)]
    else:
        y_args = list(ys)
        y_specs = [pl.BlockSpec((tm, gw), lambda i: (i, 0)) for _ in range(4)]
    row = lambda i: (i, 0)
    const2 = lambda i: (0, 0)
    return pl.pallas_call(
        functools.partial(_outproj_kernel, alpha=alpha),
        grid=(m // tm,),
        in_specs=y_specs + [
            pl.BlockSpec((tm, d), row),
            pl.BlockSpec((1, 1, d), grp),
            pl.BlockSpec((d, d), const2),
            pl.BlockSpec((1, d), const2),
            pl.BlockSpec((1, d), const2),
            pl.BlockSpec((1, 1, d), grp),
            pl.BlockSpec((1, 1, d), grp),
        ],
        out_specs=[pl.BlockSpec((tm, d), row), pl.BlockSpec((tm, d), row)],
        out_shape=[jax.ShapeDtypeStruct((m, d), F32), jax.ShapeDtypeStruct((m, d), BF16)],
        compiler_params=_cparams(("parallel",)),
        name="outproj",
    )(*y_args, x2d, gate1, w_out_bf16, ln_g.reshape(1, d), ln_b.reshape(1, d), shift2, scale2)


def _expert_kernel(be_ref, nused_ref, x_ref, gate_ref, wg_ref, wu_ref, wd_ref, y_ref):
    b = pl.program_id(0)

    @pl.when(b < nused_ref[0])
    def _():
        x = x_ref[...]
        g = jnp.dot(x, wg_ref[...], preferred_element_type=F32)
        u = jnp.dot(x, wu_ref[...], preferred_element_type=F32)
        a = (g * jax.nn.sigmoid(g) * u).astype(BF16)
        y = jnp.dot(a, wd_ref[...], preferred_element_type=F32)
        y_ref[...] = y * gate_ref[...]

    @pl.when(b >= nused_ref[0])
    def _():
        y_ref[...] = jnp.zeros_like(y_ref)


def _expert_call(block_e, n_used, x_sorted, gate_sorted, wg, wu, wd):
    n_rows, d = x_sorted.shape
    de = wg.shape[2]
    n_blocks = n_rows // MOE_TB
    wsel = lambda b, be, nu: (be[b], 0, 0)
    grid_spec = pltpu.PrefetchScalarGridSpec(
        num_scalar_prefetch=2,
        grid=(n_blocks,),
        in_specs=[
            pl.BlockSpec((MOE_TB, d), lambda b, be, nu: (b, 0)),
            pl.BlockSpec((MOE_TB, 1), lambda b, be, nu: (b, 0)),
            pl.BlockSpec((None, d, de), wsel),
            pl.BlockSpec((None, d, de), wsel),
            pl.BlockSpec((None, de, d), wsel),
        ],
        out_specs=pl.BlockSpec((MOE_TB, d), lambda b, be, nu: (b, 0)),
    )
    return pl.pallas_call(
        _expert_kernel,
        grid_spec=grid_spec,
        out_shape=jax.ShapeDtypeStruct((n_rows, d), F32),
        compiler_params=_cparams(("arbitrary",)),
        name="experts",
    )(block_e, n_used, x_sorted, gate_sorted, wg, wu, wd)


def _ffn_kernel(x_ref, wg_ref, wu_ref, wd_ref, y_ref):
    x = x_ref[...]
    g = jnp.dot(x, wg_ref[...], preferred_element_type=F32)
    u = jnp.dot(x, wu_ref[...], preferred_element_type=F32)
    a = (g * jax.nn.sigmoid(g) * u).astype(BF16)
    y_ref[...] = jnp.dot(a, wd_ref[...], preferred_element_type=F32)


def _ffn_call(h, wg, wu, wd):
    m, d = h.shape
    de = wg.shape[1]
    tm = _pick_tile(m, FFN_TM)
    const2 = lambda i: (0, 0)
    return pl.pallas_call(
        _ffn_kernel,
        grid=(m // tm,),
        in_specs=[pl.BlockSpec((tm, d), lambda i: (i, 0)), pl.BlockSpec((d, de), const2),
                  pl.BlockSpec((d, de), const2), pl.BlockSpec((de, d), const2)],
        out_specs=pl.BlockSpec((tm, d), lambda i: (i, 0)),
        out_shape=jax.ShapeDtypeStruct((m, d), F32),
        compiler_params=_cparams(("parallel",)),
        name="shared_ffn",
    )(h, wg, wu, wd)


def _layer_norm(x, gain=None, bias=None):
    xf = x.astype(F32)
    mu = jnp.mean(xf, axis=-1, keepdims=True)
    var = jnp.mean(jnp.square(xf - mu), axis=-1, keepdims=True)
    y = (xf - mu) * lax.rsqrt(var + LN_EPS)
    if gain is not None:
        y = y * gain.astype(F32) + bias.astype(F32)
    return y.astype(x.dtype)


def _heads(t, n):
    return t.reshape(*t.shape[:-1], n, t.shape[-1] // n)


def _rope_1d(x, pos):
    half = x.shape[-1] // 2
    inv = ROPE_BASE ** (-2.0 * jnp.arange(half, dtype=F32) / x.shape[-1])
    ang = pos[:, None] * inv[None, :]
    cos = jnp.cos(ang)[None, :, None, :].astype(x.dtype)
    sin = jnp.sin(ang)[None, :, None, :].astype(x.dtype)
    x1, x2 = x[..., :half], x[..., half:]
    return jnp.concatenate([x1 * cos - x2 * sin, x2 * cos + x1 * sin], axis=-1)


def _axial_rope(x, row, col):
    half = x.shape[-1] // 2
    return jnp.concatenate([_rope_1d(x[..., :half], row), _rope_1d(x[..., half:], col)], axis=-1)


def _context_attention(q, k, v, sink=None):
    bsz, n, hq, dh = q.shape
    hkv = k.shape[2]
    grp = hq // hkv
    qg = q.reshape(bsz, n, hkv, grp, dh)
    s = jnp.einsum('bnkgd,bmkd->bkgnm', qg, k).astype(F32) * dh ** -0.5
    if sink is not None:
        sk = jnp.broadcast_to(sink.astype(F32).reshape(1, hkv, grp, 1, 1), (*s.shape[:-1], 1))
        p = jax.nn.softmax(jnp.concatenate([s, sk], axis=-1), axis=-1)[..., :-1]
    else:
        p = jax.nn.softmax(s, axis=-1)
    o = jnp.einsum('bkgnm,bmkd->bnkgd', p.astype(v.dtype), v)
    return o.reshape(bsz, n, hq * dh)


def _neighbourhood_attention(q, k, v, kc, vc, rpb):
    bsz, seq, nh, dh = q.shape
    rows = seq // GRID_W
    kh = min(NA_KH, rows)
    ncb = GRID_W // NA_KW
    scale = dh ** -0.5
    r = jnp.arange(rows)
    row_idx = jnp.clip(r - kh // 2, 0, rows - kh)[:, None] + jnp.arange(kh)[None, :]
    qcol = jnp.arange(ncb)[:, None] * NA_KW + jnp.arange(NA_KW)[None, :]
    kcol = jnp.clip(jnp.arange(ncb) * NA_KW - NA_KW // 2, 0, GRID_W - NA_KBW)[:, None] + jnp.arange(NA_KBW)[None, :]
    cstart = jnp.clip(qcol - NA_KW // 2, 0, GRID_W - NA_KW)
    valid = (kcol[:, None, :] >= cstart[:, :, None]) & (kcol[:, None, :] < cstart[:, :, None] + NA_KW)
    coff = jnp.clip(kcol[:, None, :] - qcol[:, :, None], 1 - NA_KW, NA_KW - 1) + NA_KW - 1
    roff = row_idx - r[:, None] + NA_KH - 1
    bias = rpb[:, roff[:, None, None, :, None], coff[None, :, :, None, :]].astype(F32)
    bias = jnp.where(valid[None, None, :, :, None, :], bias, -jnp.inf).transpose(1, 2, 3, 0, 4, 5)
    ri = row_idx[:, :, None, None]
    ci = kcol[None, None]
    kg = k.reshape(bsz, rows, GRID_W, nh, dh)[:, ri, ci]
    vg = v.reshape(bsz, rows, GRID_W, nh, dh)[:, ri, ci]
    qg = q.reshape(bsz, rows, ncb, NA_KW, nh, dh)
    s_nb = jnp.einsum('brmqhd,brimjhd->brmqhij', qg, kg).astype(F32) * scale + bias
    s_ctx = jnp.einsum('brmqhd,bnhd->brmqhn', qg, kc).astype(F32) * scale
    nb = kh * NA_KBW
    logits = jnp.concatenate([s_nb.reshape(*s_nb.shape[:5], nb), s_ctx], axis=-1)
    p = jax.nn.softmax(logits, axis=-1).astype(v.dtype)
    o = (jnp.einsum('brmqhij,brimjhd->brmqhd', p[..., :nb].reshape(s_nb.shape), vg)
         + jnp.einsum('brmqhn,bnhd->brmqhd', p[..., nb:], vc))
    return o.reshape(bsz, seq, nh * dh)


def _window_attention(q, k, v, kc, vc, sink):
    bsz, seq, hq, dh = q.shape
    hkv = k.shape[2]
    grp = hq // hkv
    nblk = seq // SWA_BLOCK
    qb = q.reshape(bsz, nblk, SWA_BLOCK, hkv, grp, dh)

    def band(t):
        tb = jnp.pad(t.reshape(bsz, nblk, SWA_BLOCK, hkv, dh), ((0, 0), (1, 1), (0, 0), (0, 0), (0, 0)))
        return jnp.concatenate([tb[:, :-2], tb[:, 1:-1], tb[:, 2:]], axis=2)

    kb, vb = band(k), band(v)
    qpos = jnp.arange(nblk)[:, None] * SWA_BLOCK + jnp.arange(SWA_BLOCK)[None, :]
    kpos = (jnp.arange(nblk)[:, None] - 1) * SWA_BLOCK + jnp.arange(3 * SWA_BLOCK)[None, :]
    valid = ((jnp.abs(qpos[:, :, None] - kpos[:, None, :]) <= SWA_WINDOW)
             & (kpos[:, None, :] >= 0) & (kpos[:, None, :] < seq))
    scale = dh ** -0.5
    s_win = jnp.einsum('bnqkgd,bnjkd->bnkgqj', qb, kb).astype(F32) * scale
    s_win = jnp.where(valid[None, :, None, None], s_win, -jnp.inf)
    s_ctx = jnp.einsum('bnqkgd,bmkd->bnkgqm', qb, kc).astype(F32) * scale
    s_sink = jnp.broadcast_to(sink.astype(F32).reshape(1, 1, hkv, grp, 1, 1), (*s_win.shape[:-1], 1))
    p = jax.nn.softmax(jnp.concatenate([s_win, s_ctx, s_sink], axis=-1), axis=-1).astype(v.dtype)
    nw = 3 * SWA_BLOCK
    nc = kc.shape[1]
    o = (jnp.einsum('bnkgqj,bnjkd->bnqkgd', p[..., :nw], vb)
         + jnp.einsum('bnkgqm,bmkd->bnqkgd', p[..., nw:nw + nc], vc))
    return o.reshape(bsz, seq, hq * dh)


def _short_conv(u, gate_b, gate_c, w):
    ch = u.shape[-1]
    z = lax.conv_general_dilated(gate_c * u, w[:, None, :], window_strides=(1,),
                                 padding=((SC_KSIZE // 2, SC_KSIZE // 2),),
                                 dimension_numbers=('NWC', 'WIO', 'NWC'), feature_group_count=ch)
    return gate_b * z


def _gla_log_decay(z, w2, b):
    zf, zb = jnp.split(z, 2, axis=-1)
    lf = jax.nn.log_sigmoid((zf @ w2[0] + b[0]).astype(F32)) / GLA_TAU
    lb = jax.nn.log_sigmoid((zb @ w2[1] + b[1]).astype(F32)) / GLA_TAU
    return _heads(lf, GLA_HEADS), _heads(lb, GLA_HEADS)


def _gla_scan(q, k, v, log_a, s0):
    bsz, seq, nh, dk = q.shape
    nc = seq // GLA_CHUNK

    def chunks(t):
        return t.astype(F32).reshape(bsz, nc, GLA_CHUNK, nh, t.shape[-1])

    qf = chunks(q) * dk ** -0.5
    kf = chunks(k)
    vf = chunks(v)
    cum = jnp.cumsum(chunks(log_a), axis=2)
    last = cum[:, :, -1:]
    lower = jnp.tril(jnp.ones((GLA_CHUNK, GLA_CHUNK), dtype=bool))
    decay = jnp.exp(jnp.where(lower[:, :, None, None], cum[:, :, :, None] - cum[:, :, None, :], -jnp.inf))
    attn = jnp.einsum('bcthd,bcshd,bctshd->bchts', qf, kf, decay)
    o_intra = jnp.einsum('bchts,bcshv->bcthv', attn, vf)
    upd = jnp.einsum('bcshk,bcshv->bchkv', kf * jnp.exp(last - cum), vf)
    gdec = jnp.exp(last[:, :, 0])

    def step(state, inp):
        g, u = inp
        return g[..., None] * state + u, state

    s_fin, s_in = lax.scan(step, s0, (jnp.moveaxis(gdec, 1, 0), jnp.moveaxis(upd, 1, 0)))
    o_inter = jnp.einsum('bcthk,cbhkv->bcthv', qf * jnp.exp(cum), s_in)
    return (o_intra + o_inter).reshape(bsz, seq, nh, v.shape[-1]), s_fin


def _gla_bidirectional(q, k, v, la_f, la_b, qc, kc, vc, lac_f, lac_b):
    bsz, _, nh, dk = q.shape
    s0 = jnp.zeros((bsz, nh, dk, v.shape[-1]), F32)
    flip = lambda t: jnp.flip(t, axis=1)
    oc_f, sc_f = _gla_scan(qc, kc, vc, lac_f, s0)
    oc_b, sc_b = _gla_scan(flip(qc), flip(kc), flip(vc), flip(lac_b), s0)
    o_f, _ = _gla_scan(q, k, v, la_f, sc_f)
    o_b, _ = _gla_scan(flip(q), flip(k), flip(v), flip(la_b), sc_b)
    return o_f + flip(o_b), oc_f + flip(oc_b)


def _gla_out(o, g, gain):
    on = o * lax.rsqrt(jnp.mean(jnp.square(o), axis=-1, keepdims=True) + LN_EPS) * gain.astype(F32)
    return (on.astype(g.dtype) * jax.nn.silu(g.reshape(o.shape))).reshape(g.shape)


def _route(h, w_router, router_bias):
    n = h.shape[0]
    scores = jax.nn.sigmoid((h @ w_router).astype(F32))
    sel = scores + router_bias.astype(F32)
    grp = sel.reshape(n, N_GROUPS, N_EXPERTS // N_GROUPS)
    grp_score = jnp.sum(lax.top_k(grp, 2)[0], axis=-1)
    _, gidx = lax.top_k(grp_score, TOPK_GROUPS)
    gmask = jnp.sum(jax.nn.one_hot(gidx, N_GROUPS, dtype=F32), axis=-2) > 0
    emask = jnp.repeat(gmask, N_EXPERTS // N_GROUPS, axis=-1)
    _, eidx = lax.top_k(jnp.where(emask, sel, -jnp.inf), TOP_K)
    wts = jnp.take_along_axis(scores, eidx, axis=-1)
    wts = wts / jnp.sum(wts, axis=-1, keepdims=True) * ROUTED_SCALE
    return eidx, wts


def _dispatch(eidx, wts):
    n = eidx.shape[0]
    nk = n * TOP_K
    flat_e = eidx.reshape(-1)
    flat_tok = jnp.repeat(jnp.arange(n, dtype=jnp.int32), TOP_K)
    flat_w = wts.reshape(-1)
    order = jnp.argsort(flat_e)
    sorted_e = flat_e[order]
    counts = jnp.bincount(flat_e, length=N_EXPERTS)
    padded = (counts + MOE_TB - 1) // MOE_TB * MOE_TB
    pad_end = jnp.cumsum(padded)
    pad_start = pad_end - padded
    start = jnp.cumsum(counts) - counts
    dest = pad_start[sorted_e] + jnp.arange(nk) - start[sorted_e]
    n_blocks = (nk + N_EXPERTS * (MOE_TB - 1) + MOE_TB - 1) // MOE_TB
    n_rows = n_blocks * MOE_TB
    tok_buf = jnp.full((n_rows,), n, jnp.int32).at[dest].set(flat_tok[order])
    gate_buf = jnp.zeros((n_rows,), F32).at[dest].set(flat_w[order])
    block_e = jnp.minimum(jnp.searchsorted(pad_end, jnp.arange(n_blocks) * MOE_TB, side='right'), N_EXPERTS - 1)
    n_used = (pad_end[-1] // MOE_TB).astype(jnp.int32).reshape(1)
    return tok_buf, gate_buf, block_e.astype(jnp.int32), n_used


def _moe(h_bf16, w_router, router_bias, wg, wu, wd, wsg, wsu, wsd):
    n, d = h_bf16.shape
    eidx, wts = _route(h_bf16.astype(F32), w_router, router_bias)
    tok_buf, gate_buf, block_e, n_used = _dispatch(eidx, wts)
    hp = jnp.concatenate([h_bf16, jnp.zeros((1, d), BF16)], axis=0)
    x_sorted = hp[tok_buf]
    y = _expert_call(block_e, n_used, x_sorted, gate_buf[:, None], wg, wu, wd)
    routed = jax.ops.segment_sum(y, tok_buf, num_segments=n + 1)[:n]
    return routed + _ffn_call(h_bf16, wsg, wsu, wsd)


def _pack_w_in(w, d):
    gw = d // 4
    cuts = np.cumsum([gw, gw, gw, gw, gw, gw, gw // 2, gw // 2, gw, gw, 2 * GLA_RANK, gw, gw // 4, gw // 4])[:-1].tolist()
    (na_q, na_k, na_v, sc_u, sc_b, sc_c, gl_q, gl_k, gl_v, gl_g, gl_z, sw_q, sw_k, sw_v) = jnp.split(w, cuts, axis=1)
    zpad = jnp.zeros((d, V7X_LANES - 2 * GLA_RANK), w.dtype)
    cols = [na_q, na_k, na_v, sc_u, sc_b, sc_c, gl_v, gl_g, sw_q, gl_q, gl_k, sw_k, sw_v, gl_z, zpad]
    packed = jnp.concatenate(cols, axis=1)
    pad = (-packed.shape[1]) % PROJ_TN
    packed = jnp.pad(packed, ((0, 0), (0, pad)))
    return packed.astype(BF16)


def _unpack_p(p16, p32, d):
    gw = d // 4
    f = lambda t: t.astype(F32)
    o = {}
    o['na_q'], o['na_k'], o['na_v'] = f(p16[..., :gw]), f(p16[..., gw:2 * gw]), f(p16[..., 2 * gw:3 * gw])
    c = 0
    for name, wdt in (('sc_u', gw), ('sc_b', gw), ('sc_c', gw), ('gl_v', gw), ('gl_g', gw), ('sw_q', gw),
                      ('gl_q', gw // 2), ('gl_k', gw // 2), ('sw_k', gw // 4), ('sw_v', gw // 4), ('gl_z', 2 * GLA_RANK)):
        o[name] = p32[..., c:c + wdt]
        c += wdt
    return o


def _token_mixers(pl_, pc_, rpb, conv_w, gla_w2, gla_b, gla_norm_g, sink, with_ctx_out):
    seq = pl_['na_q'].shape[1]
    nh = pl_['na_q'].shape[-1] // HEAD_DIM
    y_na = _neighbourhood_attention(_heads(pl_['na_q'], nh), _heads(pl_['na_k'], nh), _heads(pl_['na_v'], nh),
                                    _heads(pc_['na_k'], nh), _heads(pc_['na_v'], nh), rpb)
    y_sc = _short_conv(pl_['sc_u'], pl_['sc_b'], pl_['sc_c'], conv_w)
    la_f, la_b = _gla_log_decay(pl_['gl_z'], gla_w2, gla_b)
    lac_f, lac_b = _gla_log_decay(pc_['gl_z'], gla_w2, gla_b)
    o_gl, oc_gl = _gla_bidirectional(_heads(pl_['gl_q'], GLA_HEADS), _heads(pl_['gl_k'], GLA_HEADS),
                                     _heads(pl_['gl_v'], GLA_HEADS), la_f, la_b,
                                     _heads(pc_['gl_q'], GLA_HEADS), _heads(pc_['gl_k'], GLA_HEADS),
                                     _heads(pc_['gl_v'], GLA_HEADS), lac_f, lac_b)
    y_gl = _gla_out(o_gl, pl_['gl_g'], gla_norm_g)
    t = jnp.arange(seq)
    row = (t // GRID_W).astype(F32)
    col = (t % GRID_W).astype(F32)
    y_sw = _window_attention(_axial_rope(_heads(pl_['sw_q'], nh), row, col),
                             _axial_rope(_heads(pl_['sw_k'], SWA_KV_HEADS), row, col),
                             _heads(pl_['sw_v'], SWA_KV_HEADS), _heads(pc_['sw_k'], SWA_KV_HEADS),
                             _heads(pc_['sw_v'], SWA_KV_HEADS), sink)
    y_lat = jnp.concatenate([y_na, y_sc, y_gl, y_sw], axis=-1)
    if not with_ctx_out:
        return y_lat, None
    y_ctx = jnp.concatenate([
        _context_attention(_heads(pc_['na_q'], nh), _heads(pc_['na_k'], nh), _heads(pc_['na_v'], nh)),
        _short_conv(pc_['sc_u'], pc_['sc_b'], pc_['sc_c'], conv_w),
        _gla_out(oc_gl, pc_['gl_g'], gla_norm_g),
        _context_attention(_heads(pc_['sw_q'], nh), _heads(pc_['sw_k'], SWA_KV_HEADS),
                           _heads(pc_['sw_v'], SWA_KV_HEADS), sink),
    ], axis=-1)
    return y_lat, y_ctx


def kernel(x, c, ctx, c_ctx, w_ada, b_ada, w_in, na_rpb, conv_w, gla_w2, gla_b, gla_norm_g, swa_sink, w_out,
           ln1_g, ln1_b, w_router, router_bias, w_exp_gate, w_exp_up, w_exp_down, w_sh_gate, w_sh_up, w_sh_down,
           ln2_g, ln2_b):
    bsz, seq, d = x.shape
    n_ctx_len = ctx.shape[1]
    depth = w_in.shape[0]
    alpha = (2 * depth) ** 0.25
    n16 = 3 * (d // 4)
    x2 = x.reshape(bsz * seq, d)
    hc2 = ctx.reshape(bsz * n_ctx_len, d)
    for layer in range(depth):
        last = layer == depth - 1
        mod = jax.nn.silu(c) @ w_ada[layer] + b_ada[layer]
        mod_c = jax.nn.silu(c_ctx) @ w_ada[layer] + b_ada[layer]
        sh1, sc1, g1, sh2, sc2, g2 = [t[:, None, :] for t in jnp.split(mod, 6, axis=-1)]
        sh1c, sc1c, g1c, sh2c, sc2c, g2c = [t[None, None, :] for t in jnp.split(mod_c, 6, axis=-1)]
        w_p = _pack_w_in(w_in[layer], d)
        p16, p32 = _proj_call(x2, sh1, sc1, w_p, rows_per_group=seq, n16=n16)
        pc16, pc32 = _proj_call(hc2, sh1c, sc1c, w_p, rows_per_group=bsz * n_ctx_len, n16=n16)
        pl_ = _unpack_p(p16.reshape(bsz, seq, -1), p32.reshape(bsz, seq, -1), d)
        pc_ = _unpack_p(pc16.reshape(bsz, n_ctx_len, -1), pc32.reshape(bsz, n_ctx_len, -1), d)
        y_lat, y_ctx = _token_mixers(pl_, pc_, na_rpb[layer], conv_w[layer], gla_w2[layer], gla_b[layer],
                                     gla_norm_g[layer], swa_sink[layer], not last)
        w_o = w_out[layer].astype(BF16)
        x2, f_in = _outproj_call([y_lat.reshape(bsz * seq, d)], x2, g1, w_o, ln1_g[layer], ln1_b[layer], sh2, sc2,
                                 rows_per_group=seq, alpha=alpha)
        moe_w = (w_router[layer], router_bias[layer], w_exp_gate[layer].astype(BF16), w_exp_up[layer].astype(BF16),
                 w_exp_down[layer].astype(BF16), w_sh_gate[layer].astype(BF16), w_sh_up[layer].astype(BF16),
                 w_sh_down[layer].astype(BF16))
        if last:
            f_lat = _moe(f_in, *moe_w)
        else:
            hc2, fc_in = _outproj_call([y_ctx.reshape(bsz * n_ctx_len, d)], hc2, g1c, w_o, ln1_g[layer], ln1_b[layer],
                                       sh2c, sc2c, rows_per_group=bsz * n_ctx_len, alpha=alpha)
            f_all = _moe(jnp.concatenate([fc_in, f_in], axis=0), *moe_w)
            nc = bsz * n_ctx_len
            hc2 = _layer_norm(alpha * hc2 + g2c[0] * f_all[:nc], ln2_g[layer], ln2_b[layer])
            f_lat = f_all[nc:]
        g2_rows = jnp.repeat(g2[:, 0, :], seq, axis=0)
        x2 = _layer_norm(alpha * x2 + g2_rows * f_lat, ln2_g[layer], ln2_b[layer])
    return x2.reshape(bsz, seq, d)
```

```python
import functools

import jax
import jax.numpy as jnp
import numpy as np
from jax import lax
from jax.experimental import pallas as pl
from jax.experimental.pallas import tpu as pltpu

GRID_W = 64
HEAD_DIM = 64
NA_KH = 8
NA_KW = 16
SC_KSIZE = 3
GLA_HEADS = 4
GLA_RANK = 16
GLA_TAU = 16.0
SWA_KV_HEADS = 2
SWA_BLOCK = 128
ROPE_BASE = 10000.0
N_EXPERTS = 64
TOP_K = 8
N_GROUPS = 8
TOPK_GROUPS = 4
ROUTED_SCALE = 2.5
LN_EPS = 1e-6

V7X_LANES = 128
V7X_VMEM_LIMIT_BYTES = 48 * 1024 * 1024

PROJ_TM = 1024
PROJ_TN = 512
OUT_TM = 256
MOE_TB = 256
ROUTER_TILE = 512
DISPATCH_TILE = 256
COMBINE_TILE = 128
CONV_HALO = 8
GLA_C = 64
GLA_TILE = 512
ADA_TN = 2048

F32 = jnp.float32
BF16 = jnp.bfloat16
PAIR_W = 2 * HEAD_DIM
ATTN_SCALE = HEAD_DIM ** -0.5
_NT = (((1,), (1,)), ((), ()))
_TN = (((0,), (0,)), ((), ()))


def _cparams(sem):
    return pltpu.CompilerParams(dimension_semantics=sem, vmem_limit_bytes=V7X_VMEM_LIMIT_BYTES)


def _pick_tile(m, preferred):
    t = preferred
    while t > 8 and m % t:
        t //= 2
    assert m % t == 0, (m, preferred)
    return t


def _ln_rows(x):
    mu = jnp.mean(x, axis=-1, keepdims=True)
    xc = x - mu
    var = jnp.mean(xc * xc, axis=-1, keepdims=True)
    return xc * lax.rsqrt(var + LN_EPS)


def _silu(x):
    return x * jax.nn.sigmoid(x)


def _ada_kernel(c_ref, w_ref, b_ref, o_ref):
    a = _silu(c_ref[...]).astype(BF16)
    o_ref[...] = jnp.dot(a, w_ref[...].astype(BF16), preferred_element_type=F32) + b_ref[...]


def _ada_call(c_rows, w_ada, b_ada):
    r, d = c_rows.shape
    n = w_ada.shape[1]
    tn = _pick_tile(n, ADA_TN)
    return pl.pallas_call(
        _ada_kernel,
        grid=(n // tn,),
        in_specs=[pl.BlockSpec((r, d), lambda j: (0, 0)), pl.BlockSpec((d, tn), lambda j: (0, j)),
                  pl.BlockSpec((1, tn), lambda j: (0, j))],
        out_specs=pl.BlockSpec((r, tn), lambda j: (0, j)),
        out_shape=jax.ShapeDtypeStruct((r, n), F32),
        compiler_params=_cparams(("parallel",)),
        name="ada_mod",
    )(c_rows, w_ada, b_ada.reshape(1, n))


def _proj_kernel(x_ref, sh_ref, sc_ref, w_ref, o16_ref, o32_ref, xn_ref, *, nb16):
    j = pl.program_id(1)

    @pl.when(j == 0)
    def _():
        y = _ln_rows(x_ref[...]) * (1.0 + sc_ref[0]) + sh_ref[0]
        xn_ref[...] = y.astype(BF16)

    acc = jnp.dot(xn_ref[...], w_ref[...], preferred_element_type=F32)

    @pl.when(j < nb16)
    def _():
        o16_ref[...] = acc.astype(BF16)

    @pl.when(j >= nb16)
    def _():
        o32_ref[...] = acc


def _proj_call(x2d, shift, scale, w_packed, *, rows_per_group, n16):
    m, d = x2d.shape
    ntot = w_packed.shape[1]
    tm = min(PROJ_TM, rows_per_group)
    assert m % tm == 0 and rows_per_group % tm == 0 and ntot % PROJ_TN == 0 and n16 % PROJ_TN == 0
    nb16 = n16 // PROJ_TN
    nb = ntot // PROJ_TN
    grp = lambda i, j: ((i * tm) // rows_per_group, 0, 0)
    return pl.pallas_call(
        functools.partial(_proj_kernel, nb16=nb16),
        grid=(m // tm, nb),
        in_specs=[
            pl.BlockSpec((tm, d), lambda i, j: (i, 0)),
            pl.BlockSpec((1, 1, d), grp),
            pl.BlockSpec((1, 1, d), grp),
            pl.BlockSpec((d, PROJ_TN), lambda i, j: (0, j)),
        ],
        out_specs=[
            pl.BlockSpec((tm, PROJ_TN), lambda i, j: (i, jnp.minimum(j, nb16 - 1))),
            pl.BlockSpec((tm, PROJ_TN), lambda i, j: (i, jnp.maximum(j - nb16, 0))),
        ],
        out_shape=[jax.ShapeDtypeStruct((m, n16), BF16), jax.ShapeDtypeStruct((m, ntot - n16), F32)],
        scratch_shapes=[pltpu.VMEM((tm, d), BF16)],
        compiler_params=_cparams(("parallel", "arbitrary")),
        name="proj",
    )(x2d, shift, scale, w_packed)


def _pack_halves(h):
    c = h.shape[1] // 2
    lo = pltpu.bitcast(h[:, :c].astype(BF16).astype(F32), jnp.uint32)
    hi = pltpu.bitcast(h[:, c:].astype(BF16).astype(F32), jnp.uint32)
    return lax.shift_right_logical(lo, jnp.uint32(16)) | (hi & jnp.uint32(0xFFFF0000))


def _unpack_halves(w):
    lo = pltpu.bitcast(lax.shift_left(w, jnp.uint32(16)), F32)
    hi = pltpu.bitcast(w & jnp.uint32(0xFFFF0000), F32)
    return jnp.concatenate([lo, hi], axis=1).astype(BF16)


def _outproj_kernel(y0_ref, y1_ref, y2_ref, y3_ref, x_ref, g1_ref, w_ref, lng_ref, lnb_ref,
                    sh2_ref, sc2_ref, xo_ref, h_ref, hp_ref, *, alpha):
    gw = y0_ref.shape[1]
    acc = jnp.dot(y0_ref[...], w_ref[0:gw, :], preferred_element_type=F32)
    acc += jnp.dot(y1_ref[...], w_ref[gw:2 * gw, :], preferred_element_type=F32)
    acc += jnp.dot(y2_ref[...], w_ref[2 * gw:3 * gw, :], preferred_element_type=F32)
    acc += jnp.dot(y3_ref[...], w_ref[3 * gw:4 * gw, :], preferred_element_type=F32)
    r = alpha * x_ref[...] + g1_ref[0] * acc
    xn = _ln_rows(r) * lng_ref[...] + lnb_ref[...]
    xo_ref[...] = xn
    h = _ln_rows(xn) * (1.0 + sc2_ref[0]) + sh2_ref[0]
    h_ref[...] = h.astype(BF16)
    hp_ref[...] = _pack_halves(h)


def _outproj_call(ys, x2d, gate1, w_out_bf16, ln_g, ln_b, shift2, scale2, *, rows_per_group, alpha):
    m, d = x2d.shape
    gw = d // 4
    tm = min(OUT_TM, rows_per_group)
    assert m % tm == 0 and rows_per_group % tm == 0
    grp = lambda i: ((i * tm) // rows_per_group, 0, 0)
    row = lambda i: (i, 0)
    const2 = lambda i: (0, 0)
    return pl.pallas_call(
        functools.partial(_outproj_kernel, alpha=alpha),
        grid=(m // tm,),
        in_specs=[pl.BlockSpec((tm, gw), row)] * 4 + [
            pl.BlockSpec((tm, d), row),
            pl.BlockSpec((1, 1, d), grp),
            pl.BlockSpec((d, d), const2, pipeline_mode=pl.Buffered(1)),
            pl.BlockSpec((1, d), const2),
            pl.BlockSpec((1, d), const2),
            pl.BlockSpec((1, 1, d), grp),
            pl.BlockSpec((1, 1, d), grp),
        ],
        out_specs=[pl.BlockSpec((tm, d), row), pl.BlockSpec((tm, d), row), pl.BlockSpec((tm, d // 2), row)],
        out_shape=[jax.ShapeDtypeStruct((m, d), F32), jax.ShapeDtypeStruct((m, d), BF16),
                   jax.ShapeDtypeStruct((m, d // 2), jnp.uint32)],
        compiler_params=_cparams(("parallel",)),
        name="outproj",
    )(*ys, x2d, gate1, w_out_bf16, ln_g.reshape(1, d), ln_b.reshape(1, d), shift2, scale2)


def _stack_pair(q2):
    lane = lax.broadcasted_iota(jnp.int32, q2.shape, 1)
    lo = jnp.where(lane < HEAD_DIM, q2, 0.0)
    hi = jnp.where(lane >= HEAD_DIM, q2, 0.0)
    return jnp.concatenate([lo, hi], axis=0).astype(BF16)


def _unstack_pair(o):
    n = o.shape[0] // 2
    lane = lax.broadcasted_iota(jnp.int32, (n, o.shape[1]), 1)
    return jnp.where(lane < HEAD_DIM, o[:n], o[n:])


def _pair_softmax_av(qs, ks, vs, biases, masks, sink_col):
    ss = []
    for k, bia, msk in zip(ks, biases, masks):
        s = lax.dot_general(qs, k, _NT, preferred_element_type=F32)
        if bia is not None:
            s = s + bia
        if msk is not None:
            s = jnp.where(msk, s, -jnp.inf)
        ss.append(s)
    m = jnp.max(ss[0], axis=-1, keepdims=True)
    for s in ss[1:]:
        m = jnp.maximum(m, jnp.max(s, axis=-1, keepdims=True))
    if sink_col is not None:
        m = jnp.maximum(m, sink_col)
        l = jnp.exp(sink_col - m)
    else:
        l = jnp.zeros_like(m)
    o = None
    for s, v in zip(ss, vs):
        e = jnp.exp(s - m)
        l = l + jnp.sum(e, axis=-1, keepdims=True)
        pv = jnp.dot(e.astype(BF16), v, preferred_element_type=F32)
        o = pv if o is None else o + pv
    return o / l


def _na_kernel(q_ref, k_ref, v_ref, kc_ref, vc_ref, bias_ref, o_ref, *, rows, kh):
    r = pl.program_id(1)
    rs = jnp.clip(r - kh // 2, 0, rows - kh)
    start = pl.multiple_of(rs * GRID_W, GRID_W)
    nwin = kh * GRID_W
    for p in range(q_ref.shape[1] // PAIR_W):
        sl = slice(p * PAIR_W, (p + 1) * PAIR_W)
        qs = _stack_pair(q_ref[:, sl].astype(F32) * ATTN_SCALE)
        kw = k_ref[pl.ds(start, nwin), sl]
        vw = v_ref[pl.ds(start, nwin), sl]
        o = _pair_softmax_av(qs, [kw, kc_ref[:, sl]], [vw, vc_ref[:, sl]], [bias_ref[p], None], [None, None], None)
        o_ref[:, sl] = _unstack_pair(o).astype(o_ref.dtype)


def _na_bias_table(rpb, kh):
    nh = rpb.shape[0]
    c = jnp.arange(GRID_W)
    cstart = jnp.clip(c - NA_KW // 2, 0, GRID_W - NA_KW)
    valid = (c[None, :] >= cstart[:, None]) & (c[None, :] < cstart[:, None] + NA_KW)
    coff = jnp.clip(c[None, :] - c[:, None], 1 - NA_KW, NA_KW - 1) + NA_KW - 1
    roff = jnp.arange(kh)[None, :] - jnp.arange(kh)[:, None] + NA_KH - 1
    bias = rpb[:, roff[:, :, None, None], coff[None, None, :, :]].astype(F32)
    bias = jnp.where(valid[None, None, None], bias, -jnp.inf)
    bias = bias.transpose(1, 0, 3, 2, 4)
    return bias.reshape(kh, nh // 2, 2 * GRID_W, kh * GRID_W)


def _na_call(p16, pc16, bias_tab, *, bsz, seq, n_ctx):
    gw = p16.shape[1] // 3
    rows = seq // GRID_W
    kh = bias_tab.shape[0]
    delta = lambda b, r: (r - jnp.clip(r - kh // 2, 0, rows - kh), 0, 0, 0)
    once = pl.Buffered(1)
    return pl.pallas_call(
        functools.partial(_na_kernel, rows=rows, kh=kh),
        grid=(bsz, rows),
        in_specs=[
            pl.BlockSpec((GRID_W, gw), lambda b, r: (b * rows + r, 0)),
            pl.BlockSpec((seq, gw), lambda b, r: (b, 1), pipeline_mode=once),
            pl.BlockSpec((seq, gw), lambda b, r: (b, 2), pipeline_mode=once),
            pl.BlockSpec((n_ctx, gw), lambda b, r: (b, 1)),
            pl.BlockSpec((n_ctx, gw), lambda b, r: (b, 2)),
            pl.BlockSpec((None,) + bias_tab.shape[1:], delta),
        ],
        out_specs=pl.BlockSpec((GRID_W, gw), lambda b, r: (b * rows + r, 0)),
        out_shape=jax.ShapeDtypeStruct((bsz * seq, gw), BF16),
        compiler_params=_cparams(("parallel", "arbitrary")),
        name="na_attn",
    )(p16, p16, p16, pc16, pc16, bias_tab)


def _dup_heads(t):
    lane = lax.broadcasted_iota(jnp.int32, t.shape, 1)
    sw = pltpu.roll(t, HEAD_DIM, 1)
    return jnp.concatenate([jnp.where(lane < HEAD_DIM, t, sw), jnp.where(lane < HEAD_DIM, sw, t)], axis=1)


def _rope(t, cos, sa, sb):
    q = HEAD_DIM // 4
    return t * cos + pltpu.roll(t, q, 1) * sa + pltpu.roll(t, V7X_LANES - q, 1) * sb


def _swa_prep_kernel(q_ref, k_ref, v_ref, cos_ref, sa_ref, sb_ref, qo_ref, ko_ref, vo_ref, *, rope):
    k = k_ref[...]
    if rope:
        cos, sa, sb = cos_ref[...], sa_ref[...], sb_ref[...]
        k = _rope(k, cos, sa, sb)
    for p in range(q_ref.shape[1] // PAIR_W):
        sl = slice(p * PAIR_W, (p + 1) * PAIR_W)
        q = q_ref[:, sl]
        if rope:
            q = _rope(q, cos, sa, sb)
        qo_ref[:, sl] = (q * ATTN_SCALE).astype(BF16)
    ko_ref[...] = _dup_heads(k).astype(BF16)
    vo_ref[...] = _dup_heads(v_ref[...]).astype(BF16)


def _rope_tables(seq):
    t = jnp.arange(seq)
    row = (t // GRID_W).astype(F32)
    col = (t % GRID_W).astype(F32)
    quarter = HEAD_DIM // 4
    inv = ROPE_BASE ** (-2.0 * jnp.arange(quarter, dtype=F32) / (HEAD_DIM // 2))
    ang_r = row[:, None] * inv[None, :]
    ang_c = col[:, None] * inv[None, :]
    zero = jnp.zeros_like(ang_r)
    cos_h = jnp.concatenate([jnp.cos(ang_r)] * 2 + [jnp.cos(ang_c)] * 2, axis=1)
    sa_h = jnp.concatenate([zero, jnp.sin(ang_r), zero, jnp.sin(ang_c)], axis=1)
    sb_h = jnp.concatenate([-jnp.sin(ang_r), zero, -jnp.sin(ang_c), zero], axis=1)
    two = lambda a: jnp.concatenate([a, a], axis=1)
    return two(cos_h), two(sa_h), two(sb_h)


def _swa_prep_call(p32, tables, *, col_q, col_k, col_v, rows_per_seq, rope):
    m = p32.shape[0]
    gw = 4 * PAIR_W
    tm = _pick_tile(rows_per_seq, 512)
    nseq_tiles = rows_per_seq // tm
    tab = lambda i: (i % nseq_tiles, 0)
    return pl.pallas_call(
        functools.partial(_swa_prep_kernel, rope=rope),
        grid=(m // tm,),
        in_specs=[
            pl.BlockSpec((tm, gw), lambda i: (i, col_q // gw)),
            pl.BlockSpec((tm, PAIR_W), lambda i: (i, col_k // PAIR_W)),
            pl.BlockSpec((tm, PAIR_W), lambda i: (i, col_v // PAIR_W)),
            pl.BlockSpec((tm, PAIR_W), tab),
            pl.BlockSpec((tm, PAIR_W), tab),
            pl.BlockSpec((tm, PAIR_W), tab),
        ],
        out_specs=[pl.BlockSpec((tm, gw), lambda i: (i, 0)), pl.BlockSpec((tm, 2 * PAIR_W), lambda i: (i, 0)),
                   pl.BlockSpec((tm, 2 * PAIR_W), lambda i: (i, 0))],
        out_shape=[jax.ShapeDtypeStruct((m, gw), BF16), jax.ShapeDtypeStruct((m, 2 * PAIR_W), BF16),
                   jax.ShapeDtypeStruct((m, 2 * PAIR_W), BF16)],
        compiler_params=_cparams(("parallel",)),
        name="swa_prep",
    )(p32, p32, p32, *tables)


def _sink_col(sink_ref, p, n):
    row = lax.broadcasted_iota(jnp.int32, (2 * n, 1), 0)
    return jnp.where(row < n, sink_ref[2 * p], sink_ref[2 * p + 1])


def _swa_kernel(sink_ref, q_ref, kp_ref, kc_ref, kn_ref, vp_ref, vc_ref, vn_ref, kx_ref, vx_ref, o_ref, *, nblk):
    n = pl.program_id(1)
    blk = q_ref.shape[0]
    qi = lax.broadcasted_iota(jnp.int32, (2 * blk, blk), 0) % blk
    kj = lax.broadcasted_iota(jnp.int32, (2 * blk, blk), 1)
    m_prev = kj >= qi + jnp.where(n > 0, 0, blk)
    m_next = kj <= qi - jnp.where(n < nblk - 1, 0, blk)
    npairs = q_ref.shape[1] // PAIR_W
    for p in range(npairs):
        sl = slice(p * PAIR_W, (p + 1) * PAIR_W)
        g = p // (npairs // SWA_KV_HEADS)
        gs = slice(g * PAIR_W, (g + 1) * PAIR_W)
        qs = _stack_pair(q_ref[:, sl].astype(F32))
        o = _pair_softmax_av(
            qs, [kp_ref[:, gs], kc_ref[:, gs], kn_ref[:, gs], kx_ref[:, gs]],
            [vp_ref[:, gs], vc_ref[:, gs], vn_ref[:, gs], vx_ref[:, gs]],
            [None] * 4, [m_prev, None, m_next, None], _sink_col(sink_ref, p, blk))
        o_ref[:, sl] = _unstack_pair(o).astype(o_ref.dtype)


def _swa_call(sink, qr, kd, vd, kxd, vxd, *, bsz, seq, n_ctx):
    gw = qr.shape[1]
    kw = kd.shape[1]
    nblk = seq // SWA_BLOCK
    cur = lambda b, n: (b * nblk + n, 0)
    prev = lambda b, n: (b * nblk + jnp.maximum(n - 1, 0), 0)
    nxt = lambda b, n: (b * nblk + jnp.minimum(n + 1, nblk - 1), 0)
    cx = lambda b, n: (b, 0)
    return pl.pallas_call(
        functools.partial(_swa_kernel, nblk=nblk),
        grid=(bsz, nblk),
        in_specs=[pl.BlockSpec(memory_space=pltpu.SMEM),
                  pl.BlockSpec((SWA_BLOCK, gw), cur),
                  pl.BlockSpec((SWA_BLOCK, kw), prev), pl.BlockSpec((SWA_BLOCK, kw), cur), pl.BlockSpec((SWA_BLOCK, kw), nxt),
                  pl.BlockSpec((SWA_BLOCK, kw), prev), pl.BlockSpec((SWA_BLOCK, kw), cur), pl.BlockSpec((SWA_BLOCK, kw), nxt),
                  pl.BlockSpec((n_ctx, kw), cx), pl.BlockSpec((n_ctx, kw), cx)],
        out_specs=pl.BlockSpec((SWA_BLOCK, gw), cur),
        out_shape=jax.ShapeDtypeStruct((bsz * seq, gw), BF16),
        compiler_params=_cparams(("parallel", "arbitrary")),
        name="swa_attn",
    )(sink, qr, kd, kd, kd, vd, vd, vd, kxd, vxd)


def _ctx_attn_kernel(sink_ref, qa_ref, ka_ref, va_ref, qd_ref, kd_ref, vd_ref, oa_ref, od_ref):
    n = qa_ref.shape[0]
    npairs = qa_ref.shape[1] // PAIR_W
    for p in range(npairs):
        sl = slice(p * PAIR_W, (p + 1) * PAIR_W)
        qs = _stack_pair(qa_ref[:, sl].astype(F32) * ATTN_SCALE)
        o = _pair_softmax_av(qs, [ka_ref[:, sl]], [va_ref[:, sl]], [None], [None], None)
        oa_ref[:, sl] = _unstack_pair(o).astype(oa_ref.dtype)
    for p in range(npairs):
        sl = slice(p * PAIR_W, (p + 1) * PAIR_W)
        g = p // (npairs // SWA_KV_HEADS)
        gs = slice(g * PAIR_W, (g + 1) * PAIR_W)
        qs = _stack_pair(qd_ref[:, sl].astype(F32))
        o = _pair_softmax_av(qs, [kd_ref[:, gs]], [vd_ref[:, gs]], [None], [None], _sink_col(sink_ref, p, n))
        od_ref[:, sl] = _unstack_pair(o).astype(od_ref.dtype)


def _ctx_attn_call(sink, pc16, qx, kxd, vxd, *, bsz, n_ctx):
    gw = qx.shape[1]
    kw = kxd.shape[1]
    return pl.pallas_call(
        _ctx_attn_kernel,
        grid=(bsz,),
        in_specs=[pl.BlockSpec(memory_space=pltpu.SMEM),
                  pl.BlockSpec((n_ctx, gw), lambda b: (b, 0)), pl.BlockSpec((n_ctx, gw), lambda b: (b, 1)),
                  pl.BlockSpec((n_ctx, gw), lambda b: (b, 2)), pl.BlockSpec((n_ctx, gw), lambda b: (b, 0)),
                  pl.BlockSpec((n_ctx, kw), lambda b: (b, 0)), pl.BlockSpec((n_ctx, kw), lambda b: (b, 0))],
        out_specs=[pl.BlockSpec((n_ctx, gw), lambda b: (b, 0)), pl.BlockSpec((n_ctx, gw), lambda b: (b, 0))],
        out_shape=[jax.ShapeDtypeStruct((bsz * n_ctx, gw), BF16)] * 2,
        compiler_params=_cparams(("parallel",)),
        name="ctx_attn",
    )(sink, pc16, pc16, pc16, qx, kxd, vxd)


def _conv_kernel(u_ref, b_ref, c_ref, up_ref, cp_ref, un_ref, cn_ref, w_ref, o_ref, *, tiles_per_seq):
    i = pl.program_id(0)
    tm = u_ref.shape[0]
    pos = i % tiles_per_seq
    keep_prev = jnp.where(pos == 0, 0.0, 1.0)
    keep_next = jnp.where(pos == tiles_per_seq - 1, 0.0, 1.0)
    z = c_ref[...] * u_ref[...]
    z_prev = (cp_ref[...] * up_ref[...])[CONV_HALO - 1:CONV_HALO, :] * keep_prev
    z_next = (cn_ref[...] * un_ref[...])[0:1, :] * keep_next
    row = lax.broadcasted_iota(jnp.int32, z.shape, 0)
    zm1 = jnp.where(row == 0, z_prev, pltpu.roll(z, 1, 0))
    zp1 = jnp.where(row == tm - 1, z_next, pltpu.roll(z, tm - 1, 0))
    w = w_ref[...]
    o_ref[...] = (b_ref[...] * (w[0:1] * zm1 + w[1:2] * z + w[2:3] * zp1)).astype(o_ref.dtype)


def _conv_call(p32, conv_w, *, col_u, col_b, col_c, rows_per_seq):
    m = p32.shape[0]
    gw = conv_w.shape[1]
    tm = _pick_tile(rows_per_seq, 512)
    hb = tm // CONV_HALO
    n_halo = m // CONV_HALO
    cur = lambda col: (lambda i: (i, col // gw))
    prv = lambda col: (lambda i: (jnp.maximum(i * hb - 1, 0), col // gw))
    nxt = lambda col: (lambda i: (jnp.minimum((i + 1) * hb, n_halo - 1), col // gw))
    return pl.pallas_call(
        functools.partial(_conv_kernel, tiles_per_seq=rows_per_seq // tm),
        grid=(m // tm,),
        in_specs=[pl.BlockSpec((tm, gw), cur(col_u)), pl.BlockSpec((tm, gw), cur(col_b)), pl.BlockSpec((tm, gw), cur(col_c)),
                  pl.BlockSpec((CONV_HALO, gw), prv(col_u)), pl.BlockSpec((CONV_HALO, gw), prv(col_c)),
                  pl.BlockSpec((CONV_HALO, gw), nxt(col_u)), pl.BlockSpec((CONV_HALO, gw), nxt(col_c)),
                  pl.BlockSpec((SC_KSIZE, gw), lambda i: (0, 0))],
        out_specs=pl.BlockSpec((tm, gw), lambda i: (i, 0)),
        out_shape=jax.ShapeDtypeStruct((m, gw), BF16),
        compiler_params=_cparams(("parallel",)),
        name="short_conv",
    )(p32, p32, p32, p32, p32, p32, p32, conv_w)


def _gla_kernel(*refs, reverse, nt, fuse_out):
    if fuse_out:
        (q_ref, k_ref, v_ref, z_ref, w2_ref, gb_ref, s0_ref, of_ref, g_ref, gain_ref, o_ref, sfin_ref, st_ref) = refs
    else:
        (q_ref, k_ref, v_ref, z_ref, w2_ref, gb_ref, s0_ref, o_ref, sfin_ref, st_ref) = refs
    i = pl.program_id(1)

    @pl.when(i == 0)
    def _():
        st_ref[...] = s0_ref[...]

    tile = q_ref.shape[0]
    c_len = min(GLA_C, tile)
    dk2 = PAIR_W
    dv2 = v_ref.shape[1] // (q_ref.shape[1] // dk2)
    dv = dv2 // 2
    u = jnp.dot(z_ref[...].astype(BF16), w2_ref[...], preferred_element_type=F32) + gb_ref[...]
    la = (jnp.minimum(u, 0.0) - jnp.log1p(jnp.exp(-jnp.abs(u)))) * (1.0 / GLA_TAU)

    r_i = lax.broadcasted_iota(jnp.int32, (c_len, c_len), 0)
    c_i = lax.broadcasted_iota(jnp.int32, (c_len, c_len), 1)
    tri = (r_i <= c_i) if reverse else (r_i >= c_i)
    tri_bf = jnp.where(tri, 1.0, 0.0).astype(BF16)
    r2 = lax.broadcasted_iota(jnp.int32, (2 * c_len, c_len), 0) % c_len
    c2 = lax.broadcasted_iota(jnp.int32, (2 * c_len, c_len), 1)
    tri2 = (r2 <= c2) if reverse else (r2 >= c2)
    bd_r = lax.broadcasted_iota(jnp.int32, (dv2, dk2), 0) // dv
    bd_c = lax.broadcasted_iota(jnp.int32, (dv2, dk2), 1) // HEAD_DIM
    block_diag = bd_r == bd_c
    last_row = 0 if reverse else c_len - 1
    mid_row = c_len // 2

    n_chunks = tile // c_len
    order = range(n_chunks - 1, -1, -1) if reverse else range(n_chunks)
    for c in order:
        rows = slice(c * c_len, (c + 1) * c_len)
        la_c = la[rows]
        la_hi = la_c.astype(BF16)
        la_lo = (la_c - la_hi.astype(F32)).astype(BF16)
        cum = (jnp.dot(tri_bf, la_hi, preferred_element_type=F32)
               + jnp.dot(tri_bf, la_lo, preferred_element_type=F32))
        last = cum[last_row:last_row + 1]
        cmid = cum[mid_row:mid_row + 1]
        qc = q_ref[rows, :] * ATTN_SCALE
        kc = k_ref[rows, :]
        vc = v_ref[rows, :].astype(BF16)
        q_in = (qc * jnp.exp(cum)).astype(BF16)
        q_t = qc * jnp.exp(cum - cmid)
        k_t = (kc * jnp.exp(cmid - cum)).astype(BF16)
        k_p = (kc * jnp.exp(last - cum)).astype(BF16)
        g = jnp.exp(last)
        for p in range(q_ref.shape[1] // dk2):
            ls = slice(p * dk2, (p + 1) * dk2)
            a = lax.dot_general(_stack_pair(q_t[:, ls]), k_t[:, ls], _NT, preferred_element_type=F32)
            a = jnp.where(tri2, a, 0.0).astype(BF16)
            o0 = jnp.dot(a[:c_len], vc[:, p * dv2:p * dv2 + dv], preferred_element_type=F32)
            o1 = jnp.dot(a[c_len:], vc[:, p * dv2 + dv:(p + 1) * dv2], preferred_element_type=F32)
            st = st_ref[p]
            o_int = lax.dot_general(q_in[:, ls], st.astype(BF16), _NT, preferred_element_type=F32)
            o_p = jnp.concatenate([o0, o1], axis=1) + o_int
            upd = lax.dot_general(vc[:, p * dv2:(p + 1) * dv2], k_p[:, ls], _TN, preferred_element_type=F32)
            st_ref[p] = g[:, ls] * st + jnp.where(block_diag, upd, 0.0)
            if not fuse_out:
                o_ref[rows, p * dv2:(p + 1) * dv2] = o_p
            else:
                tot = of_ref[rows, p * dv2:(p + 1) * dv2] + o_p
                for hh in range(2):
                    hs = slice(p * dv2 + hh * dv, p * dv2 + (hh + 1) * dv)
                    oh = tot[:, hh * dv:(hh + 1) * dv]
                    on = oh * lax.rsqrt(jnp.mean(oh * oh, axis=-1, keepdims=True) + LN_EPS) * gain_ref[...]
                    o_ref[rows, hs] = (on * _silu(g_ref[rows, hs])).astype(o_ref.dtype)

    @pl.when(i == nt - 1)
    def _():
        sfin_ref[...] = st_ref[...]


def _gla_call(p, w2pad, gbias, s0, fuse, *, cols, bsz, seq, reverse):
    nq = GLA_HEADS * HEAD_DIM
    nv = s0.shape[1] * s0.shape[2]
    tile = _pick_tile(seq, GLA_TILE)
    nt = seq // tile
    tix = (lambda i: nt - 1 - i) if reverse else (lambda i: i)
    blk = lambda w, col: pl.BlockSpec((tile, w), lambda b, i: (b * nt + tix(i), col // w))
    const2 = lambda b, i: (0, 0)
    st_spec = pl.BlockSpec((None,) + s0.shape[1:], lambda b, i: (b, 0, 0, 0))
    in_specs = [blk(nq, cols['gl_q']), blk(nq, cols['gl_k']), blk(nv, cols['gl_v']), blk(V7X_LANES, cols['gl_z']),
                pl.BlockSpec(w2pad.shape, const2), pl.BlockSpec(gbias.shape, const2), st_spec]
    args = [p, p, p, p, w2pad, gbias, s0]
    if fuse is not None:
        o_other, gain = fuse
        in_specs += [blk(nv, 0), blk(nv, cols['gl_g']), pl.BlockSpec(gain.shape, const2)]
        args += [o_other, p, gain]
    out_dtype = BF16 if fuse is not None else F32
    return pl.pallas_call(
        functools.partial(_gla_kernel, reverse=reverse, nt=nt, fuse_out=fuse is not None),
        grid=(bsz, nt),
        in_specs=in_specs,
        out_specs=[blk(nv, 0), st_spec],
        out_shape=[jax.ShapeDtypeStruct((bsz * seq, nv), out_dtype), jax.ShapeDtypeStruct(s0.shape, F32)],
        scratch_shapes=[pltpu.VMEM(s0.shape[1:], F32)],
        compiler_params=_cparams(("parallel", "arbitrary")),
        name="gla_bwd" if reverse else "gla_fwd",
    )(*args)


def _gla_all(p32, pc32, w2, gb, gain, *, bsz, seq, n_ctx, d):
    cols = _p32_cols(d)
    nq = GLA_HEADS * HEAD_DIM
    dv = d // 4 // GLA_HEADS
    w2pad = [jnp.zeros((V7X_LANES, nq), F32).at[GLA_RANK * k:GLA_RANK * (k + 1)].set(w2[k]).astype(BF16) for k in range(2)]
    gbias = [gb[k].reshape(1, nq) for k in range(2)]
    s0 = jnp.zeros((bsz, GLA_HEADS // 2, 2 * dv, PAIR_W), F32)
    gain2 = gain.reshape(1, dv)
    oc_f, sc_f = _gla_call(pc32, w2pad[0], gbias[0], s0, None, cols=cols, bsz=bsz, seq=n_ctx, reverse=False)
    y_ctx, sc_b = _gla_call(pc32, w2pad[1], gbias[1], s0, (oc_f, gain2), cols=cols, bsz=bsz, seq=n_ctx, reverse=True)
    o_f, _ = _gla_call(p32, w2pad[0], gbias[0], sc_f, None, cols=cols, bsz=bsz, seq=seq, reverse=False)
    y_lat, _ = _gla_call(p32, w2pad[1], gbias[1], sc_b, (o_f, gain2), cols=cols, bsz=bsz, seq=seq, reverse=True)
    return y_lat, y_ctx


def _swiglu(x, wg, wu, wd):
    g = jnp.dot(x, wg, preferred_element_type=F32)
    u = jnp.dot(x, wu, preferred_element_type=F32)
    a = (_silu(g) * u).astype(BF16)
    return jnp.dot(a, wd, preferred_element_type=F32)


def _first_argmax(vals, idx, sentinel):
    m = jnp.max(vals, axis=0, keepdims=True)
    first = jnp.min(jnp.where(vals == m, idx, sentinel), axis=0, keepdims=True)
    return m, first


def _router_kernel(h_ref, wr_ref, rb_ref, su_ref, eidx_ref, wts_ref, rank_ref, cnt_ref, run_ref):
    i = pl.program_id(0)

    @pl.when(i == 0)
    def _():
        run_ref[...] = jnp.zeros_like(run_ref)

    t = h_ref.shape[0]
    gsz = N_EXPERTS // N_GROUPS
    logits = lax.dot_general(wr_ref[...], h_ref[...], _NT, preferred_element_type=F32)
    scores = jax.nn.sigmoid(logits)
    sel = scores + rb_ref[...]
    sub = lax.broadcasted_iota(jnp.int32, (gsz, t), 0)
    gscore = []
    for g in range(N_GROUPS):
        blk = sel[g * gsz:(g + 1) * gsz]
        m1, a1 = _first_argmax(blk, sub, gsz)
        m2 = jnp.max(jnp.where(sub == a1, -jnp.inf, blk), axis=0, keepdims=True)
        gscore.append(m1 + m2)
    gcur = jnp.concatenate(gscore, axis=0)
    gid = lax.broadcasted_iota(jnp.int32, (N_GROUPS, t), 0)
    gkeep = jnp.zeros((N_GROUPS, t), F32)
    for _ in range(TOPK_GROUPS):
        _, a = _first_argmax(gcur, gid, N_GROUPS)
        hit = gid == a
        gkeep = jnp.where(hit, 1.0, gkeep)
        gcur = jnp.where(hit, -jnp.inf, gcur)
    cur = jnp.concatenate(
        [jnp.where(gkeep[g:g + 1] > 0.0, sel[g * gsz:(g + 1) * gsz], -jnp.inf) for g in range(N_GROUPS)], axis=0)
    eid = lax.broadcasted_iota(jnp.int32, (N_EXPERTS, t), 0)
    chosen = jnp.zeros((N_EXPERTS, t), F32)
    hits, picks, wraw = [], [], []
    for _ in range(TOP_K):
        _, a = _first_argmax(cur, eid, N_EXPERTS)
        hit = eid == a
        hits.append(hit)
        picks.append(a)
        wraw.append(jnp.sum(jnp.where(hit, scores, 0.0), axis=0, keepdims=True))
        chosen = jnp.where(hit, 1.0, chosen)
        cur = jnp.where(hit, -jnp.inf, cur)
    wsum = wraw[0]
    for w in wraw[1:]:
        wsum = wsum + w
    eidx_ref[...] = jnp.concatenate(picks, axis=0)
    wts_ref[...] = jnp.concatenate([w / wsum * ROUTED_SCALE for w in wraw], axis=0)
    before = jnp.dot(chosen.astype(BF16), su_ref[...], preferred_element_type=F32) + run_ref[...][:, 0:1]
    rank_ref[...] = jnp.concatenate(
        [jnp.sum(jnp.where(hit, before, 0.0), axis=0, keepdims=True) for hit in hits], axis=0).astype(jnp.int32)
    run_ref[...] = run_ref[...] + jnp.sum(chosen, axis=1, keepdims=True)
    cnt_ref[...] = run_ref[...]


def _router_call(h, w_router, router_bias):
    n, d = h.shape
    t = _pick_tile(n, ROUTER_TILE)
    wr_t = w_router.T.astype(BF16)
    strict_upper = jnp.triu(jnp.ones((t, t), F32), 1).astype(BF16)
    const2 = lambda i: (0, 0)
    tok = lambda i: (0, i)
    eidx, wts, rank, cnt = pl.pallas_call(
        _router_kernel,
        grid=(n // t,),
        in_specs=[pl.BlockSpec((t, d), lambda i: (i, 0)), pl.BlockSpec((N_EXPERTS, d), const2),
                  pl.BlockSpec((N_EXPERTS, 1), const2), pl.BlockSpec((t, t), const2)],
        out_specs=[pl.BlockSpec((TOP_K, t), tok), pl.BlockSpec((TOP_K, t), tok), pl.BlockSpec((TOP_K, t), tok),
                   pl.BlockSpec((N_EXPERTS, V7X_LANES), const2)],
        out_shape=[jax.ShapeDtypeStruct((TOP_K, n), jnp.int32), jax.ShapeDtypeStruct((TOP_K, n), F32),
                   jax.ShapeDtypeStruct((TOP_K, n), jnp.int32), jax.ShapeDtypeStruct((N_EXPERTS, V7X_LANES), F32)],
        scratch_shapes=[pltpu.VMEM((N_EXPERTS, V7X_LANES), F32)],
        compiler_params=_cparams(("arbitrary",)),
        name="router",
    )(h, wr_t, router_bias.reshape(N_EXPERTS, 1).astype(F32), strict_upper)
    return eidx, wts, rank, cnt[:, 0].astype(jnp.int32)


def _dispatch_kernel(zoff_ref, dest_ref, hp_hbm, xs_hbm, zbuf, sem, zsem):
    i = pl.program_id(0)
    t = dest_ref.shape[1]

    @pl.when(i == 0)
    def _():
        zbuf[...] = jnp.zeros_like(zbuf)
        for e in range(N_EXPERTS):
            off = pl.multiple_of(zoff_ref[e], MOE_TB)
            pltpu.make_async_copy(zbuf, xs_hbm.at[pl.ds(off, MOE_TB)], zsem).start()
        for e in range(N_EXPERTS):
            pltpu.make_async_copy(zbuf, xs_hbm.at[pl.ds(0, MOE_TB)], zsem).wait()

    def row_copy(tok, slot):
        return pltpu.make_async_copy(hp_hbm.at[pl.ds(tok, 1)], xs_hbm.at[pl.ds(slot, 1)], sem)

    def issue(j, carry):
        for k in range(TOP_K):
            row_copy(i * t + j, dest_ref[k, j]).start()
        return carry

    def drain(j, carry):
        for k in range(TOP_K):
            row_copy(0, 0).wait()
        return carry

    lax.fori_loop(0, t, issue, 0)
    lax.fori_loop(0, t, drain, 0)


def _dispatch_call(zero_off, dest, h_packed, n_rows):
    n, c = h_packed.shape
    t = _pick_tile(n, DISPATCH_TILE)
    grid_spec = pltpu.PrefetchScalarGridSpec(
        num_scalar_prefetch=1,
        grid=(n // t,),
        in_specs=[pl.BlockSpec((TOP_K, t), lambda i, z: (0, i), memory_space=pltpu.SMEM),
                  pl.BlockSpec(memory_space=pl.ANY)],
        out_specs=pl.BlockSpec(memory_space=pl.ANY),
        scratch_shapes=[pltpu.VMEM((MOE_TB, c), jnp.uint32), pltpu.SemaphoreType.DMA, pltpu.SemaphoreType.DMA],
    )
    return pl.pallas_call(
        _dispatch_kernel,
        grid_spec=grid_spec,
        out_shape=jax.ShapeDtypeStruct((n_rows, c), jnp.uint32),
        compiler_params=_cparams(("arbitrary",)),
        name="moe_dispatch",
    )(zero_off, dest, h_packed)


def _expert_kernel(be_ref, nused_ref, x_ref, wg_ref, wu_ref, wd_ref, y_ref, wg_s, wu_s, wd_s):
    b = pl.program_id(0)
    prev = be_ref[jnp.maximum(b - 1, 0)]

    @pl.when((b == 0) | (be_ref[b] != prev))
    def _():
        wg_s[...] = wg_ref[...].astype(BF16)
        wu_s[...] = wu_ref[...].astype(BF16)
        wd_s[...] = wd_ref[...].astype(BF16)

    @pl.when(b < nused_ref[0])
    def _():
        y_ref[...] = _swiglu(_unpack_halves(x_ref[...]), wg_s[...], wu_s[...], wd_s[...])

    @pl.when(b >= nused_ref[0])
    def _():
        y_ref[...] = jnp.zeros_like(y_ref)


def _expert_call(block_e, n_used, x_sorted, wg, wu, wd):
    n_rows, c = x_sorted.shape
    _, d, de = wg.shape
    n_blocks = n_rows // MOE_TB
    wsel = lambda b, be, nu: (be[b], 0, 0)
    xsel = lambda b, be, nu: (jnp.minimum(b, nu[0] - 1), 0)
    grid_spec = pltpu.PrefetchScalarGridSpec(
        num_scalar_prefetch=2,
        grid=(n_blocks,),
        in_specs=[pl.BlockSpec((MOE_TB, c), xsel), pl.BlockSpec((None, d, de), wsel),
                  pl.BlockSpec((None, d, de), wsel), pl.BlockSpec((None, de, d), wsel)],
        out_specs=pl.BlockSpec((MOE_TB, d), lambda b, be, nu: (b, 0)),
        scratch_shapes=[pltpu.VMEM((d, de), BF16), pltpu.VMEM((d, de), BF16), pltpu.VMEM((de, d), BF16)],
    )
    return pl.pallas_call(
        _expert_kernel,
        grid_spec=grid_spec,
        out_shape=jax.ShapeDtypeStruct((n_rows, d), F32),
        compiler_params=_cparams(("arbitrary",)),
        name="experts",
    )(block_e, n_used, x_sorted, wg, wu, wd)


def _combine_kernel(dest_ref, w_ref, h_ref, x_ref, g2_ref, wsg_ref, wsu_ref, wsd_ref, lng_ref, lnb_ref, y_hbm,
                    o_ref, ybuf, sem, *, alpha):
    t = h_ref.shape[0]

    def row_copy(slot, k, j):
        return pltpu.make_async_copy(y_hbm.at[pl.ds(slot, 1)], ybuf.at[k, pl.ds(j, 1)], sem)

    def issue(j, carry):
        for k in range(TOP_K):
            row_copy(dest_ref[k, j], k, j).start()
        return carry

    def drain(j, carry):
        for k in range(TOP_K):
            row_copy(0, k, j).wait()
        return carry

    lax.fori_loop(0, t, issue, 0)
    acc = _swiglu(h_ref[...], wsg_ref[...], wsu_ref[...], wsd_ref[...])
    lax.fori_loop(0, t, drain, 0)
    w = w_ref[...]
    for k in range(TOP_K):
        acc = acc + w[:, k:k + 1] * ybuf[k]
    r = alpha * x_ref[...] + g2_ref[0] * acc
    o_ref[...] = _ln_rows(r) * lng_ref[...] + lnb_ref[...]


def _combine_call(dest, wts_t, h, x2d, gate2, wsg, wsu, wsd, ln_g, ln_b, y_sorted, *, row0, rows_per_group, alpha):
    m, d = x2d.shape
    de = wsg.shape[1]
    t = _pick_tile(min(rows_per_group, m), COMBINE_TILE)
    assert row0 % t == 0 and rows_per_group % t == 0
    t0 = row0 // t
    const2 = lambda i: (0, 0)
    once = pl.Buffered(1)
    return pl.pallas_call(
        functools.partial(_combine_kernel, alpha=alpha),
        grid=(m // t,),
        in_specs=[pl.BlockSpec((TOP_K, t), lambda i: (0, t0 + i), memory_space=pltpu.SMEM),
                  pl.BlockSpec((t, TOP_K), lambda i: (t0 + i, 0)),
                  pl.BlockSpec((t, d), lambda i: (t0 + i, 0)),
                  pl.BlockSpec((t, d), lambda i: (i, 0)),
                  pl.BlockSpec((1, 1, d), lambda i: ((i * t) // rows_per_group, 0, 0)),
                  pl.BlockSpec((d, de), const2, pipeline_mode=once),
                  pl.BlockSpec((d, de), const2, pipeline_mode=once),
                  pl.BlockSpec((de, d), const2, pipeline_mode=once),
                  pl.BlockSpec((1, d), const2), pl.BlockSpec((1, d), const2),
                  pl.BlockSpec(memory_space=pl.ANY)],
        out_specs=pl.BlockSpec((t, d), lambda i: (i, 0)),
        out_shape=jax.ShapeDtypeStruct((m, d), F32),
        scratch_shapes=[pltpu.VMEM((TOP_K, t, d), F32), pltpu.SemaphoreType.DMA],
        compiler_params=_cparams(("arbitrary",)),
        name="moe_combine",
    )(dest, wts_t, h, x2d, gate2, wsg, wsu, wsd, ln_g.reshape(1, d), ln_b.reshape(1, d), y_sorted)


def _moe_routed(h, h_packed, w_router, router_bias, w_eg, w_eu, w_ed):
    n = h.shape[0]
    eidx, wts, rank, counts = _router_call(h, w_router, router_bias)
    padded = (counts + MOE_TB - 1) // MOE_TB * MOE_TB
    pad_end = jnp.cumsum(padded)
    pad_start = pad_end - padded
    n_blocks = (n * TOP_K + N_EXPERTS * (MOE_TB - 1) + MOE_TB - 1) // MOE_TB
    block_e = jnp.minimum(jnp.searchsorted(pad_end, jnp.arange(n_blocks) * MOE_TB, side='right'),
                          N_EXPERTS - 1).astype(jnp.int32)
    n_used = (pad_end[-1:] // MOE_TB).astype(jnp.int32)
    zero_off = jnp.maximum(pad_end - MOE_TB, 0).astype(jnp.int32)
    dest = (pad_start[eidx] + rank).astype(jnp.int32)
    x_sorted = _dispatch_call(zero_off, dest, h_packed, n_blocks * MOE_TB)
    y_sorted = _expert_call(block_e, n_used, x_sorted, w_eg, w_eu, w_ed)
    return dest, wts.T, y_sorted


def _pack_w_in(w, d):
    gw = d // 4
    cuts = np.cumsum([gw, gw, gw, gw, gw, gw, gw // 2, gw // 2, gw, gw, 2 * GLA_RANK, gw, gw // 4, gw // 4])[:-1].tolist()
    (na_q, na_k, na_v, sc_u, sc_b, sc_c, gl_q, gl_k, gl_v, gl_g, gl_z, sw_q, sw_k, sw_v) = jnp.split(w, cuts, axis=1)
    zpad = jnp.zeros((d, V7X_LANES - 2 * GLA_RANK), w.dtype)
    cols = [na_q, na_k, na_v, sc_u, sc_b, sc_c, gl_v, gl_g, sw_q, gl_q, gl_k, sw_k, sw_v, gl_z, zpad]
    packed = jnp.concatenate(cols, axis=1)
    pad = (-packed.shape[1]) % PROJ_TN
    packed = jnp.pad(packed, ((0, 0), (0, pad)))
    return packed.astype(BF16)


def _p32_cols(d):
    gw = d // 4
    cols, c = {}, 0
    for name, wdt in (('sc_u', gw), ('sc_b', gw), ('sc_c', gw), ('gl_v', gw), ('gl_g', gw), ('sw_q', gw),
                      ('gl_q', gw // 2), ('gl_k', gw // 2), ('sw_k', gw // 4), ('sw_v', gw // 4), ('gl_z', V7X_LANES)):
        cols[name] = c
        c += wdt
    return cols


def _token_mixers(p16, p32, pc16, pc32, rpb, conv_w, gla_w2, gla_b, gla_norm_g, sink, with_ctx_out, *,
                  bsz, seq, n_ctx, d):
    cols = _p32_cols(d)
    kh = min(NA_KH, seq // GRID_W)
    y_na = _na_call(p16, pc16, _na_bias_table(rpb, kh), bsz=bsz, seq=seq, n_ctx=n_ctx)
    conv_cols = dict(col_u=cols['sc_u'], col_b=cols['sc_b'], col_c=cols['sc_c'])
    y_sc = _conv_call(p32, conv_w, rows_per_seq=seq, **conv_cols)
    y_gl, yc_gl = _gla_all(p32, pc32, gla_w2, gla_b, gla_norm_g, bsz=bsz, seq=seq, n_ctx=n_ctx, d=d)
    tables = _rope_tables(seq)
    swa_cols = dict(col_q=cols['sw_q'], col_k=cols['sw_k'], col_v=cols['sw_v'])
    qr, kd, vd = _swa_prep_call(p32, tables, rows_per_seq=seq, rope=True, **swa_cols)
    ctx_tables = tuple(t[:n_ctx] for t in tables)
    qx, kxd, vxd = _swa_prep_call(pc32, ctx_tables, rows_per_seq=n_ctx, rope=False, **swa_cols)
    y_sw = _swa_call(sink, qr, kd, vd, kxd, vxd, bsz=bsz, seq=seq, n_ctx=n_ctx)
    y_lat = (y_na, y_sc, y_gl, y_sw)
    if not with_ctx_out:
        return y_lat, None
    yc_na, yc_sw = _ctx_attn_call(sink, pc16, qx, kxd, vxd, bsz=bsz, n_ctx=n_ctx)
    yc_sc = _conv_call(pc32, conv_w, rows_per_seq=n_ctx, **conv_cols)
    return y_lat, (yc_na, yc_sc, yc_gl, yc_sw)


def kernel(x, c, ctx, c_ctx, w_ada, b_ada, w_in, na_rpb, conv_w, gla_w2, gla_b, gla_norm_g, swa_sink, w_out,
           ln1_g, ln1_b, w_router, router_bias, w_exp_gate, w_exp_up, w_exp_down, w_sh_gate, w_sh_up, w_sh_down,
           ln2_g, ln2_b):
    bsz, seq, d = x.shape
    n_ctx = ctx.shape[1]
    nc = bsz * n_ctx
    depth = w_in.shape[0]
    alpha = (2 * depth) ** 0.25
    n16 = 3 * (d // 4)
    x2 = x.reshape(bsz * seq, d)
    hc2 = ctx.reshape(nc, d)
    c_rows = jnp.zeros((8, d), F32).at[:bsz].set(c).at[bsz].set(c_ctx)
    for layer in range(depth):
        last = layer == depth - 1
        mod = _ada_call(c_rows, w_ada[layer], b_ada[layer])
        sh1, sc1, g1, sh2, sc2, g2 = [t[:bsz, None, :] for t in jnp.split(mod, 6, axis=-1)]
        sh1c, sc1c, g1c, sh2c, sc2c, g2c = [t[bsz:bsz + 1, None, :] for t in jnp.split(mod, 6, axis=-1)]
        w_p = _pack_w_in(w_in[layer], d)
        p16, p32 = _proj_call(x2, sh1, sc1, w_p, rows_per_group=seq, n16=n16)
        pc16, pc32 = _proj_call(hc2, sh1c, sc1c, w_p, rows_per_group=nc, n16=n16)
        y_lat, y_ctx = _token_mixers(p16, p32, pc16, pc32, na_rpb[layer], conv_w[layer], gla_w2[layer], gla_b[layer],
                                     gla_norm_g[layer], swa_sink[layer], not last, bsz=bsz, seq=seq, n_ctx=n_ctx, d=d)
        w_o = w_out[layer].astype(BF16)
        x2, h_lat, hp_lat = _outproj_call(y_lat, x2, g1, w_o, ln1_g[layer], ln1_b[layer], sh2, sc2,
                                          rows_per_group=seq, alpha=alpha)
        shared_w = (w_sh_gate[layer].astype(BF16), w_sh_up[layer].astype(BF16), w_sh_down[layer].astype(BF16))
        route_w = (w_router[layer], router_bias[layer], w_exp_gate[layer], w_exp_up[layer], w_exp_down[layer])
        if last:
            dest, wts_t, y_sorted = _moe_routed(h_lat, hp_lat, *route_w)
            x2 = _combine_call(dest, wts_t, h_lat, x2, g2, *shared_w, ln2_g[layer], ln2_b[layer], y_sorted,
                               row0=0, rows_per_group=seq, alpha=alpha)
        else:
            hc2, h_ctx, hp_ctx = _outproj_call(y_ctx, hc2, g1c, w_o, ln1_g[layer], ln1_b[layer], sh2c, sc2c,
                                               rows_per_group=nc, alpha=alpha)
            h_all = jnp.concatenate([h_ctx, h_lat], axis=0)
            hp_all = jnp.concatenate([hp_ctx, hp_lat], axis=0)
            dest, wts_t, y_sorted = _moe_routed(h_all, hp_all, *route_w)
            hc2 = _combine_call(dest, wts_t, h_all, hc2, g2c, *shared_w, ln2_g[layer], ln2_b[layer], y_sorted,
                                row0=0, rows_per_group=nc, alpha=alpha)
            x2 = _combine_call(dest, wts_t, h_all, x2, g2, *shared_w, ln2_g[layer], ln2_b[layer], y_sorted,
                               row0=nc, rows_per_group=seq, alpha=alpha)
    return x2.reshape(bsz, seq, d)
```

```python
import functools

import jax
import jax.numpy as jnp
import numpy as np
from jax import lax
from jax.experimental import pallas as pl
from jax.experimental.pallas import tpu as pltpu

GRID_W = 64
HEAD_DIM = 64
NA_KH = 8
NA_KW = 16
SC_KSIZE = 3
GLA_HEADS = 4
GLA_RANK = 16
GLA_TAU = 16.0
SWA_KV_HEADS = 2
SWA_BLOCK = 128
ROPE_BASE = 10000.0
N_EXPERTS = 64
TOP_K = 8
N_GROUPS = 8
TOPK_GROUPS = 4
ROUTED_SCALE = 2.5
LN_EPS = 1e-6

V7X_LANES = 128
V7X_VMEM_LIMIT_BYTES = 48 * 1024 * 1024

PROJ_TM = 1024
PROJ_TN = 512
OUT_TM = 256
MOE_TB = 256
ROUTER_TILE = 512
DISPATCH_TILE = 256
COMBINE_TILE = 128
CONV_HALO = 8
GLA_C = 64
GLA_TILE = 512
ADA_TN = 2048

F32 = jnp.float32
BF16 = jnp.bfloat16
PAIR_W = 2 * HEAD_DIM
ATTN_SCALE = HEAD_DIM ** -0.5
_NT = (((1,), (1,)), ((), ()))
_TN = (((0,), (0,)), ((), ()))


def _cparams(sem):
    return pltpu.CompilerParams(dimension_semantics=sem, vmem_limit_bytes=V7X_VMEM_LIMIT_BYTES)


def _pick_tile(m, preferred):
    t = preferred
    while t > 8 and m % t:
        t //= 2
    assert m % t == 0, (m, preferred)
    return t


def _ln_rows(x):
    mu = jnp.mean(x, axis=-1, keepdims=True)
    xc = x - mu
    var = jnp.mean(xc * xc, axis=-1, keepdims=True)
    return xc * lax.rsqrt(var + LN_EPS)


def _silu(x):
    return x * jax.nn.sigmoid(x)


def _ada_kernel(c_ref, w_ref, b_ref, o_ref):
    a = _silu(c_ref[...]).astype(BF16)
    o_ref[...] = jnp.dot(a, w_ref[...].astype(BF16), preferred_element_type=F32) + b_ref[...]


def _ada_call(c_rows, w_ada, b_ada, layer):
    r, d = c_rows.shape
    n = w_ada.shape[2]
    tn = _pick_tile(n, ADA_TN)
    return pl.pallas_call(
        _ada_kernel,
        grid=(n // tn,),
        in_specs=[pl.BlockSpec((r, d), lambda j: (0, 0)), pl.BlockSpec((None, d, tn), lambda j: (layer, 0, j)),
                  pl.BlockSpec((1, tn), lambda j: (0, j))],
        out_specs=pl.BlockSpec((r, tn), lambda j: (0, j)),
        out_shape=jax.ShapeDtypeStruct((r, n), F32),
        compiler_params=_cparams(("parallel",)),
        name="ada_mod",
    )(c_rows, w_ada, b_ada.reshape(1, n))


def _proj_kernel(x_ref, sh_ref, sc_ref, w_ref, o16_ref, o32_ref, xn_ref, *, nb16):
    j = pl.program_id(1)

    @pl.when(j == 0)
    def _():
        y = _ln_rows(x_ref[...]) * (1.0 + sc_ref[0]) + sh_ref[0]
        xn_ref[...] = y.astype(BF16)

    acc = jnp.dot(xn_ref[...], w_ref[...], preferred_element_type=F32)

    @pl.when(j < nb16)
    def _():
        o16_ref[...] = acc.astype(BF16)

    @pl.when(j >= nb16)
    def _():
        o32_ref[...] = acc


def _proj_call(x2d, shift, scale, w_packed, *, rows_per_group, n16):
    m, d = x2d.shape
    ntot = w_packed.shape[1]
    tm = min(PROJ_TM, rows_per_group)
    assert m % tm == 0 and rows_per_group % tm == 0 and ntot % PROJ_TN == 0 and n16 % PROJ_TN == 0
    nb16 = n16 // PROJ_TN
    nb = ntot // PROJ_TN
    grp = lambda i, j: ((i * tm) // rows_per_group, 0, 0)
    return pl.pallas_call(
        functools.partial(_proj_kernel, nb16=nb16),
        grid=(m // tm, nb),
        in_specs=[
            pl.BlockSpec((tm, d), lambda i, j: (i, 0)),
            pl.BlockSpec((1, 1, d), grp),
            pl.BlockSpec((1, 1, d), grp),
            pl.BlockSpec((d, PROJ_TN), lambda i, j: (0, j)),
        ],
        out_specs=[
            pl.BlockSpec((tm, PROJ_TN), lambda i, j: (i, jnp.minimum(j, nb16 - 1))),
            pl.BlockSpec((tm, PROJ_TN), lambda i, j: (i, jnp.maximum(j - nb16, 0))),
        ],
        out_shape=[jax.ShapeDtypeStruct((m, n16), BF16), jax.ShapeDtypeStruct((m, ntot - n16), F32)],
        scratch_shapes=[pltpu.VMEM((tm, d), BF16)],
        compiler_params=_cparams(("parallel", "arbitrary")),
        name="proj",
    )(x2d, shift, scale, w_packed)


def _pack_halves(h):
    c = h.shape[1] // 2
    lo = pltpu.bitcast(h[:, :c].astype(BF16).astype(F32), jnp.uint32)
    hi = pltpu.bitcast(h[:, c:].astype(BF16).astype(F32), jnp.uint32)
    return lax.shift_right_logical(lo, jnp.uint32(16)) | (hi & jnp.uint32(0xFFFF0000))


def _unpack_halves(w):
    lo = pltpu.bitcast(lax.shift_left(w, jnp.uint32(16)), F32)
    hi = pltpu.bitcast(w & jnp.uint32(0xFFFF0000), F32)
    return jnp.concatenate([lo, hi], axis=1).astype(BF16)


def _outproj_kernel(y0_ref, y1_ref, y2_ref, y3_ref, x_ref, g1_ref, w_ref, lng_ref, lnb_ref,
                    sh2_ref, sc2_ref, xo_ref, h_ref, hp_ref, *, alpha):
    gw = y0_ref.shape[1]
    acc = jnp.dot(y0_ref[...], w_ref[0:gw, :], preferred_element_type=F32)
    acc += jnp.dot(y1_ref[...], w_ref[gw:2 * gw, :], preferred_element_type=F32)
    acc += jnp.dot(y2_ref[...], w_ref[2 * gw:3 * gw, :], preferred_element_type=F32)
    acc += jnp.dot(y3_ref[...], w_ref[3 * gw:4 * gw, :], preferred_element_type=F32)
    r = alpha * x_ref[...] + g1_ref[0] * acc
    xn = _ln_rows(r) * lng_ref[...] + lnb_ref[...]
    xo_ref[...] = xn
    h = _ln_rows(xn) * (1.0 + sc2_ref[0]) + sh2_ref[0]
    h_ref[...] = h.astype(BF16)
    hp_ref[...] = _pack_halves(h)


def _outproj_call(ys, x2d, gate1, w_out_bf16, ln_g, ln_b, shift2, scale2, *, rows_per_group, alpha):
    m, d = x2d.shape
    gw = d // 4
    tm = min(OUT_TM, rows_per_group)
    assert m % tm == 0 and rows_per_group % tm == 0
    grp = lambda i: ((i * tm) // rows_per_group, 0, 0)
    row = lambda i: (i, 0)
    const2 = lambda i: (0, 0)
    return pl.pallas_call(
        functools.partial(_outproj_kernel, alpha=alpha),
        grid=(m // tm,),
        in_specs=[pl.BlockSpec((tm, gw), row)] * 4 + [
            pl.BlockSpec((tm, d), row),
            pl.BlockSpec((1, 1, d), grp),
            pl.BlockSpec((d, d), const2, pipeline_mode=pl.Buffered(1)),
            pl.BlockSpec((1, d), const2),
            pl.BlockSpec((1, d), const2),
            pl.BlockSpec((1, 1, d), grp),
            pl.BlockSpec((1, 1, d), grp),
        ],
        out_specs=[pl.BlockSpec((tm, d), row), pl.BlockSpec((tm, d), row), pl.BlockSpec((tm, d // 2), row)],
        out_shape=[jax.ShapeDtypeStruct((m, d), F32), jax.ShapeDtypeStruct((m, d), BF16),
                   jax.ShapeDtypeStruct((m, d // 2), jnp.uint32)],
        compiler_params=_cparams(("parallel",)),
        name="outproj",
    )(*ys, x2d, gate1, w_out_bf16, ln_g.reshape(1, d), ln_b.reshape(1, d), shift2, scale2)


def _stack_pair(q2):
    lane = lax.broadcasted_iota(jnp.int32, q2.shape, 1)
    lo = jnp.where(lane < HEAD_DIM, q2, 0.0)
    hi = jnp.where(lane >= HEAD_DIM, q2, 0.0)
    return jnp.concatenate([lo, hi], axis=0).astype(BF16)


def _unstack_pair(o):
    n = o.shape[0] // 2
    lane = lax.broadcasted_iota(jnp.int32, (n, o.shape[1]), 1)
    return jnp.where(lane < HEAD_DIM, o[:n], o[n:])


def _pair_softmax_av(qs, ks, vs, biases, masks, sink_col):
    ss = []
    for k, bia, msk in zip(ks, biases, masks):
        s = lax.dot_general(qs, k, _NT, preferred_element_type=F32)
        if bia is not None:
            s = s + bia
        if msk is not None:
            s = jnp.where(msk, s, -jnp.inf)
        ss.append(s)
    m = jnp.max(ss[0], axis=-1, keepdims=True)
    for s in ss[1:]:
        m = jnp.maximum(m, jnp.max(s, axis=-1, keepdims=True))
    if sink_col is not None:
        m = jnp.maximum(m, sink_col)
        l = jnp.exp(sink_col - m)
    else:
        l = jnp.zeros_like(m)
    o = None
    for s, v in zip(ss, vs):
        e = jnp.exp(s - m)
        l = l + jnp.sum(e, axis=-1, keepdims=True)
        pv = jnp.dot(e.astype(BF16), v, preferred_element_type=F32)
        o = pv if o is None else o + pv
    return o / l


def _na_kernel(q_ref, k_ref, v_ref, kc_ref, vc_ref, bias_ref, o_ref, *, rows, kh):
    r = pl.program_id(1)
    rs = jnp.clip(r - kh // 2, 0, rows - kh)
    start = pl.multiple_of(rs * GRID_W, GRID_W)
    nwin = kh * GRID_W
    for p in range(q_ref.shape[1] // PAIR_W):
        sl = slice(p * PAIR_W, (p + 1) * PAIR_W)
        qs = _stack_pair(q_ref[:, sl].astype(F32) * ATTN_SCALE)
        kw = k_ref[pl.ds(start, nwin), sl]
        vw = v_ref[pl.ds(start, nwin), sl]
        o = _pair_softmax_av(qs, [kw, kc_ref[:, sl]], [vw, vc_ref[:, sl]], [bias_ref[p], None], [None, None], None)
        o_ref[:, sl] = _unstack_pair(o).astype(o_ref.dtype)


def _na_bias_table(rpb, kh):
    nh = rpb.shape[0]
    c = jnp.arange(GRID_W)
    cstart = jnp.clip(c - NA_KW // 2, 0, GRID_W - NA_KW)
    valid = (c[None, :] >= cstart[:, None]) & (c[None, :] < cstart[:, None] + NA_KW)
    coff = jnp.clip(c[None, :] - c[:, None], 1 - NA_KW, NA_KW - 1) + NA_KW - 1
    roff = jnp.arange(kh)[None, :] - jnp.arange(kh)[:, None] + NA_KH - 1
    pick_r = (roff[:, :, None] == jnp.arange(rpb.shape[1])).astype(F32)
    pick_c = (coff[:, :, None] == jnp.arange(rpb.shape[2])).astype(F32)
    bias = jnp.einsum('hab,dia,ckb->hdick', rpb.astype(F32), pick_r, pick_c, precision=lax.Precision.HIGHEST)
    bias = jnp.where(valid[None, None, None], bias, -jnp.inf)
    bias = bias.transpose(1, 0, 3, 2, 4)
    return bias.reshape(kh, nh // 2, 2 * GRID_W, kh * GRID_W)


def _na_call(p16, pc16, bias_tab, *, bsz, seq, n_ctx):
    gw = p16.shape[1] // 3
    rows = seq // GRID_W
    kh = bias_tab.shape[0]
    delta = lambda b, r: (r - jnp.clip(r - kh // 2, 0, rows - kh), 0, 0, 0)
    once = pl.Buffered(1)
    return pl.pallas_call(
        functools.partial(_na_kernel, rows=rows, kh=kh),
        grid=(bsz, rows),
        in_specs=[
            pl.BlockSpec((GRID_W, gw), lambda b, r: (b * rows + r, 0)),
            pl.BlockSpec((seq, gw), lambda b, r: (b, 1), pipeline_mode=once),
            pl.BlockSpec((seq, gw), lambda b, r: (b, 2), pipeline_mode=once),
            pl.BlockSpec((n_ctx, gw), lambda b, r: (b, 1)),
            pl.BlockSpec((n_ctx, gw), lambda b, r: (b, 2)),
            pl.BlockSpec((None,) + bias_tab.shape[1:], delta),
        ],
        out_specs=pl.BlockSpec((GRID_W, gw), lambda b, r: (b * rows + r, 0)),
        out_shape=jax.ShapeDtypeStruct((bsz * seq, gw), BF16),
        compiler_params=_cparams(("parallel", "arbitrary")),
        name="na_attn",
    )(p16, p16, p16, pc16, pc16, bias_tab)


def _dup_heads(t):
    lane = lax.broadcasted_iota(jnp.int32, t.shape, 1)
    sw = pltpu.roll(t, HEAD_DIM, 1)
    return jnp.concatenate([jnp.where(lane < HEAD_DIM, t, sw), jnp.where(lane < HEAD_DIM, sw, t)], axis=1)


def _rope(t, cos, sa, sb):
    q = HEAD_DIM // 4
    return t * cos + pltpu.roll(t, q, 1) * sa + pltpu.roll(t, V7X_LANES - q, 1) * sb


def _swa_prep_kernel(q_ref, k_ref, v_ref, cos_ref, sa_ref, sb_ref, qo_ref, ko_ref, vo_ref, *, rope):
    k = k_ref[...]
    if rope:
        cos, sa, sb = cos_ref[...], sa_ref[...], sb_ref[...]
        k = _rope(k, cos, sa, sb)
    for p in range(q_ref.shape[1] // PAIR_W):
        sl = slice(p * PAIR_W, (p + 1) * PAIR_W)
        q = q_ref[:, sl]
        if rope:
            q = _rope(q, cos, sa, sb)
        qo_ref[:, sl] = (q * ATTN_SCALE).astype(BF16)
    ko_ref[...] = _dup_heads(k).astype(BF16)
    vo_ref[...] = _dup_heads(v_ref[...]).astype(BF16)


def _rope_tables(seq):
    t = jnp.arange(seq)
    row = (t // GRID_W).astype(F32)
    col = (t % GRID_W).astype(F32)
    quarter = HEAD_DIM // 4
    inv = ROPE_BASE ** (-2.0 * jnp.arange(quarter, dtype=F32) / (HEAD_DIM // 2))
    ang_r = row[:, None] * inv[None, :]
    ang_c = col[:, None] * inv[None, :]
    zero = jnp.zeros_like(ang_r)
    cos_h = jnp.concatenate([jnp.cos(ang_r)] * 2 + [jnp.cos(ang_c)] * 2, axis=1)
    sa_h = jnp.concatenate([zero, jnp.sin(ang_r), zero, jnp.sin(ang_c)], axis=1)
    sb_h = jnp.concatenate([-jnp.sin(ang_r), zero, -jnp.sin(ang_c), zero], axis=1)
    two = lambda a: jnp.concatenate([a, a], axis=1)
    return two(cos_h), two(sa_h), two(sb_h)


def _swa_prep_call(p32, tables, *, col_q, col_k, col_v, rows_per_seq, rope):
    m = p32.shape[0]
    gw = 4 * PAIR_W
    tm = _pick_tile(rows_per_seq, 512)
    nseq_tiles = rows_per_seq // tm
    tab = lambda i: (i % nseq_tiles, 0)
    return pl.pallas_call(
        functools.partial(_swa_prep_kernel, rope=rope),
        grid=(m // tm,),
        in_specs=[
            pl.BlockSpec((tm, gw), lambda i: (i, col_q // gw)),
            pl.BlockSpec((tm, PAIR_W), lambda i: (i, col_k // PAIR_W)),
            pl.BlockSpec((tm, PAIR_W), lambda i: (i, col_v // PAIR_W)),
            pl.BlockSpec((tm, PAIR_W), tab),
            pl.BlockSpec((tm, PAIR_W), tab),
            pl.BlockSpec((tm, PAIR_W), tab),
        ],
        out_specs=[pl.BlockSpec((tm, gw), lambda i: (i, 0)), pl.BlockSpec((tm, 2 * PAIR_W), lambda i: (i, 0)),
                   pl.BlockSpec((tm, 2 * PAIR_W), lambda i: (i, 0))],
        out_shape=[jax.ShapeDtypeStruct((m, gw), BF16), jax.ShapeDtypeStruct((m, 2 * PAIR_W), BF16),
                   jax.ShapeDtypeStruct((m, 2 * PAIR_W), BF16)],
        compiler_params=_cparams(("parallel",)),
        name="swa_prep",
    )(p32, p32, p32, *tables)


def _sink_col(sink_ref, p, n):
    row = lax.broadcasted_iota(jnp.int32, (2 * n, 1), 0)
    return jnp.where(row < n, sink_ref[2 * p], sink_ref[2 * p + 1])


def _swa_kernel(sink_ref, q_ref, kp_ref, kc_ref, kn_ref, vp_ref, vc_ref, vn_ref, kx_ref, vx_ref, o_ref, *, nblk):
    n = pl.program_id(1)
    blk = q_ref.shape[0]
    qi = lax.broadcasted_iota(jnp.int32, (2 * blk, blk), 0) % blk
    kj = lax.broadcasted_iota(jnp.int32, (2 * blk, blk), 1)
    m_prev = kj >= qi + jnp.where(n > 0, 0, blk)
    m_next = kj <= qi - jnp.where(n < nblk - 1, 0, blk)
    npairs = q_ref.shape[1] // PAIR_W
    for p in range(npairs):
        sl = slice(p * PAIR_W, (p + 1) * PAIR_W)
        g = p // (npairs // SWA_KV_HEADS)
        gs = slice(g * PAIR_W, (g + 1) * PAIR_W)
        qs = _stack_pair(q_ref[:, sl].astype(F32))
        o = _pair_softmax_av(
            qs, [kp_ref[:, gs], kc_ref[:, gs], kn_ref[:, gs], kx_ref[:, gs]],
            [vp_ref[:, gs], vc_ref[:, gs], vn_ref[:, gs], vx_ref[:, gs]],
            [None] * 4, [m_prev, None, m_next, None], _sink_col(sink_ref, p, blk))
        o_ref[:, sl] = _unstack_pair(o).astype(o_ref.dtype)


def _swa_call(sink, qr, kd, vd, kxd, vxd, *, bsz, seq, n_ctx):
    gw = qr.shape[1]
    kw = kd.shape[1]
    nblk = seq // SWA_BLOCK
    cur = lambda b, n: (b * nblk + n, 0)
    prev = lambda b, n: (b * nblk + jnp.maximum(n - 1, 0), 0)
    nxt = lambda b, n: (b * nblk + jnp.minimum(n + 1, nblk - 1), 0)
    cx = lambda b, n: (b, 0)
    return pl.pallas_call(
        functools.partial(_swa_kernel, nblk=nblk),
        grid=(bsz, nblk),
        in_specs=[pl.BlockSpec(memory_space=pltpu.SMEM),
                  pl.BlockSpec((SWA_BLOCK, gw), cur),
                  pl.BlockSpec((SWA_BLOCK, kw), prev), pl.BlockSpec((SWA_BLOCK, kw), cur), pl.BlockSpec((SWA_BLOCK, kw), nxt),
                  pl.BlockSpec((SWA_BLOCK, kw), prev), pl.BlockSpec((SWA_BLOCK, kw), cur), pl.BlockSpec((SWA_BLOCK, kw), nxt),
                  pl.BlockSpec((n_ctx, kw), cx), pl.BlockSpec((n_ctx, kw), cx)],
        out_specs=pl.BlockSpec((SWA_BLOCK, gw), cur),
        out_shape=jax.ShapeDtypeStruct((bsz * seq, gw), BF16),
        compiler_params=_cparams(("parallel", "arbitrary")),
        name="swa_attn",
    )(sink, qr, kd, kd, kd, vd, vd, vd, kxd, vxd)


def _ctx_attn_kernel(sink_ref, qa_ref, ka_ref, va_ref, qd_ref, kd_ref, vd_ref, oa_ref, od_ref):
    n = qa_ref.shape[0]
    npairs = qa_ref.shape[1] // PAIR_W
    for p in range(npairs):
        sl = slice(p * PAIR_W, (p + 1) * PAIR_W)
        qs = _stack_pair(qa_ref[:, sl].astype(F32) * ATTN_SCALE)
        o = _pair_softmax_av(qs, [ka_ref[:, sl]], [va_ref[:, sl]], [None], [None], None)
        oa_ref[:, sl] = _unstack_pair(o).astype(oa_ref.dtype)
    for p in range(npairs):
        sl = slice(p * PAIR_W, (p + 1) * PAIR_W)
        g = p // (npairs // SWA_KV_HEADS)
        gs = slice(g * PAIR_W, (g + 1) * PAIR_W)
        qs = _stack_pair(qd_ref[:, sl].astype(F32))
        o = _pair_softmax_av(qs, [kd_ref[:, gs]], [vd_ref[:, gs]], [None], [None], _sink_col(sink_ref, p, n))
        od_ref[:, sl] = _unstack_pair(o).astype(od_ref.dtype)


def _ctx_attn_call(sink, pc16, qx, kxd, vxd, *, bsz, n_ctx):
    gw = qx.shape[1]
    kw = kxd.shape[1]
    return pl.pallas_call(
        _ctx_attn_kernel,
        grid=(bsz,),
        in_specs=[pl.BlockSpec(memory_space=pltpu.SMEM),
                  pl.BlockSpec((n_ctx, gw), lambda b: (b, 0)), pl.BlockSpec((n_ctx, gw), lambda b: (b, 1)),
                  pl.BlockSpec((n_ctx, gw), lambda b: (b, 2)), pl.BlockSpec((n_ctx, gw), lambda b: (b, 0)),
                  pl.BlockSpec((n_ctx, kw), lambda b: (b, 0)), pl.BlockSpec((n_ctx, kw), lambda b: (b, 0))],
        out_specs=[pl.BlockSpec((n_ctx, gw), lambda b: (b, 0)), pl.BlockSpec((n_ctx, gw), lambda b: (b, 0))],
        out_shape=[jax.ShapeDtypeStruct((bsz * n_ctx, gw), BF16)] * 2,
        compiler_params=_cparams(("parallel",)),
        name="ctx_attn",
    )(sink, pc16, pc16, pc16, qx, kxd, vxd)


def _conv_kernel(u_ref, b_ref, c_ref, up_ref, cp_ref, un_ref, cn_ref, w_ref, o_ref, *, tiles_per_seq):
    i = pl.program_id(0)
    tm = u_ref.shape[0]
    pos = i % tiles_per_seq
    keep_prev = jnp.where(pos == 0, 0.0, 1.0)
    keep_next = jnp.where(pos == tiles_per_seq - 1, 0.0, 1.0)
    z = c_ref[...] * u_ref[...]
    z_prev = (cp_ref[...] * up_ref[...])[CONV_HALO - 1:CONV_HALO, :] * keep_prev
    z_next = (cn_ref[...] * un_ref[...])[0:1, :] * keep_next
    row = lax.broadcasted_iota(jnp.int32, z.shape, 0)
    zm1 = jnp.where(row == 0, z_prev, pltpu.roll(z, 1, 0))
    zp1 = jnp.where(row == tm - 1, z_next, pltpu.roll(z, tm - 1, 0))
    w = w_ref[...]
    o_ref[...] = (b_ref[...] * (w[0:1] * zm1 + w[1:2] * z + w[2:3] * zp1)).astype(o_ref.dtype)


def _conv_call(p32, conv_w, *, col_u, col_b, col_c, rows_per_seq):
    m = p32.shape[0]
    gw = conv_w.shape[1]
    tm = _pick_tile(rows_per_seq, 512)
    hb = tm // CONV_HALO
    n_halo = m // CONV_HALO
    cur = lambda col: (lambda i: (i, col // gw))
    prv = lambda col: (lambda i: (jnp.maximum(i * hb - 1, 0), col // gw))
    nxt = lambda col: (lambda i: (jnp.minimum((i + 1) * hb, n_halo - 1), col // gw))
    return pl.pallas_call(
        functools.partial(_conv_kernel, tiles_per_seq=rows_per_seq // tm),
        grid=(m // tm,),
        in_specs=[pl.BlockSpec((tm, gw), cur(col_u)), pl.BlockSpec((tm, gw), cur(col_b)), pl.BlockSpec((tm, gw), cur(col_c)),
                  pl.BlockSpec((CONV_HALO, gw), prv(col_u)), pl.BlockSpec((CONV_HALO, gw), prv(col_c)),
                  pl.BlockSpec((CONV_HALO, gw), nxt(col_u)), pl.BlockSpec((CONV_HALO, gw), nxt(col_c)),
                  pl.BlockSpec((SC_KSIZE, gw), lambda i: (0, 0))],
        out_specs=pl.BlockSpec((tm, gw), lambda i: (i, 0)),
        out_shape=jax.ShapeDtypeStruct((m, gw), BF16),
        compiler_params=_cparams(("parallel",)),
        name="short_conv",
    )(p32, p32, p32, p32, p32, p32, p32, conv_w)


def _gla_kernel(*refs, reverse, nt, fuse_out):
    if fuse_out:
        (q_ref, k_ref, v_ref, z_ref, w2_ref, gb_ref, s0_ref, of_ref, g_ref, gain_ref, o_ref, sfin_ref, st_ref) = refs
    else:
        (q_ref, k_ref, v_ref, z_ref, w2_ref, gb_ref, s0_ref, o_ref, sfin_ref, st_ref) = refs
    i = pl.program_id(1)

    @pl.when(i == 0)
    def _():
        st_ref[...] = s0_ref[...]

    tile = q_ref.shape[0]
    c_len = min(GLA_C, tile)
    dk2 = PAIR_W
    dv2 = v_ref.shape[1] // (q_ref.shape[1] // dk2)
    dv = dv2 // 2
    u = jnp.dot(z_ref[...].astype(BF16), w2_ref[...], preferred_element_type=F32) + gb_ref[...]
    la = (jnp.minimum(u, 0.0) - jnp.log1p(jnp.exp(-jnp.abs(u)))) * (1.0 / GLA_TAU)

    r_i = lax.broadcasted_iota(jnp.int32, (c_len, c_len), 0)
    c_i = lax.broadcasted_iota(jnp.int32, (c_len, c_len), 1)
    tri = (r_i <= c_i) if reverse else (r_i >= c_i)
    tri_bf = jnp.where(tri, 1.0, 0.0).astype(BF16)
    r2 = lax.broadcasted_iota(jnp.int32, (2 * c_len, c_len), 0) % c_len
    c2 = lax.broadcasted_iota(jnp.int32, (2 * c_len, c_len), 1)
    tri2 = (r2 <= c2) if reverse else (r2 >= c2)
    bd_r = lax.broadcasted_iota(jnp.int32, (dv2, dk2), 0) // dv
    bd_c = lax.broadcasted_iota(jnp.int32, (dv2, dk2), 1) // HEAD_DIM
    block_diag = bd_r == bd_c
    last_row = 0 if reverse else c_len - 1
    mid_row = c_len // 2

    n_chunks = tile // c_len
    order = range(n_chunks - 1, -1, -1) if reverse else range(n_chunks)
    for c in order:
        rows = slice(c * c_len, (c + 1) * c_len)
        la_c = la[rows]
        la_hi = la_c.astype(BF16)
        la_lo = (la_c - la_hi.astype(F32)).astype(BF16)
        cum = (jnp.dot(tri_bf, la_hi, preferred_element_type=F32)
               + jnp.dot(tri_bf, la_lo, preferred_element_type=F32))
        last = cum[last_row:last_row + 1]
        cmid = cum[mid_row:mid_row + 1]
        qc = q_ref[rows, :] * ATTN_SCALE
        kc = k_ref[rows, :]
        vc = v_ref[rows, :].astype(BF16)
        q_in = (qc * jnp.exp(cum)).astype(BF16)
        q_t = qc * jnp.exp(cum - cmid)
        k_t = (kc * jnp.exp(cmid - cum)).astype(BF16)
        k_p = (kc * jnp.exp(last - cum)).astype(BF16)
        g = jnp.exp(last)
        for p in range(q_ref.shape[1] // dk2):
            ls = slice(p * dk2, (p + 1) * dk2)
            a = lax.dot_general(_stack_pair(q_t[:, ls]), k_t[:, ls], _NT, preferred_element_type=F32)
            a = jnp.where(tri2, a, 0.0).astype(BF16)
            o0 = jnp.dot(a[:c_len], vc[:, p * dv2:p * dv2 + dv], preferred_element_type=F32)
            o1 = jnp.dot(a[c_len:], vc[:, p * dv2 + dv:(p + 1) * dv2], preferred_element_type=F32)
            st = st_ref[p]
            o_int = lax.dot_general(q_in[:, ls], st.astype(BF16), _NT, preferred_element_type=F32)
            o_p = jnp.concatenate([o0, o1], axis=1) + o_int
            upd = lax.dot_general(vc[:, p * dv2:(p + 1) * dv2], k_p[:, ls], _TN, preferred_element_type=F32)
            st_ref[p] = g[:, ls] * st + jnp.where(block_diag, upd, 0.0)
            if not fuse_out:
                o_ref[rows, p * dv2:(p + 1) * dv2] = o_p
            else:
                tot = of_ref[rows, p * dv2:(p + 1) * dv2] + o_p
                for hh in range(2):
                    hs = slice(p * dv2 + hh * dv, p * dv2 + (hh + 1) * dv)
                    oh = tot[:, hh * dv:(hh + 1) * dv]
                    on = oh * lax.rsqrt(jnp.mean(oh * oh, axis=-1, keepdims=True) + LN_EPS) * gain_ref[...]
                    o_ref[rows, hs] = (on * _silu(g_ref[rows, hs])).astype(o_ref.dtype)

    @pl.when(i == nt - 1)
    def _():
        sfin_ref[...] = st_ref[...]


def _gla_call(p, w2pad, gbias, s0, fuse, *, cols, bsz, seq, reverse):
    nq = GLA_HEADS * HEAD_DIM
    nv = s0.shape[1] * s0.shape[2]
    tile = _pick_tile(seq, GLA_TILE)
    nt = seq // tile
    tix = (lambda i: nt - 1 - i) if reverse else (lambda i: i)
    blk = lambda w, col: pl.BlockSpec((tile, w), lambda b, i: (b * nt + tix(i), col // w))
    const2 = lambda b, i: (0, 0)
    st_spec = pl.BlockSpec((None,) + s0.shape[1:], lambda b, i: (b, 0, 0, 0))
    in_specs = [blk(nq, cols['gl_q']), blk(nq, cols['gl_k']), blk(nv, cols['gl_v']), blk(V7X_LANES, cols['gl_z']),
                pl.BlockSpec(w2pad.shape, const2), pl.BlockSpec(gbias.shape, const2), st_spec]
    args = [p, p, p, p, w2pad, gbias, s0]
    if fuse is not None:
        o_other, gain = fuse
        in_specs += [blk(nv, 0), blk(nv, cols['gl_g']), pl.BlockSpec(gain.shape, const2)]
        args += [o_other, p, gain]
    out_dtype = BF16 if fuse is not None else F32
    return pl.pallas_call(
        functools.partial(_gla_kernel, reverse=reverse, nt=nt, fuse_out=fuse is not None),
        grid=(bsz, nt),
        in_specs=in_specs,
        out_specs=[blk(nv, 0), st_spec],
        out_shape=[jax.ShapeDtypeStruct((bsz * seq, nv), out_dtype), jax.ShapeDtypeStruct(s0.shape, F32)],
        scratch_shapes=[pltpu.VMEM(s0.shape[1:], F32)],
        compiler_params=_cparams(("parallel", "arbitrary")),
        name="gla_bwd" if reverse else "gla_fwd",
    )(*args)


def _gla_all(p32, pc32, w2, gb, gain, *, bsz, seq, n_ctx, d):
    cols = _p32_cols(d)
    nq = GLA_HEADS * HEAD_DIM
    dv = d // 4 // GLA_HEADS
    w2pad = [jnp.zeros((V7X_LANES, nq), F32).at[GLA_RANK * k:GLA_RANK * (k + 1)].set(w2[k]).astype(BF16) for k in range(2)]
    gbias = [gb[k].reshape(1, nq) for k in range(2)]
    s0 = jnp.zeros((bsz, GLA_HEADS // 2, 2 * dv, PAIR_W), F32)
    gain2 = gain.reshape(1, dv)
    oc_f, sc_f = _gla_call(pc32, w2pad[0], gbias[0], s0, None, cols=cols, bsz=bsz, seq=n_ctx, reverse=False)
    y_ctx, sc_b = _gla_call(pc32, w2pad[1], gbias[1], s0, (oc_f, gain2), cols=cols, bsz=bsz, seq=n_ctx, reverse=True)
    o_f, _ = _gla_call(p32, w2pad[0], gbias[0], sc_f, None, cols=cols, bsz=bsz, seq=seq, reverse=False)
    y_lat, _ = _gla_call(p32, w2pad[1], gbias[1], sc_b, (o_f, gain2), cols=cols, bsz=bsz, seq=seq, reverse=True)
    return y_lat, y_ctx


def _swiglu(x, wg, wu, wd):
    g = jnp.dot(x, wg, preferred_element_type=F32)
    u = jnp.dot(x, wu, preferred_element_type=F32)
    a = (_silu(g) * u).astype(BF16)
    return jnp.dot(a, wd, preferred_element_type=F32)


def _first_argmax(vals, idx, sentinel):
    m = jnp.max(vals, axis=0, keepdims=True)
    first = jnp.min(jnp.where(vals == m, idx, sentinel), axis=0, keepdims=True)
    return m, first


def _router_kernel(h_ref, wr_ref, rb_ref, su_ref, eidx_ref, wts_ref, rank_ref, cnt_ref, run_ref):
    i = pl.program_id(0)

    @pl.when(i == 0)
    def _():
        run_ref[...] = jnp.zeros_like(run_ref)

    t = h_ref.shape[0]
    gsz = N_EXPERTS // N_GROUPS
    logits = lax.dot_general(wr_ref[...], h_ref[...], _NT, preferred_element_type=F32)
    scores = jax.nn.sigmoid(logits)
    sel = scores + rb_ref[...]
    sub = lax.broadcasted_iota(jnp.int32, (gsz, t), 0)
    gscore = []
    for g in range(N_GROUPS):
        blk = sel[g * gsz:(g + 1) * gsz]
        m1, a1 = _first_argmax(blk, sub, gsz)
        m2 = jnp.max(jnp.where(sub == a1, -jnp.inf, blk), axis=0, keepdims=True)
        gscore.append(m1 + m2)
    gcur = jnp.concatenate(gscore, axis=0)
    gid = lax.broadcasted_iota(jnp.int32, (N_GROUPS, t), 0)
    gkeep = jnp.zeros((N_GROUPS, t), F32)
    for _ in range(TOPK_GROUPS):
        _, a = _first_argmax(gcur, gid, N_GROUPS)
        hit = gid == a
        gkeep = jnp.where(hit, 1.0, gkeep)
        gcur = jnp.where(hit, -jnp.inf, gcur)
    cur = jnp.concatenate(
        [jnp.where(gkeep[g:g + 1] > 0.0, sel[g * gsz:(g + 1) * gsz], -jnp.inf) for g in range(N_GROUPS)], axis=0)
    eid = lax.broadcasted_iota(jnp.int32, (N_EXPERTS, t), 0)
    chosen = jnp.zeros((N_EXPERTS, t), F32)
    hits, picks, wraw = [], [], []
    for _ in range(TOP_K):
        _, a = _first_argmax(cur, eid, N_EXPERTS)
        hit = eid == a
        hits.append(hit)
        picks.append(a)
        wraw.append(jnp.sum(jnp.where(hit, scores, 0.0), axis=0, keepdims=True))
        chosen = jnp.where(hit, 1.0, chosen)
        cur = jnp.where(hit, -jnp.inf, cur)
    wsum = wraw[0]
    for w in wraw[1:]:
        wsum = wsum + w
    eidx_ref[...] = jnp.concatenate(picks, axis=0)
    wts_ref[...] = jnp.concatenate([w / wsum * ROUTED_SCALE for w in wraw], axis=0)
    before = jnp.dot(chosen.astype(BF16), su_ref[...], preferred_element_type=F32) + run_ref[...][:, 0:1]
    rank_ref[...] = jnp.concatenate(
        [jnp.sum(jnp.where(hit, before, 0.0), axis=0, keepdims=True) for hit in hits], axis=0).astype(jnp.int32)
    run_ref[...] = run_ref[...] + jnp.sum(chosen, axis=1, keepdims=True)
    cnt_ref[...] = run_ref[...]


def _router_call(h, w_router, router_bias):
    n, d = h.shape
    t = _pick_tile(n, ROUTER_TILE)
    wr_t = w_router.T.astype(BF16)
    strict_upper = jnp.triu(jnp.ones((t, t), F32), 1).astype(BF16)
    const2 = lambda i: (0, 0)
    tok = lambda i: (0, i)
    eidx, wts, rank, cnt = pl.pallas_call(
        _router_kernel,
        grid=(n // t,),
        in_specs=[pl.BlockSpec((t, d), lambda i: (i, 0)), pl.BlockSpec((N_EXPERTS, d), const2),
                  pl.BlockSpec((N_EXPERTS, 1), const2), pl.BlockSpec((t, t), const2)],
        out_specs=[pl.BlockSpec((TOP_K, t), tok), pl.BlockSpec((TOP_K, t), tok), pl.BlockSpec((TOP_K, t), tok),
                   pl.BlockSpec((N_EXPERTS, V7X_LANES), const2)],
        out_shape=[jax.ShapeDtypeStruct((TOP_K, n), jnp.int32), jax.ShapeDtypeStruct((TOP_K, n), F32),
                   jax.ShapeDtypeStruct((TOP_K, n), jnp.int32), jax.ShapeDtypeStruct((N_EXPERTS, V7X_LANES), F32)],
        scratch_shapes=[pltpu.VMEM((N_EXPERTS, V7X_LANES), F32)],
        compiler_params=_cparams(("arbitrary",)),
        name="router",
    )(h, wr_t, router_bias.reshape(N_EXPERTS, 1).astype(F32), strict_upper)
    return eidx, wts, rank, cnt[:, 0].astype(jnp.int32)


def _dispatch_kernel(zoff_ref, dest_ref, hp_ref, xs_hbm, zbuf, sem, zsem):
    i = pl.program_id(0)
    t = dest_ref.shape[1]

    @pl.when(i == 0)
    def _():
        zbuf[...] = jnp.zeros_like(zbuf)
        for e in range(N_EXPERTS):
            off = pl.multiple_of(zoff_ref[e], MOE_TB)
            pltpu.make_async_copy(zbuf, xs_hbm.at[pl.ds(off, MOE_TB)], zsem).start()
        for e in range(N_EXPERTS):
            pltpu.make_async_copy(zbuf, xs_hbm.at[pl.ds(0, MOE_TB)], zsem).wait()

    def row_copy(j, slot):
        return pltpu.make_async_copy(hp_ref.at[pl.ds(j, 1)], xs_hbm.at[pl.ds(slot, 1)], sem)

    def issue(j, carry):
        for k in range(TOP_K):
            row_copy(j, dest_ref[k, j]).start()
        return carry

    def drain(j, carry):
        for k in range(TOP_K):
            row_copy(0, 0).wait()
        return carry

    lax.fori_loop(0, t, issue, 0)
    lax.fori_loop(0, t, drain, 0)


def _dispatch_call(zero_off, dest, h_packed, n_rows):
    n, c = h_packed.shape
    t = _pick_tile(n, DISPATCH_TILE)
    grid_spec = pltpu.PrefetchScalarGridSpec(
        num_scalar_prefetch=1,
        grid=(n // t,),
        in_specs=[pl.BlockSpec((TOP_K, t), lambda i, z: (0, i), memory_space=pltpu.SMEM),
                  pl.BlockSpec((t, c), lambda i, z: (i, 0))],
        out_specs=pl.BlockSpec(memory_space=pl.ANY),
        scratch_shapes=[pltpu.VMEM((MOE_TB, c), jnp.uint32), pltpu.SemaphoreType.DMA, pltpu.SemaphoreType.DMA],
    )
    return pl.pallas_call(
        _dispatch_kernel,
        grid_spec=grid_spec,
        out_shape=jax.ShapeDtypeStruct((n_rows, c), jnp.uint32),
        compiler_params=_cparams(("arbitrary",)),
        name="moe_dispatch",
    )(zero_off, dest, h_packed)


def _expert_kernel(be_ref, nused_ref, x_ref, wg_ref, wu_ref, wd_ref, y_ref, wg_s, wu_s, wd_s):
    b = pl.program_id(0)
    prev = be_ref[jnp.maximum(b - 1, 0)]

    @pl.when((b == 0) | (be_ref[b] != prev))
    def _():
        wg_s[...] = wg_ref[...].astype(BF16)
        wu_s[...] = wu_ref[...].astype(BF16)
        wd_s[...] = wd_ref[...].astype(BF16)

    @pl.when(b < nused_ref[0])
    def _():
        y_ref[...] = _swiglu(_unpack_halves(x_ref[...]), wg_s[...], wu_s[...], wd_s[...])

    @pl.when(b >= nused_ref[0])
    def _():
        y_ref[...] = jnp.zeros_like(y_ref)


def _expert_call(block_e, n_used, x_sorted, wg, wu, wd, layer):
    n_rows, c = x_sorted.shape
    _, _, d, de = wg.shape
    n_blocks = n_rows // MOE_TB
    wsel = lambda b, be, nu: (layer, be[b], 0, 0)
    xsel = lambda b, be, nu: (jnp.minimum(b, nu[0] - 1), 0)
    grid_spec = pltpu.PrefetchScalarGridSpec(
        num_scalar_prefetch=2,
        grid=(n_blocks,),
        in_specs=[pl.BlockSpec((MOE_TB, c), xsel), pl.BlockSpec((None, None, d, de), wsel),
                  pl.BlockSpec((None, None, d, de), wsel), pl.BlockSpec((None, None, de, d), wsel)],
        out_specs=pl.BlockSpec((MOE_TB, d), lambda b, be, nu: (b, 0)),
        scratch_shapes=[pltpu.VMEM((d, de), BF16), pltpu.VMEM((d, de), BF16), pltpu.VMEM((de, d), BF16)],
    )
    return pl.pallas_call(
        _expert_kernel,
        grid_spec=grid_spec,
        out_shape=jax.ShapeDtypeStruct((n_rows, d), F32),
        compiler_params=_cparams(("arbitrary",)),
        name="experts",
    )(block_e, n_used, x_sorted, wg, wu, wd)


def _combine_kernel(dest_ref, w_ref, h_ref, x_ref, g2_ref, wsg_ref, wsu_ref, wsd_ref, lng_ref, lnb_ref, y_hbm,
                    o_ref, ybuf, sem, *, alpha):
    t = h_ref.shape[0]

    def row_copy(slot, k, j):
        return pltpu.make_async_copy(y_hbm.at[pl.ds(slot, 1)], ybuf.at[k, pl.ds(j, 1)], sem)

    def issue(j, carry):
        for k in range(TOP_K):
            row_copy(dest_ref[k, j], k, j).start()
        return carry

    def drain(j, carry):
        for k in range(TOP_K):
            row_copy(0, k, j).wait()
        return carry

    lax.fori_loop(0, t, issue, 0)
    acc = _swiglu(h_ref[...], wsg_ref[...], wsu_ref[...], wsd_ref[...])
    lax.fori_loop(0, t, drain, 0)
    w = w_ref[...]
    for k in range(TOP_K):
        acc = acc + w[:, k:k + 1] * ybuf[k]
    r = alpha * x_ref[...] + g2_ref[0] * acc
    o_ref[...] = _ln_rows(r) * lng_ref[...] + lnb_ref[...]


def _combine_call(dest, wts_t, h, x2d, gate2, wsg, wsu, wsd, ln_g, ln_b, y_sorted, *, row0, rows_per_group, alpha):
    m, d = x2d.shape
    de = wsg.shape[1]
    t = _pick_tile(min(rows_per_group, m), COMBINE_TILE)
    assert row0 % t == 0 and rows_per_group % t == 0
    t0 = row0 // t
    const2 = lambda i: (0, 0)
    once = pl.Buffered(1)
    return pl.pallas_call(
        functools.partial(_combine_kernel, alpha=alpha),
        grid=(m // t,),
        in_specs=[pl.BlockSpec((TOP_K, t), lambda i: (0, t0 + i), memory_space=pltpu.SMEM),
                  pl.BlockSpec((t, TOP_K), lambda i: (t0 + i, 0)),
                  pl.BlockSpec((t, d), lambda i: (t0 + i, 0)),
                  pl.BlockSpec((t, d), lambda i: (i, 0)),
                  pl.BlockSpec((1, 1, d), lambda i: ((i * t) // rows_per_group, 0, 0)),
                  pl.BlockSpec((d, de), const2, pipeline_mode=once),
                  pl.BlockSpec((d, de), const2, pipeline_mode=once),
                  pl.BlockSpec((de, d), const2, pipeline_mode=once),
                  pl.BlockSpec((1, d), const2), pl.BlockSpec((1, d), const2),
                  pl.BlockSpec(memory_space=pl.ANY)],
        out_specs=pl.BlockSpec((t, d), lambda i: (i, 0)),
        out_shape=jax.ShapeDtypeStruct((m, d), F32),
        scratch_shapes=[pltpu.VMEM((TOP_K, t, d), F32), pltpu.SemaphoreType.DMA],
        compiler_params=_cparams(("arbitrary",)),
        name="moe_combine",
    )(dest, wts_t, h, x2d, gate2, wsg, wsu, wsd, ln_g.reshape(1, d), ln_b.reshape(1, d), y_sorted)


def _moe_routed(h, h_packed, w_router, router_bias, w_eg, w_eu, w_ed, layer):
    n = h.shape[0]
    eidx, wts, rank, counts = _router_call(h, w_router, router_bias)
    padded = (counts + MOE_TB - 1) // MOE_TB * MOE_TB
    pad_end = jnp.cumsum(padded)
    pad_start = pad_end - padded
    n_blocks = (n * TOP_K + N_EXPERTS * (MOE_TB - 1) + MOE_TB - 1) // MOE_TB
    block_first = jnp.arange(n_blocks, dtype=jnp.int32) * MOE_TB
    block_e = jnp.minimum(jnp.sum((pad_end[None, :] <= block_first[:, None]).astype(jnp.int32), axis=1), N_EXPERTS - 1)
    n_used = (pad_end[-1:] // MOE_TB).astype(jnp.int32)
    zero_off = jnp.maximum(pad_end - MOE_TB, 0).astype(jnp.int32)
    expert_ids = jnp.arange(N_EXPERTS, dtype=jnp.int32)
    first_row = jnp.sum(jnp.where(eidx[:, :, None] == expert_ids, pad_start.astype(jnp.int32), 0), axis=-1)
    dest = first_row + rank
    x_sorted = _dispatch_call(zero_off, dest, h_packed, n_blocks * MOE_TB)
    y_sorted = _expert_call(block_e, n_used, x_sorted, w_eg, w_eu, w_ed, layer)
    return dest, wts.T, y_sorted


def _pack_w_in(w, d):
    gw = d // 4
    cuts = np.cumsum([gw, gw, gw, gw, gw, gw, gw // 2, gw // 2, gw, gw, 2 * GLA_RANK, gw, gw // 4, gw // 4])[:-1].tolist()
    (na_q, na_k, na_v, sc_u, sc_b, sc_c, gl_q, gl_k, gl_v, gl_g, gl_z, sw_q, sw_k, sw_v) = jnp.split(w, cuts, axis=1)
    zpad = jnp.zeros((d, V7X_LANES - 2 * GLA_RANK), w.dtype)
    cols = [na_q, na_k, na_v, sc_u, sc_b, sc_c, gl_v, gl_g, sw_q, gl_q, gl_k, sw_k, sw_v, gl_z, zpad]
    packed = jnp.concatenate(cols, axis=1)
    pad = (-packed.shape[1]) % PROJ_TN
    packed = jnp.pad(packed, ((0, 0), (0, pad)))
    return packed.astype(BF16)


def _p32_cols(d):
    gw = d // 4
    cols, c = {}, 0
    for name, wdt in (('sc_u', gw), ('sc_b', gw), ('sc_c', gw), ('gl_v', gw), ('gl_g', gw), ('sw_q', gw),
                      ('gl_q', gw // 2), ('gl_k', gw // 2), ('sw_k', gw // 4), ('sw_v', gw // 4), ('gl_z', V7X_LANES)):
        cols[name] = c
        c += wdt
    return cols


def _token_mixers(p16, p32, pc16, pc32, rpb, conv_w, gla_w2, gla_b, gla_norm_g, sink, with_ctx_out, *,
                  bsz, seq, n_ctx, d):
    cols = _p32_cols(d)
    kh = min(NA_KH, seq // GRID_W)
    y_na = _na_call(p16, pc16, _na_bias_table(rpb, kh), bsz=bsz, seq=seq, n_ctx=n_ctx)
    conv_cols = dict(col_u=cols['sc_u'], col_b=cols['sc_b'], col_c=cols['sc_c'])
    y_sc = _conv_call(p32, conv_w, rows_per_seq=seq, **conv_cols)
    y_gl, yc_gl = _gla_all(p32, pc32, gla_w2, gla_b, gla_norm_g, bsz=bsz, seq=seq, n_ctx=n_ctx, d=d)
    tables = _rope_tables(seq)
    swa_cols = dict(col_q=cols['sw_q'], col_k=cols['sw_k'], col_v=cols['sw_v'])
    qr, kd, vd = _swa_prep_call(p32, tables, rows_per_seq=seq, rope=True, **swa_cols)
    ctx_tables = tuple(t[:n_ctx] for t in tables)
    qx, kxd, vxd = _swa_prep_call(pc32, ctx_tables, rows_per_seq=n_ctx, rope=False, **swa_cols)
    y_sw = _swa_call(sink, qr, kd, vd, kxd, vxd, bsz=bsz, seq=seq, n_ctx=n_ctx)
    y_lat = (y_na, y_sc, y_gl, y_sw)
    if not with_ctx_out:
        return y_lat, None
    yc_na, yc_sw = _ctx_attn_call(sink, pc16, qx, kxd, vxd, bsz=bsz, n_ctx=n_ctx)
    yc_sc = _conv_call(pc32, conv_w, rows_per_seq=n_ctx, **conv_cols)
    return y_lat, (yc_na, yc_sc, yc_gl, yc_sw)


def kernel(x, c, ctx, c_ctx, w_ada, b_ada, w_in, na_rpb, conv_w, gla_w2, gla_b, gla_norm_g, swa_sink, w_out,
           ln1_g, ln1_b, w_router, router_bias, w_exp_gate, w_exp_up, w_exp_down, w_sh_gate, w_sh_up, w_sh_down,
           ln2_g, ln2_b):
    bsz, seq, d = x.shape
    n_ctx = ctx.shape[1]
    nc = bsz * n_ctx
    depth = w_in.shape[0]
    alpha = (2 * depth) ** 0.25
    n16 = 3 * (d // 4)
    x2 = x.reshape(bsz * seq, d)
    hc2 = ctx.reshape(nc, d)
    c_rows = jnp.zeros((8, d), F32).at[:bsz].set(c).at[bsz].set(c_ctx)
    for layer in range(depth):
        last = layer == depth - 1
        mod = _ada_call(c_rows, w_ada, b_ada[layer], layer)
        sh1, sc1, g1, sh2, sc2, g2 = [t[:bsz, None, :] for t in jnp.split(mod, 6, axis=-1)]
        sh1c, sc1c, g1c, sh2c, sc2c, g2c = [t[bsz:bsz + 1, None, :] for t in jnp.split(mod, 6, axis=-1)]
        w_p = _pack_w_in(w_in[layer], d)
        p16, p32 = _proj_call(x2, sh1, sc1, w_p, rows_per_group=seq, n16=n16)
        pc16, pc32 = _proj_call(hc2, sh1c, sc1c, w_p, rows_per_group=nc, n16=n16)
        y_lat, y_ctx = _token_mixers(p16, p32, pc16, pc32, na_rpb[layer], conv_w[layer], gla_w2[layer], gla_b[layer],
                                     gla_norm_g[layer], swa_sink[layer], not last, bsz=bsz, seq=seq, n_ctx=n_ctx, d=d)
        w_o = w_out[layer].astype(BF16)
        x2, h_lat, hp_lat = _outproj_call(y_lat, x2, g1, w_o, ln1_g[layer], ln1_b[layer], sh2, sc2,
                                          rows_per_group=seq, alpha=alpha)
        shared_w = (w_sh_gate[layer].astype(BF16), w_sh_up[layer].astype(BF16), w_sh_down[layer].astype(BF16))
        route_w = (w_router[layer], router_bias[layer], w_exp_gate, w_exp_up, w_exp_down, layer)
        if last:
            dest, wts_t, y_sorted = _moe_routed(h_lat, hp_lat, *route_w)
            x2 = _combine_call(dest, wts_t, h_lat, x2, g2, *shared_w, ln2_g[layer], ln2_b[layer], y_sorted,
                               row0=0, rows_per_group=seq, alpha=alpha)
        else:
            hc2, h_ctx, hp_ctx = _outproj_call(y_ctx, hc2, g1c, w_o, ln1_g[layer], ln1_b[layer], sh2c, sc2c,
                                               rows_per_group=nc, alpha=alpha)
            h_all = jnp.concatenate([h_ctx, h_lat], axis=0)
            hp_all = jnp.concatenate([hp_ctx, hp_lat], axis=0)
            dest, wts_t, y_sorted = _moe_routed(h_all, hp_all, *route_w)
            hc2 = _combine_call(dest, wts_t, h_all, hc2, g2c, *shared_w, ln2_g[layer], ln2_b[layer], y_sorted,
                                row0=0, rows_per_group=nc, alpha=alpha)
            x2 = _combine_call(dest, wts_t, h_all, x2, g2, *shared_w, ln2_g[layer], ln2_b[layer], y_sorted,
                               row0=nc, rows_per_group=seq, alpha=alpha)
    return x2.reshape(bsz, seq, d)
```

```python
import functools

import jax
import jax.numpy as jnp
import numpy as np
from jax import lax
from jax.experimental import pallas as pl
from jax.experimental.pallas import tpu as pltpu

GRID_W = 64
HEAD_DIM = 64
NA_KH = 8
NA_KW = 16
SC_KSIZE = 3
GLA_HEADS = 4
GLA_RANK = 16
GLA_TAU = 16.0
SWA_KV_HEADS = 2
SWA_BLOCK = 128
ROPE_BASE = 10000.0
N_EXPERTS = 64
TOP_K = 8
N_GROUPS = 8
TOPK_GROUPS = 4
ROUTED_SCALE = 2.5
LN_EPS = 1e-6

V7X_LANES = 128
V7X_VMEM_LIMIT_BYTES = 48 * 1024 * 1024

PROJ_TM = 1024
PROJ_TN = 512
OUT_TM = 256
MOE_TB = 256
ROUTER_TILE = 512
DISPATCH_TILE = 256
COMBINE_TILE = 128
CONV_HALO = 8
GLA_C = 64
GLA_TILE = 512
ADA_TN = 2048

F32 = jnp.float32
BF16 = jnp.bfloat16
PAIR_W = 2 * HEAD_DIM
ATTN_SCALE = HEAD_DIM ** -0.5
_NT = (((1,), (1,)), ((), ()))
_TN = (((0,), (0,)), ((), ()))


def _cparams(sem):
    return pltpu.CompilerParams(dimension_semantics=sem, vmem_limit_bytes=V7X_VMEM_LIMIT_BYTES)


def _pick_tile(m, preferred):
    t = preferred
    while t > 8 and m % t:
        t //= 2
    assert m % t == 0, (m, preferred)
    return t


def _ln_rows(x):
    mu = jnp.mean(x, axis=-1, keepdims=True)
    xc = x - mu
    var = jnp.mean(xc * xc, axis=-1, keepdims=True)
    return xc * lax.rsqrt(var + LN_EPS)


def _silu(x):
    return x * jax.nn.sigmoid(x)


def _ada_kernel(c_ref, w_ref, b_ref, o_ref):
    a = _silu(c_ref[...]).astype(BF16)
    o_ref[...] = jnp.dot(a, w_ref[...].astype(BF16), preferred_element_type=F32) + b_ref[...]


def _ada_call(c_rows, w_ada, b_ada, layer):
    r, d = c_rows.shape
    n = w_ada.shape[2]
    tn = _pick_tile(n, ADA_TN)
    return pl.pallas_call(
        _ada_kernel,
        grid=(n // tn,),
        in_specs=[pl.BlockSpec((r, d), lambda j: (0, 0)), pl.BlockSpec((None, d, tn), lambda j: (layer, 0, j)),
                  pl.BlockSpec((1, tn), lambda j: (0, j))],
        out_specs=pl.BlockSpec((r, tn), lambda j: (0, j)),
        out_shape=jax.ShapeDtypeStruct((r, n), F32),
        compiler_params=_cparams(("parallel",)),
        name="ada_mod",
    )(c_rows, w_ada, b_ada.reshape(1, n))


def _proj_kernel(x_ref, sh_ref, sc_ref, w_ref, o16_ref, o32_ref, xn_ref, *, nb16):
    j = pl.program_id(1)

    @pl.when(j == 0)
    def _():
        y = _ln_rows(x_ref[...]) * (1.0 + sc_ref[0]) + sh_ref[0]
        xn_ref[...] = y.astype(BF16)

    acc = jnp.dot(xn_ref[...], w_ref[...], preferred_element_type=F32)

    @pl.when(j < nb16)
    def _():
        o16_ref[...] = acc.astype(BF16)

    @pl.when(j >= nb16)
    def _():
        o32_ref[...] = acc


def _proj_call(x2d, shift, scale, w_packed, *, rows_per_group, n16):
    m, d = x2d.shape
    ntot = w_packed.shape[1]
    tm = min(PROJ_TM, rows_per_group)
    assert m % tm == 0 and rows_per_group % tm == 0 and ntot % PROJ_TN == 0 and n16 % PROJ_TN == 0
    nb16 = n16 // PROJ_TN
    nb = ntot // PROJ_TN
    grp = lambda i, j: ((i * tm) // rows_per_group, 0, 0)
    return pl.pallas_call(
        functools.partial(_proj_kernel, nb16=nb16),
        grid=(m // tm, nb),
        in_specs=[
            pl.BlockSpec((tm, d), lambda i, j: (i, 0)),
            pl.BlockSpec((1, 1, d), grp),
            pl.BlockSpec((1, 1, d), grp),
            pl.BlockSpec((d, PROJ_TN), lambda i, j: (0, j)),
        ],
        out_specs=[
            pl.BlockSpec((tm, PROJ_TN), lambda i, j: (i, jnp.minimum(j, nb16 - 1))),
            pl.BlockSpec((tm, PROJ_TN), lambda i, j: (i, jnp.maximum(j - nb16, 0))),
        ],
        out_shape=[jax.ShapeDtypeStruct((m, n16), BF16), jax.ShapeDtypeStruct((m, ntot - n16), F32)],
        scratch_shapes=[pltpu.VMEM((tm, d), BF16)],
        compiler_params=_cparams(("parallel", "arbitrary")),
        name="proj",
    )(x2d, shift, scale, w_packed)


def _pack_halves(h):
    c = h.shape[1] // 2
    lo = pltpu.bitcast(h[:, :c].astype(BF16).astype(F32), jnp.uint32)
    hi = pltpu.bitcast(h[:, c:].astype(BF16).astype(F32), jnp.uint32)
    return lax.shift_right_logical(lo, jnp.uint32(16)) | (hi & jnp.uint32(0xFFFF0000))


def _unpack_halves(w):
    lo = pltpu.bitcast(lax.shift_left(w, jnp.uint32(16)), F32)
    hi = pltpu.bitcast(w & jnp.uint32(0xFFFF0000), F32)
    return jnp.concatenate([lo, hi], axis=1).astype(BF16)


def _outproj_kernel(y0_ref, y1_ref, y2_ref, y3_ref, x_ref, g1_ref, w_ref, lng_ref, lnb_ref,
                    sh2_ref, sc2_ref, xo_ref, h_ref, hp_ref, *, alpha):
    gw = y0_ref.shape[1]
    acc = jnp.dot(y0_ref[...], w_ref[0:gw, :], preferred_element_type=F32)
    acc += jnp.dot(y1_ref[...], w_ref[gw:2 * gw, :], preferred_element_type=F32)
    acc += jnp.dot(y2_ref[...], w_ref[2 * gw:3 * gw, :], preferred_element_type=F32)
    acc += jnp.dot(y3_ref[...], w_ref[3 * gw:4 * gw, :], preferred_element_type=F32)
    r = alpha * x_ref[...] + g1_ref[0] * acc
    xn = _ln_rows(r) * lng_ref[...] + lnb_ref[...]
    xo_ref[...] = xn
    h = _ln_rows(xn) * (1.0 + sc2_ref[0]) + sh2_ref[0]
    h_ref[...] = h.astype(BF16)
    hp_ref[...] = _pack_halves(h)


def _outproj_call(ys, x2d, gate1, w_out_bf16, ln_g, ln_b, shift2, scale2, *, rows_per_group, alpha):
    m, d = x2d.shape
    gw = d // 4
    tm = min(OUT_TM, rows_per_group)
    assert m % tm == 0 and rows_per_group % tm == 0
    grp = lambda i: ((i * tm) // rows_per_group, 0, 0)
    row = lambda i: (i, 0)
    const2 = lambda i: (0, 0)
    return pl.pallas_call(
        functools.partial(_outproj_kernel, alpha=alpha),
        grid=(m // tm,),
        in_specs=[pl.BlockSpec((tm, gw), row)] * 4 + [
            pl.BlockSpec((tm, d), row),
            pl.BlockSpec((1, 1, d), grp),
            pl.BlockSpec((d, d), const2, pipeline_mode=pl.Buffered(1)),
            pl.BlockSpec((1, d), const2),
            pl.BlockSpec((1, d), const2),
            pl.BlockSpec((1, 1, d), grp),
            pl.BlockSpec((1, 1, d), grp),
        ],
        out_specs=[pl.BlockSpec((tm, d), row), pl.BlockSpec((tm, d), row), pl.BlockSpec((tm, d // 2), row)],
        out_shape=[jax.ShapeDtypeStruct((m, d), F32), jax.ShapeDtypeStruct((m, d), BF16),
                   jax.ShapeDtypeStruct((m, d // 2), jnp.uint32)],
        compiler_params=_cparams(("parallel",)),
        name="outproj",
    )(*ys, x2d, gate1, w_out_bf16, ln_g.reshape(1, d), ln_b.reshape(1, d), shift2, scale2)


def _stack_pair(q2):
    lane = lax.broadcasted_iota(jnp.int32, q2.shape, 1)
    lo = jnp.where(lane < HEAD_DIM, q2, 0.0)
    hi = jnp.where(lane >= HEAD_DIM, q2, 0.0)
    return jnp.concatenate([lo, hi], axis=0).astype(BF16)


def _unstack_pair(o):
    n = o.shape[0] // 2
    lane = lax.broadcasted_iota(jnp.int32, (n, o.shape[1]), 1)
    return jnp.where(lane < HEAD_DIM, o[:n], o[n:])


def _pair_softmax_av(qs, ks, vs, biases, masks, sink_col):
    ss = []
    for k, bia, msk in zip(ks, biases, masks):
        s = lax.dot_general(qs, k, _NT, preferred_element_type=F32)
        if bia is not None:
            s = s + bia
        if msk is not None:
            s = jnp.where(msk, s, -jnp.inf)
        ss.append(s)
    m = jnp.max(ss[0], axis=-1, keepdims=True)
    for s in ss[1:]:
        m = jnp.maximum(m, jnp.max(s, axis=-1, keepdims=True))
    if sink_col is not None:
        m = jnp.maximum(m, sink_col)
        l = jnp.exp(sink_col - m)
    else:
        l = jnp.zeros_like(m)
    o = None
    for s, v in zip(ss, vs):
        e = jnp.exp(s - m)
        l = l + jnp.sum(e, axis=-1, keepdims=True)
        pv = jnp.dot(e.astype(BF16), v, preferred_element_type=F32)
        o = pv if o is None else o + pv
    return o / l


def _na_kernel(q_ref, k_ref, v_ref, kc_ref, vc_ref, bias_ref, o_ref, *, rows, kh):
    r = pl.program_id(1)
    rs = jnp.clip(r - kh // 2, 0, rows - kh)
    start = pl.multiple_of(rs * GRID_W, GRID_W)
    nwin = kh * GRID_W
    for p in range(q_ref.shape[1] // PAIR_W):
        sl = slice(p * PAIR_W, (p + 1) * PAIR_W)
        qs = _stack_pair(q_ref[:, sl].astype(F32) * ATTN_SCALE)
        kw = k_ref[pl.ds(start, nwin), sl]
        vw = v_ref[pl.ds(start, nwin), sl]
        o = _pair_softmax_av(qs, [kw, kc_ref[:, sl]], [vw, vc_ref[:, sl]], [bias_ref[p], None], [None, None], None)
        o_ref[:, sl] = _unstack_pair(o).astype(o_ref.dtype)


def _na_bias_table(rpb, kh):
    nh = rpb.shape[0]
    c = jnp.arange(GRID_W)
    cstart = jnp.clip(c - NA_KW // 2, 0, GRID_W - NA_KW)
    valid = (c[None, :] >= cstart[:, None]) & (c[None, :] < cstart[:, None] + NA_KW)
    coff = jnp.clip(c[None, :] - c[:, None], 1 - NA_KW, NA_KW - 1) + NA_KW - 1
    roff = jnp.arange(kh)[None, :] - jnp.arange(kh)[:, None] + NA_KH - 1
    pick_r = (roff[:, :, None] == jnp.arange(rpb.shape[1])).astype(F32)
    pick_c = (coff[:, :, None] == jnp.arange(rpb.shape[2])).astype(F32)
    bias = jnp.einsum('hab,dia,ckb->hdick', rpb.astype(F32), pick_r, pick_c, precision=lax.Precision.HIGHEST)
    bias = jnp.where(valid[None, None, None], bias, -jnp.inf)
    bias = bias.transpose(1, 0, 3, 2, 4)
    return bias.reshape(kh, nh // 2, 2 * GRID_W, kh * GRID_W)


def _na_call(p16, pc16, bias_tab, *, bsz, seq, n_ctx):
    gw = p16.shape[1] // 3
    rows = seq // GRID_W
    kh = bias_tab.shape[0]
    delta = lambda b, r: (r - jnp.clip(r - kh // 2, 0, rows - kh), 0, 0, 0)
    once = pl.Buffered(1)
    return pl.pallas_call(
        functools.partial(_na_kernel, rows=rows, kh=kh),
        grid=(bsz, rows),
        in_specs=[
            pl.BlockSpec((GRID_W, gw), lambda b, r: (b * rows + r, 0)),
            pl.BlockSpec((seq, gw), lambda b, r: (b, 1), pipeline_mode=once),
            pl.BlockSpec((seq, gw), lambda b, r: (b, 2), pipeline_mode=once),
            pl.BlockSpec((n_ctx, gw), lambda b, r: (b, 1)),
            pl.BlockSpec((n_ctx, gw), lambda b, r: (b, 2)),
            pl.BlockSpec((None,) + bias_tab.shape[1:], delta),
        ],
        out_specs=pl.BlockSpec((GRID_W, gw), lambda b, r: (b * rows + r, 0)),
        out_shape=jax.ShapeDtypeStruct((bsz * seq, gw), BF16),
        compiler_params=_cparams(("parallel", "arbitrary")),
        name="na_attn",
    )(p16, p16, p16, pc16, pc16, bias_tab)


def _dup_heads(t):
    lane = lax.broadcasted_iota(jnp.int32, t.shape, 1)
    sw = pltpu.roll(t, HEAD_DIM, 1)
    return jnp.concatenate([jnp.where(lane < HEAD_DIM, t, sw), jnp.where(lane < HEAD_DIM, sw, t)], axis=1)


def _rope(t, cos, sa, sb):
    q = HEAD_DIM // 4
    return t * cos + pltpu.roll(t, q, 1) * sa + pltpu.roll(t, V7X_LANES - q, 1) * sb


def _swa_prep_kernel(q_ref, k_ref, v_ref, cos_ref, sa_ref, sb_ref, qo_ref, ko_ref, vo_ref, *, rope):
    k = k_ref[...]
    if rope:
        cos, sa, sb = cos_ref[...], sa_ref[...], sb_ref[...]
        k = _rope(k, cos, sa, sb)
    for p in range(q_ref.shape[1] // PAIR_W):
        sl = slice(p * PAIR_W, (p + 1) * PAIR_W)
        q = q_ref[:, sl]
        if rope:
            q = _rope(q, cos, sa, sb)
        qo_ref[:, sl] = (q * ATTN_SCALE).astype(BF16)
    ko_ref[...] = _dup_heads(k).astype(BF16)
    vo_ref[...] = _dup_heads(v_ref[...]).astype(BF16)


def _rope_tables(seq):
    t = jnp.arange(seq)
    row = (t // GRID_W).astype(F32)
    col = (t % GRID_W).astype(F32)
    quarter = HEAD_DIM // 4
    inv = ROPE_BASE ** (-2.0 * jnp.arange(quarter, dtype=F32) / (HEAD_DIM // 2))
    ang_r = row[:, None] * inv[None, :]
    ang_c = col[:, None] * inv[None, :]
    zero = jnp.zeros_like(ang_r)
    cos_h = jnp.concatenate([jnp.cos(ang_r)] * 2 + [jnp.cos(ang_c)] * 2, axis=1)
    sa_h = jnp.concatenate([zero, jnp.sin(ang_r), zero, jnp.sin(ang_c)], axis=1)
    sb_h = jnp.concatenate([-jnp.sin(ang_r), zero, -jnp.sin(ang_c), zero], axis=1)
    two = lambda a: jnp.concatenate([a, a], axis=1)
    return two(cos_h), two(sa_h), two(sb_h)


def _swa_prep_call(p32, tables, *, col_q, col_k, col_v, rows_per_seq, rope):
    m = p32.shape[0]
    gw = 4 * PAIR_W
    tm = _pick_tile(rows_per_seq, 512)
    nseq_tiles = rows_per_seq // tm
    tab = lambda i: (i % nseq_tiles, 0)
    return pl.pallas_call(
        functools.partial(_swa_prep_kernel, rope=rope),
        grid=(m // tm,),
        in_specs=[
            pl.BlockSpec((tm, gw), lambda i: (i, col_q // gw)),
            pl.BlockSpec((tm, PAIR_W), lambda i: (i, col_k // PAIR_W)),
            pl.BlockSpec((tm, PAIR_W), lambda i: (i, col_v // PAIR_W)),
            pl.BlockSpec((tm, PAIR_W), tab),
            pl.BlockSpec((tm, PAIR_W), tab),
            pl.BlockSpec((tm, PAIR_W), tab),
        ],
        out_specs=[pl.BlockSpec((tm, gw), lambda i: (i, 0)), pl.BlockSpec((tm, 2 * PAIR_W), lambda i: (i, 0)),
                   pl.BlockSpec((tm, 2 * PAIR_W), lambda i: (i, 0))],
        out_shape=[jax.ShapeDtypeStruct((m, gw), BF16), jax.ShapeDtypeStruct((m, 2 * PAIR_W), BF16),
                   jax.ShapeDtypeStruct((m, 2 * PAIR_W), BF16)],
        compiler_params=_cparams(("parallel",)),
        name="swa_prep",
    )(p32, p32, p32, *tables)


def _sink_col(sink_ref, p, n):
    row = lax.broadcasted_iota(jnp.int32, (2 * n, 1), 0)
    return jnp.where(row < n, sink_ref[2 * p], sink_ref[2 * p + 1])


def _swa_kernel(sink_ref, q_ref, kp_ref, kc_ref, kn_ref, vp_ref, vc_ref, vn_ref, kx_ref, vx_ref, o_ref, *, nblk):
    n = pl.program_id(1)
    blk = q_ref.shape[0]
    qi = lax.broadcasted_iota(jnp.int32, (2 * blk, blk), 0) % blk
    kj = lax.broadcasted_iota(jnp.int32, (2 * blk, blk), 1)
    m_prev = kj >= qi + jnp.where(n > 0, 0, blk)
    m_next = kj <= qi - jnp.where(n < nblk - 1, 0, blk)
    npairs = q_ref.shape[1] // PAIR_W
    for p in range(npairs):
        sl = slice(p * PAIR_W, (p + 1) * PAIR_W)
        g = p // (npairs // SWA_KV_HEADS)
        gs = slice(g * PAIR_W, (g + 1) * PAIR_W)
        qs = _stack_pair(q_ref[:, sl].astype(F32))
        o = _pair_softmax_av(
            qs, [kp_ref[:, gs], kc_ref[:, gs], kn_ref[:, gs], kx_ref[:, gs]],
            [vp_ref[:, gs], vc_ref[:, gs], vn_ref[:, gs], vx_ref[:, gs]],
            [None] * 4, [m_prev, None, m_next, None], _sink_col(sink_ref, p, blk))
        o_ref[:, sl] = _unstack_pair(o).astype(o_ref.dtype)


def _swa_call(sink, qr, kd, vd, kxd, vxd, *, bsz, seq, n_ctx):
    gw = qr.shape[1]
    kw = kd.shape[1]
    nblk = seq // SWA_BLOCK
    cur = lambda b, n: (b * nblk + n, 0)
    prev = lambda b, n: (b * nblk + jnp.maximum(n - 1, 0), 0)
    nxt = lambda b, n: (b * nblk + jnp.minimum(n + 1, nblk - 1), 0)
    cx = lambda b, n: (b, 0)
    return pl.pallas_call(
        functools.partial(_swa_kernel, nblk=nblk),
        grid=(bsz, nblk),
        in_specs=[pl.BlockSpec(memory_space=pltpu.SMEM),
                  pl.BlockSpec((SWA_BLOCK, gw), cur),
                  pl.BlockSpec((SWA_BLOCK, kw), prev), pl.BlockSpec((SWA_BLOCK, kw), cur), pl.BlockSpec((SWA_BLOCK, kw), nxt),
                  pl.BlockSpec((SWA_BLOCK, kw), prev), pl.BlockSpec((SWA_BLOCK, kw), cur), pl.BlockSpec((SWA_BLOCK, kw), nxt),
                  pl.BlockSpec((n_ctx, kw), cx), pl.BlockSpec((n_ctx, kw), cx)],
        out_specs=pl.BlockSpec((SWA_BLOCK, gw), cur),
        out_shape=jax.ShapeDtypeStruct((bsz * seq, gw), BF16),
        compiler_params=_cparams(("parallel", "arbitrary")),
        name="swa_attn",
    )(sink, qr, kd, kd, kd, vd, vd, vd, kxd, vxd)


def _ctx_attn_kernel(sink_ref, qa_ref, ka_ref, va_ref, qd_ref, kd_ref, vd_ref, oa_ref, od_ref):
    n = qa_ref.shape[0]
    npairs = qa_ref.shape[1] // PAIR_W
    for p in range(npairs):
        sl = slice(p * PAIR_W, (p + 1) * PAIR_W)
        qs = _stack_pair(qa_ref[:, sl].astype(F32) * ATTN_SCALE)
        o = _pair_softmax_av(qs, [ka_ref[:, sl]], [va_ref[:, sl]], [None], [None], None)
        oa_ref[:, sl] = _unstack_pair(o).astype(oa_ref.dtype)
    for p in range(npairs):
        sl = slice(p * PAIR_W, (p + 1) * PAIR_W)
        g = p // (npairs // SWA_KV_HEADS)
        gs = slice(g * PAIR_W, (g + 1) * PAIR_W)
        qs = _stack_pair(qd_ref[:, sl].astype(F32))
        o = _pair_softmax_av(qs, [kd_ref[:, gs]], [vd_ref[:, gs]], [None], [None], _sink_col(sink_ref, p, n))
        od_ref[:, sl] = _unstack_pair(o).astype(od_ref.dtype)


def _ctx_attn_call(sink, pc16, qx, kxd, vxd, *, bsz, n_ctx):
    gw = qx.shape[1]
    kw = kxd.shape[1]
    return pl.pallas_call(
        _ctx_attn_kernel,
        grid=(bsz,),
        in_specs=[pl.BlockSpec(memory_space=pltpu.SMEM),
                  pl.BlockSpec((n_ctx, gw), lambda b: (b, 0)), pl.BlockSpec((n_ctx, gw), lambda b: (b, 1)),
                  pl.BlockSpec((n_ctx, gw), lambda b: (b, 2)), pl.BlockSpec((n_ctx, gw), lambda b: (b, 0)),
                  pl.BlockSpec((n_ctx, kw), lambda b: (b, 0)), pl.BlockSpec((n_ctx, kw), lambda b: (b, 0))],
        out_specs=[pl.BlockSpec((n_ctx, gw), lambda b: (b, 0)), pl.BlockSpec((n_ctx, gw), lambda b: (b, 0))],
        out_shape=[jax.ShapeDtypeStruct((bsz * n_ctx, gw), BF16)] * 2,
        compiler_params=_cparams(("parallel",)),
        name="ctx_attn",
    )(sink, pc16, pc16, pc16, qx, kxd, vxd)


def _conv_kernel(u_ref, b_ref, c_ref, up_ref, cp_ref, un_ref, cn_ref, w_ref, o_ref, *, tiles_per_seq):
    i = pl.program_id(0)
    tm = u_ref.shape[0]
    pos = i % tiles_per_seq
    keep_prev = jnp.where(pos == 0, 0.0, 1.0)
    keep_next = jnp.where(pos == tiles_per_seq - 1, 0.0, 1.0)
    z = c_ref[...] * u_ref[...]
    z_prev = (cp_ref[...] * up_ref[...])[CONV_HALO - 1:CONV_HALO, :] * keep_prev
    z_next = (cn_ref[...] * un_ref[...])[0:1, :] * keep_next
    row = lax.broadcasted_iota(jnp.int32, z.shape, 0)
    zm1 = jnp.where(row == 0, z_prev, pltpu.roll(z, 1, 0))
    zp1 = jnp.where(row == tm - 1, z_next, pltpu.roll(z, tm - 1, 0))
    w = w_ref[...]
    o_ref[...] = (b_ref[...] * (w[0:1] * zm1 + w[1:2] * z + w[2:3] * zp1)).astype(o_ref.dtype)


def _conv_call(p32, conv_w, *, col_u, col_b, col_c, rows_per_seq):
    m = p32.shape[0]
    gw = conv_w.shape[1]
    tm = _pick_tile(rows_per_seq, 512)
    hb = tm // CONV_HALO
    n_halo = m // CONV_HALO
    cur = lambda col: (lambda i: (i, col // gw))
    prv = lambda col: (lambda i: (jnp.maximum(i * hb - 1, 0), col // gw))
    nxt = lambda col: (lambda i: (jnp.minimum((i + 1) * hb, n_halo - 1), col // gw))
    return pl.pallas_call(
        functools.partial(_conv_kernel, tiles_per_seq=rows_per_seq // tm),
        grid=(m // tm,),
        in_specs=[pl.BlockSpec((tm, gw), cur(col_u)), pl.BlockSpec((tm, gw), cur(col_b)), pl.BlockSpec((tm, gw), cur(col_c)),
                  pl.BlockSpec((CONV_HALO, gw), prv(col_u)), pl.BlockSpec((CONV_HALO, gw), prv(col_c)),
                  pl.BlockSpec((CONV_HALO, gw), nxt(col_u)), pl.BlockSpec((CONV_HALO, gw), nxt(col_c)),
                  pl.BlockSpec((SC_KSIZE, gw), lambda i: (0, 0))],
        out_specs=pl.BlockSpec((tm, gw), lambda i: (i, 0)),
        out_shape=jax.ShapeDtypeStruct((m, gw), BF16),
        compiler_params=_cparams(("parallel",)),
        name="short_conv",
    )(p32, p32, p32, p32, p32, p32, p32, conv_w)


def _gla_kernel(*refs, reverse, nt, fuse_out):
    if fuse_out:
        (q_ref, k_ref, v_ref, z_ref, w2_ref, gb_ref, s0_ref, of_ref, g_ref, gain_ref, o_ref, sfin_ref, st_ref) = refs
    else:
        (q_ref, k_ref, v_ref, z_ref, w2_ref, gb_ref, s0_ref, o_ref, sfin_ref, st_ref) = refs
    i = pl.program_id(1)

    @pl.when(i == 0)
    def _():
        st_ref[...] = s0_ref[...]

    tile = q_ref.shape[0]
    c_len = min(GLA_C, tile)
    dk2 = PAIR_W
    dv2 = v_ref.shape[1] // (q_ref.shape[1] // dk2)
    dv = dv2 // 2
    u = jnp.dot(z_ref[...].astype(BF16), w2_ref[...], preferred_element_type=F32) + gb_ref[...]
    la = (jnp.minimum(u, 0.0) - jnp.log1p(jnp.exp(-jnp.abs(u)))) * (1.0 / GLA_TAU)

    r_i = lax.broadcasted_iota(jnp.int32, (c_len, c_len), 0)
    c_i = lax.broadcasted_iota(jnp.int32, (c_len, c_len), 1)
    tri = (r_i <= c_i) if reverse else (r_i >= c_i)
    tri_bf = jnp.where(tri, 1.0, 0.0).astype(BF16)
    r2 = lax.broadcasted_iota(jnp.int32, (2 * c_len, c_len), 0) % c_len
    c2 = lax.broadcasted_iota(jnp.int32, (2 * c_len, c_len), 1)
    tri2 = (r2 <= c2) if reverse else (r2 >= c2)
    bd_r = lax.broadcasted_iota(jnp.int32, (dv2, dk2), 0) // dv
    bd_c = lax.broadcasted_iota(jnp.int32, (dv2, dk2), 1) // HEAD_DIM
    block_diag = bd_r == bd_c
    last_row = 0 if reverse else c_len - 1
    mid_row = c_len // 2

    n_chunks = tile // c_len
    order = range(n_chunks - 1, -1, -1) if reverse else range(n_chunks)
    for c in order:
        rows = slice(c * c_len, (c + 1) * c_len)
        la_c = la[rows]
        la_hi = la_c.astype(BF16)
        la_lo = (la_c - la_hi.astype(F32)).astype(BF16)
        cum = (jnp.dot(tri_bf, la_hi, preferred_element_type=F32)
               + jnp.dot(tri_bf, la_lo, preferred_element_type=F32))
        last = cum[last_row:last_row + 1]
        cmid = cum[mid_row:mid_row + 1]
        qc = q_ref[rows, :] * ATTN_SCALE
        kc = k_ref[rows, :]
        vc = v_ref[rows, :].astype(BF16)
        q_in = (qc * jnp.exp(cum)).astype(BF16)
        q_t = qc * jnp.exp(cum - cmid)
        k_t = (kc * jnp.exp(cmid - cum)).astype(BF16)
        k_p = (kc * jnp.exp(last - cum)).astype(BF16)
        g = jnp.exp(last)
        for p in range(q_ref.shape[1] // dk2):
            ls = slice(p * dk2, (p + 1) * dk2)
            a = lax.dot_general(_stack_pair(q_t[:, ls]), k_t[:, ls], _NT, preferred_element_type=F32)
            a = jnp.where(tri2, a, 0.0).astype(BF16)
            o0 = jnp.dot(a[:c_len], vc[:, p * dv2:p * dv2 + dv], preferred_element_type=F32)
            o1 = jnp.dot(a[c_len:], vc[:, p * dv2 + dv:(p + 1) * dv2], preferred_element_type=F32)
            st = st_ref[p]
            o_int = lax.dot_general(q_in[:, ls], st.astype(BF16), _NT, preferred_element_type=F32)
            o_p = jnp.concatenate([o0, o1], axis=1) + o_int
            upd = lax.dot_general(vc[:, p * dv2:(p + 1) * dv2], k_p[:, ls], _TN, preferred_element_type=F32)
            st_ref[p] = g[:, ls] * st + jnp.where(block_diag, upd, 0.0)
            if not fuse_out:
                o_ref[rows, p * dv2:(p + 1) * dv2] = o_p
            else:
                tot = of_ref[rows, p * dv2:(p + 1) * dv2] + o_p
                for hh in range(2):
                    hs = slice(p * dv2 + hh * dv, p * dv2 + (hh + 1) * dv)
                    oh = tot[:, hh * dv:(hh + 1) * dv]
                    on = oh * lax.rsqrt(jnp.mean(oh * oh, axis=-1, keepdims=True) + LN_EPS) * gain_ref[...]
                    o_ref[rows, hs] = (on * _silu(g_ref[rows, hs])).astype(o_ref.dtype)

    @pl.when(i == nt - 1)
    def _():
        sfin_ref[...] = st_ref[...]


def _gla_call(p, w2pad, gbias, s0, fuse, *, cols, bsz, seq, reverse):
    nq = GLA_HEADS * HEAD_DIM
    nv = s0.shape[1] * s0.shape[2]
    tile = _pick_tile(seq, GLA_TILE)
    nt = seq // tile
    tix = (lambda i: nt - 1 - i) if reverse else (lambda i: i)
    blk = lambda w, col: pl.BlockSpec((tile, w), lambda b, i: (b * nt + tix(i), col // w))
    const2 = lambda b, i: (0, 0)
    st_spec = pl.BlockSpec((None,) + s0.shape[1:], lambda b, i: (b, 0, 0, 0))
    in_specs = [blk(nq, cols['gl_q']), blk(nq, cols['gl_k']), blk(nv, cols['gl_v']), blk(V7X_LANES, cols['gl_z']),
                pl.BlockSpec(w2pad.shape, const2), pl.BlockSpec(gbias.shape, const2), st_spec]
    args = [p, p, p, p, w2pad, gbias, s0]
    if fuse is not None:
        o_other, gain = fuse
        in_specs += [blk(nv, 0), blk(nv, cols['gl_g']), pl.BlockSpec(gain.shape, const2)]
        args += [o_other, p, gain]
    out_dtype = BF16 if fuse is not None else F32
    return pl.pallas_call(
        functools.partial(_gla_kernel, reverse=reverse, nt=nt, fuse_out=fuse is not None),
        grid=(bsz, nt),
        in_specs=in_specs,
        out_specs=[blk(nv, 0), st_spec],
        out_shape=[jax.ShapeDtypeStruct((bsz * seq, nv), out_dtype), jax.ShapeDtypeStruct(s0.shape, F32)],
        scratch_shapes=[pltpu.VMEM(s0.shape[1:], F32)],
        compiler_params=_cparams(("parallel", "arbitrary")),
        name="gla_bwd" if reverse else "gla_fwd",
    )(*args)


def _gla_all(p32, pc32, w2, gb, gain, *, bsz, seq, n_ctx, d):
    cols = _p32_cols(d)
    nq = GLA_HEADS * HEAD_DIM
    dv = d // 4 // GLA_HEADS
    w2pad = [jnp.zeros((V7X_LANES, nq), F32).at[GLA_RANK * k:GLA_RANK * (k + 1)].set(w2[k]).astype(BF16) for k in range(2)]
    gbias = [gb[k].reshape(1, nq) for k in range(2)]
    s0 = jnp.zeros((bsz, GLA_HEADS // 2, 2 * dv, PAIR_W), F32)
    gain2 = gain.reshape(1, dv)
    oc_f, sc_f = _gla_call(pc32, w2pad[0], gbias[0], s0, None, cols=cols, bsz=bsz, seq=n_ctx, reverse=False)
    y_ctx, sc_b = _gla_call(pc32, w2pad[1], gbias[1], s0, (oc_f, gain2), cols=cols, bsz=bsz, seq=n_ctx, reverse=True)
    o_f, _ = _gla_call(p32, w2pad[0], gbias[0], sc_f, None, cols=cols, bsz=bsz, seq=seq, reverse=False)
    y_lat, _ = _gla_call(p32, w2pad[1], gbias[1], sc_b, (o_f, gain2), cols=cols, bsz=bsz, seq=seq, reverse=True)
    return y_lat, y_ctx


def _swiglu(x, wg, wu, wd):
    g = jnp.dot(x, wg, preferred_element_type=F32)
    u = jnp.dot(x, wu, preferred_element_type=F32)
    a = (_silu(g) * u).astype(BF16)
    return jnp.dot(a, wd, preferred_element_type=F32)


def _first_argmax(vals, idx, sentinel):
    m = jnp.max(vals, axis=0, keepdims=True)
    first = jnp.min(jnp.where(vals == m, idx, sentinel), axis=0, keepdims=True)
    return m, first


def _router_kernel(h_ref, wr_ref, rb_ref, su_ref, eidx_ref, wts_ref, rank_ref, cnt_ref, run_ref):
    i = pl.program_id(0)

    @pl.when(i == 0)
    def _():
        run_ref[...] = jnp.zeros_like(run_ref)

    t = h_ref.shape[0]
    gsz = N_EXPERTS // N_GROUPS
    logits = lax.dot_general(wr_ref[...], h_ref[...], _NT, preferred_element_type=F32)
    scores = jax.nn.sigmoid(logits)
    sel = scores + rb_ref[...]
    sub = lax.broadcasted_iota(jnp.int32, (gsz, t), 0)
    gscore = []
    for g in range(N_GROUPS):
        blk = sel[g * gsz:(g + 1) * gsz]
        m1, a1 = _first_argmax(blk, sub, gsz)
        m2 = jnp.max(jnp.where(sub == a1, -jnp.inf, blk), axis=0, keepdims=True)
        gscore.append(m1 + m2)
    gcur = jnp.concatenate(gscore, axis=0)
    gid = lax.broadcasted_iota(jnp.int32, (N_GROUPS, t), 0)
    gkeep = jnp.zeros((N_GROUPS, t), F32)
    for _ in range(TOPK_GROUPS):
        _, a = _first_argmax(gcur, gid, N_GROUPS)
        hit = gid == a
        gkeep = jnp.where(hit, 1.0, gkeep)
        gcur = jnp.where(hit, -jnp.inf, gcur)
    cur = jnp.concatenate(
        [jnp.where(gkeep[g:g + 1] > 0.0, sel[g * gsz:(g + 1) * gsz], -jnp.inf) for g in range(N_GROUPS)], axis=0)
    eid = lax.broadcasted_iota(jnp.int32, (N_EXPERTS, t), 0)
    chosen = jnp.zeros((N_EXPERTS, t), F32)
    hits, picks, wraw = [], [], []
    for _ in range(TOP_K):
        _, a = _first_argmax(cur, eid, N_EXPERTS)
        hit = eid == a
        hits.append(hit)
        picks.append(a)
        wraw.append(jnp.sum(jnp.where(hit, scores, 0.0), axis=0, keepdims=True))
        chosen = jnp.where(hit, 1.0, chosen)
        cur = jnp.where(hit, -jnp.inf, cur)
    wsum = wraw[0]
    for w in wraw[1:]:
        wsum = wsum + w
    eidx_ref[...] = jnp.concatenate(picks, axis=0)
    wts_ref[...] = jnp.concatenate([w / wsum * ROUTED_SCALE for w in wraw], axis=0)
    before = jnp.dot(chosen.astype(BF16), su_ref[...], preferred_element_type=F32) + run_ref[...][:, 0:1]
    rank_ref[...] = jnp.concatenate(
        [jnp.sum(jnp.where(hit, before, 0.0), axis=0, keepdims=True) for hit in hits], axis=0).astype(jnp.int32)
    run_ref[...] = run_ref[...] + jnp.sum(chosen, axis=1, keepdims=True)
    cnt_ref[...] = run_ref[...]


def _router_call(h, w_router, router_bias):
    n, d = h.shape
    t = _pick_tile(n, ROUTER_TILE)
    wr_t = w_router.T.astype(BF16)
    strict_upper = jnp.triu(jnp.ones((t, t), F32), 1).astype(BF16)
    const2 = lambda i: (0, 0)
    tok = lambda i: (0, i)
    eidx, wts, rank, cnt = pl.pallas_call(
        _router_kernel,
        grid=(n // t,),
        in_specs=[pl.BlockSpec((t, d), lambda i: (i, 0)), pl.BlockSpec((N_EXPERTS, d), const2),
                  pl.BlockSpec((N_EXPERTS, 1), const2), pl.BlockSpec((t, t), const2)],
        out_specs=[pl.BlockSpec((TOP_K, t), tok), pl.BlockSpec((TOP_K, t), tok), pl.BlockSpec((TOP_K, t), tok),
                   pl.BlockSpec((N_EXPERTS, V7X_LANES), const2)],
        out_shape=[jax.ShapeDtypeStruct((TOP_K, n), jnp.int32), jax.ShapeDtypeStruct((TOP_K, n), F32),
                   jax.ShapeDtypeStruct((TOP_K, n), jnp.int32), jax.ShapeDtypeStruct((N_EXPERTS, V7X_LANES), F32)],
        scratch_shapes=[pltpu.VMEM((N_EXPERTS, V7X_LANES), F32)],
        compiler_params=_cparams(("arbitrary",)),
        name="router",
    )(h, wr_t, router_bias.reshape(N_EXPERTS, 1).astype(F32), strict_upper)
    return eidx, wts, rank, cnt[:, 0].astype(jnp.int32)


def _dispatch_kernel(zoff_ref, nused_ref, dest_ref, hp_ref, xs_hbm, zbuf, sem, zsem):
    i = pl.program_id(0)
    t = dest_ref.shape[1]
    n_blocks = xs_hbm.shape[0] // MOE_TB

    def clear_block(off):
        return pltpu.make_async_copy(zbuf, xs_hbm.at[pl.ds(pl.multiple_of(off, MOE_TB), MOE_TB)], zsem)

    @pl.when(i == 0)
    def _():
        zbuf[...] = jnp.zeros_like(zbuf)
        for e in range(N_EXPERTS):
            clear_block(zoff_ref[e]).start()
        for e in range(N_EXPERTS):
            clear_block(0).wait()

        def clear_tail(b, carry):
            cp = clear_block(b * MOE_TB)
            cp.start()
            cp.wait()
            return carry

        lax.fori_loop(nused_ref[0], n_blocks, clear_tail, 0)

    def row_copy(j, slot):
        return pltpu.make_async_copy(hp_ref.at[pl.ds(j, 1)], xs_hbm.at[pl.ds(slot, 1)], sem)

    def issue(j, carry):
        for k in range(TOP_K):
            row_copy(j, dest_ref[k, j]).start(priority=k % 2)
        return carry

    def drain(j, carry):
        for k in range(TOP_K):
            row_copy(0, 0).wait()
        return carry

    lax.fori_loop(0, t, issue, 0)
    lax.fori_loop(0, t, drain, 0)


def _dispatch_call(zero_off, n_used, dest, h_packed, n_rows):
    n, c = h_packed.shape
    t = _pick_tile(n, DISPATCH_TILE)
    grid_spec = pltpu.PrefetchScalarGridSpec(
        num_scalar_prefetch=2,
        grid=(n // t,),
        in_specs=[pl.BlockSpec((TOP_K, t), lambda i, z, nu: (0, i), memory_space=pltpu.SMEM),
                  pl.BlockSpec((t, c), lambda i, z, nu: (i, 0))],
        out_specs=pl.BlockSpec(memory_space=pl.ANY),
        scratch_shapes=[pltpu.VMEM((MOE_TB, c), jnp.uint32), pltpu.SemaphoreType.DMA, pltpu.SemaphoreType.DMA],
    )
    return pl.pallas_call(
        _dispatch_kernel,
        grid_spec=grid_spec,
        out_shape=jax.ShapeDtypeStruct((n_rows, c), jnp.uint32),
        compiler_params=_cparams(("arbitrary",)),
        name="moe_dispatch",
    )(zero_off, n_used, dest, h_packed)


def _expert_kernel(be_ref, nused_ref, x_ref, wg_ref, wu_ref, wd_ref, y_ref, wg_s, wu_s, wd_s):
    b = pl.program_id(0)
    prev = be_ref[jnp.maximum(b - 1, 0)]

    @pl.when((b == 0) | (be_ref[b] != prev))
    def _():
        wg_s[...] = wg_ref[...].astype(BF16)
        wu_s[...] = wu_ref[...].astype(BF16)
        wd_s[...] = wd_ref[...].astype(BF16)

    @pl.when(b < nused_ref[0])
    def _():
        y_ref[...] = _pack_halves(_swiglu(_unpack_halves(x_ref[...]), wg_s[...], wu_s[...], wd_s[...]))

    @pl.when(b >= nused_ref[0])
    def _():
        y_ref[...] = jnp.zeros_like(y_ref)


def _expert_call(block_e, n_used, x_sorted, wg, wu, wd, layer):
    n_rows, c = x_sorted.shape
    _, _, d, de = wg.shape
    n_blocks = n_rows // MOE_TB
    wsel = lambda b, be, nu: (layer, be[b], 0, 0)
    xsel = lambda b, be, nu: (jnp.minimum(b, nu[0] - 1), 0)
    grid_spec = pltpu.PrefetchScalarGridSpec(
        num_scalar_prefetch=2,
        grid=(n_blocks,),
        in_specs=[pl.BlockSpec((MOE_TB, c), xsel), pl.BlockSpec((None, None, d, de), wsel),
                  pl.BlockSpec((None, None, d, de), wsel), pl.BlockSpec((None, None, de, d), wsel)],
        out_specs=pl.BlockSpec((MOE_TB, c), lambda b, be, nu: (b, 0)),
        scratch_shapes=[pltpu.VMEM((d, de), BF16), pltpu.VMEM((d, de), BF16), pltpu.VMEM((de, d), BF16)],
    )
    return pl.pallas_call(
        _expert_kernel,
        grid_spec=grid_spec,
        out_shape=jax.ShapeDtypeStruct((n_rows, c), jnp.uint32),
        compiler_params=_cparams(("arbitrary",)),
        name="experts",
    )(block_e, n_used, x_sorted, wg, wu, wd)


def _combine_kernel(dest_ref, w_ref, h_ref, x_ref, g2_ref, wsg_ref, wsu_ref, wsd_ref, lng_ref, lnb_ref, y_hbm,
                    o_ref, ybuf, sem, *, alpha):
    t = h_ref.shape[0]

    def row_copy(slot, k, j):
        return pltpu.make_async_copy(y_hbm.at[pl.ds(slot, 1)], ybuf.at[k, pl.ds(j, 1)], sem)

    def issue(j, carry):
        for k in range(TOP_K):
            row_copy(dest_ref[k, j], k, j).start(priority=k % 2)
        return carry

    def drain(j, carry):
        for k in range(TOP_K):
            row_copy(0, k, j).wait()
        return carry

    lax.fori_loop(0, t, issue, 0)
    acc = _swiglu(h_ref[...], wsg_ref[...], wsu_ref[...], wsd_ref[...])
    lax.fori_loop(0, t, drain, 0)
    w = w_ref[...]
    half = acc.shape[1] // 2
    lo, hi = acc[:, :half], acc[:, half:]
    for k in range(TOP_K):
        yk = ybuf[k]
        wk = w[:, k:k + 1]
        lo = lo + wk * pltpu.bitcast(lax.shift_left(yk, jnp.uint32(16)), F32)
        hi = hi + wk * pltpu.bitcast(yk & jnp.uint32(0xFFFF0000), F32)
    acc = jnp.concatenate([lo, hi], axis=1)
    r = alpha * x_ref[...] + g2_ref[0] * acc
    o_ref[...] = _ln_rows(r) * lng_ref[...] + lnb_ref[...]


def _combine_call(dest, wts_t, h, x2d, gate2, wsg, wsu, wsd, ln_g, ln_b, y_sorted, *, row0, rows_per_group, alpha):
    m, d = x2d.shape
    de = wsg.shape[1]
    t = _pick_tile(min(rows_per_group, m), COMBINE_TILE)
    assert row0 % t == 0 and rows_per_group % t == 0
    t0 = row0 // t
    const2 = lambda i: (0, 0)
    once = pl.Buffered(1)
    return pl.pallas_call(
        functools.partial(_combine_kernel, alpha=alpha),
        grid=(m // t,),
        in_specs=[pl.BlockSpec((TOP_K, t), lambda i: (0, t0 + i), memory_space=pltpu.SMEM),
                  pl.BlockSpec((t, TOP_K), lambda i: (t0 + i, 0)),
                  pl.BlockSpec((t, d), lambda i: (t0 + i, 0)),
                  pl.BlockSpec((t, d), lambda i: (i, 0)),
                  pl.BlockSpec((1, 1, d), lambda i: ((i * t) // rows_per_group, 0, 0)),
                  pl.BlockSpec((d, de), const2, pipeline_mode=once),
                  pl.BlockSpec((d, de), const2, pipeline_mode=once),
                  pl.BlockSpec((de, d), const2, pipeline_mode=once),
                  pl.BlockSpec((1, d), const2), pl.BlockSpec((1, d), const2),
                  pl.BlockSpec(memory_space=pl.ANY)],
        out_specs=pl.BlockSpec((t, d), lambda i: (i, 0)),
        out_shape=jax.ShapeDtypeStruct((m, d), F32),
        scratch_shapes=[pltpu.VMEM((TOP_K, t, d // 2), jnp.uint32), pltpu.SemaphoreType.DMA],
        compiler_params=_cparams(("arbitrary",)),
        name="moe_combine",
    )(dest, wts_t, h, x2d, gate2, wsg, wsu, wsd, ln_g.reshape(1, d), ln_b.reshape(1, d), y_sorted)


def _moe_routed(h, h_packed, w_router, router_bias, w_eg, w_eu, w_ed, layer):
    n = h.shape[0]
    eidx, wts, rank, counts = _router_call(h, w_router, router_bias)
    padded = (counts + MOE_TB - 1) // MOE_TB * MOE_TB
    pad_end = jnp.cumsum(padded)
    pad_start = pad_end - padded
    n_blocks = (n * TOP_K + N_EXPERTS * (MOE_TB - 1) + MOE_TB - 1) // MOE_TB
    block_first = jnp.arange(n_blocks, dtype=jnp.int32) * MOE_TB
    block_e = jnp.minimum(jnp.sum((pad_end[None, :] <= block_first[:, None]).astype(jnp.int32), axis=1), N_EXPERTS - 1)
    n_used = (pad_end[-1:] // MOE_TB).astype(jnp.int32)
    zero_off = jnp.maximum(pad_end - MOE_TB, 0).astype(jnp.int32)
    expert_ids = jnp.arange(N_EXPERTS, dtype=jnp.int32)
    first_row = jnp.sum(jnp.where(eidx[:, :, None] == expert_ids, pad_start.astype(jnp.int32), 0), axis=-1)
    dest = first_row + rank
    x_sorted = _dispatch_call(zero_off, n_used, dest, h_packed, n_blocks * MOE_TB)
    y_sorted = _expert_call(block_e, n_used, x_sorted, w_eg, w_eu, w_ed, layer)
    return dest, wts.T, y_sorted


def _pack_w_in(w, d):
    gw = d // 4
    cuts = np.cumsum([gw, gw, gw, gw, gw, gw, gw // 2, gw // 2, gw, gw, 2 * GLA_RANK, gw, gw // 4, gw // 4])[:-1].tolist()
    (na_q, na_k, na_v, sc_u, sc_b, sc_c, gl_q, gl_k, gl_v, gl_g, gl_z, sw_q, sw_k, sw_v) = jnp.split(w, cuts, axis=1)
    zpad = jnp.zeros((d, V7X_LANES - 2 * GLA_RANK), w.dtype)
    cols = [na_q, na_k, na_v, sc_u, sc_b, sc_c, gl_v, gl_g, sw_q, gl_q, gl_k, sw_k, sw_v, gl_z, zpad]
    packed = jnp.concatenate(cols, axis=1)
    pad = (-packed.shape[1]) % PROJ_TN
    packed = jnp.pad(packed, ((0, 0), (0, pad)))
    return packed.astype(BF16)


def _p32_cols(d):
    gw = d // 4
    cols, c = {}, 0
    for name, wdt in (('sc_u', gw), ('sc_b', gw), ('sc_c', gw), ('gl_v', gw), ('gl_g', gw), ('sw_q', gw),
                      ('gl_q', gw // 2), ('gl_k', gw // 2), ('sw_k', gw // 4), ('sw_v', gw // 4), ('gl_z', V7X_LANES)):
        cols[name] = c
        c += wdt
    return cols


def _token_mixers(p16, p32, pc16, pc32, rpb, conv_w, gla_w2, gla_b, gla_norm_g, sink, with_ctx_out, *,
                  bsz, seq, n_ctx, d):
    cols = _p32_cols(d)
    kh = min(NA_KH, seq // GRID_W)
    y_na = _na_call(p16, pc16, _na_bias_table(rpb, kh), bsz=bsz, seq=seq, n_ctx=n_ctx)
    conv_cols = dict(col_u=cols['sc_u'], col_b=cols['sc_b'], col_c=cols['sc_c'])
    y_sc = _conv_call(p32, conv_w, rows_per_seq=seq, **conv_cols)
    y_gl, yc_gl = _gla_all(p32, pc32, gla_w2, gla_b, gla_norm_g, bsz=bsz, seq=seq, n_ctx=n_ctx, d=d)
    tables = _rope_tables(seq)
    swa_cols = dict(col_q=cols['sw_q'], col_k=cols['sw_k'], col_v=cols['sw_v'])
    qr, kd, vd = _swa_prep_call(p32, tables, rows_per_seq=seq, rope=True, **swa_cols)
    ctx_tables = tuple(t[:n_ctx] for t in tables)
    qx, kxd, vxd = _swa_prep_call(pc32, ctx_tables, rows_per_seq=n_ctx, rope=False, **swa_cols)
    y_sw = _swa_call(sink, qr, kd, vd, kxd, vxd, bsz=bsz, seq=seq, n_ctx=n_ctx)
    y_lat = (y_na, y_sc, y_gl, y_sw)
    if not with_ctx_out:
        return y_lat, None
    yc_na, yc_sw = _ctx_attn_call(sink, pc16, qx, kxd, vxd, bsz=bsz, n_ctx=n_ctx)
    yc_sc = _conv_call(pc32, conv_w, rows_per_seq=n_ctx, **conv_cols)
    return y_lat, (yc_na, yc_sc, yc_gl, yc_sw)


def kernel(x, c, ctx, c_ctx, w_ada, b_ada, w_in, na_rpb, conv_w, gla_w2, gla_b, gla_norm_g, swa_sink, w_out,
           ln1_g, ln1_b, w_router, router_bias, w_exp_gate, w_exp_up, w_exp_down, w_sh_gate, w_sh_up, w_sh_down,
           ln2_g, ln2_b):
    bsz, seq, d = x.shape
    n_ctx = ctx.shape[1]
    nc = bsz * n_ctx
    depth = w_in.shape[0]
    alpha = (2 * depth) ** 0.25
    n16 = 3 * (d // 4)
    x2 = x.reshape(bsz * seq, d)
    hc2 = ctx.reshape(nc, d)
    c_rows = jnp.zeros((8, d), F32).at[:bsz].set(c).at[bsz].set(c_ctx)
    for layer in range(depth):
        last = layer == depth - 1
        mod = _ada_call(c_rows, w_ada, b_ada[layer], layer)
        sh1, sc1, g1, sh2, sc2, g2 = [t[:bsz, None, :] for t in jnp.split(mod, 6, axis=-1)]
        sh1c, sc1c, g1c, sh2c, sc2c, g2c = [t[bsz:bsz + 1, None, :] for t in jnp.split(mod, 6, axis=-1)]
        w_p = _pack_w_in(w_in[layer], d)
        p16, p32 = _proj_call(x2, sh1, sc1, w_p, rows_per_group=seq, n16=n16)
        pc16, pc32 = _proj_call(hc2, sh1c, sc1c, w_p, rows_per_group=nc, n16=n16)
        y_lat, y_ctx = _token_mixers(p16, p32, pc16, pc32, na_rpb[layer], conv_w[layer], gla_w2[layer], gla_b[layer],
                                     gla_norm_g[layer], swa_sink[layer], not last, bsz=bsz, seq=seq, n_ctx=n_ctx, d=d)
        w_o = w_out[layer].astype(BF16)
        x2, h_lat, hp_lat = _outproj_call(y_lat, x2, g1, w_o, ln1_g[layer], ln1_b[layer], sh2, sc2,
                                          rows_per_group=seq, alpha=alpha)
        shared_w = (w_sh_gate[layer].astype(BF16), w_sh_up[layer].astype(BF16), w_sh_down[layer].astype(BF16))
        route_w = (w_router[layer], router_bias[layer], w_exp_gate, w_exp_up, w_exp_down, layer)
        if last:
            dest, wts_t, y_sorted = _moe_routed(h_lat, hp_lat, *route_w)
            x2 = _combine_call(dest, wts_t, h_lat, x2, g2, *shared_w, ln2_g[layer], ln2_b[layer], y_sorted,
                               row0=0, rows_per_group=seq, alpha=alpha)
        else:
            hc2, h_ctx, hp_ctx = _outproj_call(y_ctx, hc2, g1c, w_o, ln1_g[layer], ln1_b[layer], sh2c, sc2c,
                                               rows_per_group=nc, alpha=alpha)
            h_all = jnp.concatenate([h_ctx, h_lat], axis=0)
            hp_all = jnp.concatenate([hp_ctx, hp_lat], axis=0)
            dest, wts_t, y_sorted = _moe_routed(h_all, hp_all, *route_w)
            hc2 = _combine_call(dest, wts_t, h_all, hc2, g2c, *shared_w, ln2_g[layer], ln2_b[layer], y_sorted,
                                row0=0, rows_per_group=nc, alpha=alpha)
            x2 = _combine_call(dest, wts_t, h_all, x2, g2, *shared_w, ln2_g[layer], ln2_b[layer], y_sorted,
                               row0=nc, rows_per_group=seq, alpha=alpha)
    return x2.reshape(bsz, seq, d)
```

```python
import functools

import jax
import jax.numpy as jnp
import numpy as np
from jax import lax
from jax.experimental import pallas as pl
from jax.experimental.pallas import tpu as pltpu

GRID_W = 64
HEAD_DIM = 64
NA_KH = 8
NA_KW = 16
SC_KSIZE = 3
GLA_HEADS = 4
GLA_RANK = 16
GLA_TAU = 16.0
SWA_KV_HEADS = 2
SWA_BLOCK = 128
ROPE_BASE = 10000.0
N_EXPERTS = 64
TOP_K = 8
N_GROUPS = 8
TOPK_GROUPS = 4
ROUTED_SCALE = 2.5
LN_EPS = 1e-6

V7X_LANES = 128
V7X_VMEM_LIMIT_BYTES = 48 * 1024 * 1024

PROJ_TM = 1024
PROJ_TN = 512
OUT_TM = 512
MOE_TB = 512
ROUTER_TILE = 512
DISPATCH_TILE = 256
COMBINE_TILE = 128
CONV_HALO = 8
GLA_C = 64
GLA_TILE = 512
ADA_TN = 2048

F32 = jnp.float32
BF16 = jnp.bfloat16
PAIR_W = 2 * HEAD_DIM
ATTN_SCALE = HEAD_DIM ** -0.5
_NT = (((1,), (1,)), ((), ()))
_TN = (((0,), (0,)), ((), ()))


def _cparams(sem):
    return pltpu.CompilerParams(dimension_semantics=sem, vmem_limit_bytes=V7X_VMEM_LIMIT_BYTES)


def _pick_tile(m, preferred):
    t = preferred
    while t > 8 and m % t:
        t //= 2
    assert m % t == 0, (m, preferred)
    return t


def _ln_rows(x):
    mu = jnp.mean(x, axis=-1, keepdims=True)
    xc = x - mu
    var = jnp.mean(xc * xc, axis=-1, keepdims=True)
    return xc * lax.rsqrt(var + LN_EPS)


def _silu(x):
    return x * jax.nn.sigmoid(x)


def _ada_kernel(c_ref, w_ref, b_ref, o_ref):
    a = _silu(c_ref[...]).astype(BF16)
    o_ref[...] = jnp.dot(a, w_ref[...].astype(BF16), preferred_element_type=F32) + b_ref[...]


def _ada_call(c_rows, w_ada, b_ada, layer):
    r, d = c_rows.shape
    n = w_ada.shape[2]
    tn = _pick_tile(n, ADA_TN)
    return pl.pallas_call(
        _ada_kernel,
        grid=(n // tn,),
        in_specs=[pl.BlockSpec((r, d), lambda j: (0, 0)), pl.BlockSpec((None, d, tn), lambda j: (layer, 0, j)),
                  pl.BlockSpec((1, tn), lambda j: (0, j))],
        out_specs=pl.BlockSpec((r, tn), lambda j: (0, j)),
        out_shape=jax.ShapeDtypeStruct((r, n), F32),
        compiler_params=_cparams(("parallel",)),
        name="ada_mod",
    )(c_rows, w_ada, b_ada.reshape(1, n))


def _proj_kernel(x_ref, sh_ref, sc_ref, w_ref, o16_ref, o32_ref, xn_ref, *, nb16):
    j = pl.program_id(1)

    @pl.when(j == 0)
    def _():
        y = _ln_rows(x_ref[...]) * (1.0 + sc_ref[0]) + sh_ref[0]
        xn_ref[...] = y.astype(BF16)

    acc = jnp.dot(xn_ref[...], w_ref[...], preferred_element_type=F32)

    @pl.when(j < nb16)
    def _():
        o16_ref[...] = acc.astype(BF16)

    @pl.when(j >= nb16)
    def _():
        o32_ref[...] = acc


def _proj_call(x2d, shift, scale, w_packed, *, rows_per_group, n16):
    m, d = x2d.shape
    ntot = w_packed.shape[1]
    tm = min(PROJ_TM, rows_per_group)
    assert m % tm == 0 and rows_per_group % tm == 0 and ntot % PROJ_TN == 0 and n16 % PROJ_TN == 0
    nb16 = n16 // PROJ_TN
    nb = ntot // PROJ_TN
    grp = lambda i, j: ((i * tm) // rows_per_group, 0, 0)
    return pl.pallas_call(
        functools.partial(_proj_kernel, nb16=nb16),
        grid=(m // tm, nb),
        in_specs=[
            pl.BlockSpec((tm, d), lambda i, j: (i, 0)),
            pl.BlockSpec((1, 1, d), grp),
            pl.BlockSpec((1, 1, d), grp),
            pl.BlockSpec((d, PROJ_TN), lambda i, j: (0, j)),
        ],
        out_specs=[
            pl.BlockSpec((tm, PROJ_TN), lambda i, j: (i, jnp.minimum(j, nb16 - 1))),
            pl.BlockSpec((tm, PROJ_TN), lambda i, j: (i, jnp.maximum(j - nb16, 0))),
        ],
        out_shape=[jax.ShapeDtypeStruct((m, n16), BF16), jax.ShapeDtypeStruct((m, ntot - n16), F32)],
        scratch_shapes=[pltpu.VMEM((tm, d), BF16)],
        compiler_params=_cparams(("parallel", "arbitrary")),
        name="proj",
    )(x2d, shift, scale, w_packed)


def _pack_halves(h):
    c = h.shape[1] // 2
    lo = pltpu.bitcast(h[:, :c].astype(BF16).astype(F32), jnp.uint32)
    hi = pltpu.bitcast(h[:, c:].astype(BF16).astype(F32), jnp.uint32)
    return lax.shift_right_logical(lo, jnp.uint32(16)) | (hi & jnp.uint32(0xFFFF0000))


def _unpack_halves(w):
    lo = pltpu.bitcast(lax.shift_left(w, jnp.uint32(16)), F32)
    hi = pltpu.bitcast(w & jnp.uint32(0xFFFF0000), F32)
    return jnp.concatenate([lo, hi], axis=1).astype(BF16)


def _outproj_kernel(y0_ref, y1_ref, y2_ref, y3_ref, x_ref, g1_ref, w_ref, lng_ref, lnb_ref,
                    sh2_ref, sc2_ref, xo_ref, h_ref, hp_ref, *, alpha):
    gw = y0_ref.shape[1]
    acc = jnp.dot(y0_ref[...], w_ref[0:gw, :], preferred_element_type=F32)
    acc += jnp.dot(y1_ref[...], w_ref[gw:2 * gw, :], preferred_element_type=F32)
    acc += jnp.dot(y2_ref[...], w_ref[2 * gw:3 * gw, :], preferred_element_type=F32)
    acc += jnp.dot(y3_ref[...], w_ref[3 * gw:4 * gw, :], preferred_element_type=F32)
    r = alpha * x_ref[...] + g1_ref[0] * acc
    xn = _ln_rows(r) * lng_ref[...] + lnb_ref[...]
    xo_ref[...] = xn
    h = _ln_rows(xn) * (1.0 + sc2_ref[0]) + sh2_ref[0]
    h_ref[...] = h.astype(BF16)
    hp_ref[...] = _pack_halves(h)


def _outproj_call(ys, x2d, gate1, w_out_bf16, ln_g, ln_b, shift2, scale2, *, rows_per_group, alpha):
    m, d = x2d.shape
    gw = d // 4
    tm = min(OUT_TM, rows_per_group)
    assert m % tm == 0 and rows_per_group % tm == 0
    grp = lambda i: ((i * tm) // rows_per_group, 0, 0)
    row = lambda i: (i, 0)
    const2 = lambda i: (0, 0)
    return pl.pallas_call(
        functools.partial(_outproj_kernel, alpha=alpha),
        grid=(m // tm,),
        in_specs=[pl.BlockSpec((tm, gw), row)] * 4 + [
            pl.BlockSpec((tm, d), row),
            pl.BlockSpec((1, 1, d), grp),
            pl.BlockSpec((d, d), const2, pipeline_mode=pl.Buffered(1)),
            pl.BlockSpec((1, d), const2),
            pl.BlockSpec((1, d), const2),
            pl.BlockSpec((1, 1, d), grp),
            pl.BlockSpec((1, 1, d), grp),
        ],
        out_specs=[pl.BlockSpec((tm, d), row), pl.BlockSpec((tm, d), row), pl.BlockSpec((tm, d // 2), row)],
        out_shape=[jax.ShapeDtypeStruct((m, d), F32), jax.ShapeDtypeStruct((m, d), BF16),
                   jax.ShapeDtypeStruct((m, d // 2), jnp.uint32)],
        compiler_params=_cparams(("parallel",)),
        name="outproj",
    )(*ys, x2d, gate1, w_out_bf16, ln_g.reshape(1, d), ln_b.reshape(1, d), shift2, scale2)


def _stack_pair(q2):
    lane = lax.broadcasted_iota(jnp.int32, q2.shape, 1)
    lo = jnp.where(lane < HEAD_DIM, q2, 0.0)
    hi = jnp.where(lane >= HEAD_DIM, q2, 0.0)
    return jnp.concatenate([lo, hi], axis=0).astype(BF16)


def _unstack_pair(o):
    n = o.shape[0] // 2
    lane = lax.broadcasted_iota(jnp.int32, (n, o.shape[1]), 1)
    return jnp.where(lane < HEAD_DIM, o[:n], o[n:])


def _pair_softmax_av(qs, ks, vs, biases, masks, sink_col):
    ss = []
    for k, bia, msk in zip(ks, biases, masks):
        s = lax.dot_general(qs, k, _NT, preferred_element_type=F32)
        if bia is not None:
            s = s + bia
        if msk is not None:
            s = jnp.where(msk, s, -jnp.inf)
        ss.append(s)
    m = jnp.max(ss[0], axis=-1, keepdims=True)
    for s in ss[1:]:
        m = jnp.maximum(m, jnp.max(s, axis=-1, keepdims=True))
    if sink_col is not None:
        m = jnp.maximum(m, sink_col)
        l = jnp.exp(sink_col - m)
    else:
        l = jnp.zeros_like(m)
    o = None
    for s, v in zip(ss, vs):
        e = jnp.exp(s - m)
        l = l + jnp.sum(e, axis=-1, keepdims=True)
        pv = jnp.dot(e.astype(BF16), v, preferred_element_type=F32)
        o = pv if o is None else o + pv
    return o / l


def _na_kernel(q_ref, k_ref, v_ref, kc_ref, vc_ref, bias_ref, o_ref, *, rows, kh):
    r = pl.program_id(1)
    rs = jnp.clip(r - kh // 2, 0, rows - kh)
    start = pl.multiple_of(rs * GRID_W, GRID_W)
    nwin = kh * GRID_W
    for p in range(q_ref.shape[1] // PAIR_W):
        sl = slice(p * PAIR_W, (p + 1) * PAIR_W)
        qs = _stack_pair(q_ref[:, sl].astype(F32) * ATTN_SCALE)
        kw = k_ref[pl.ds(start, nwin), sl]
        vw = v_ref[pl.ds(start, nwin), sl]
        o = _pair_softmax_av(qs, [kw, kc_ref[:, sl]], [vw, vc_ref[:, sl]], [bias_ref[p], None], [None, None], None)
        o_ref[:, sl] = _unstack_pair(o).astype(o_ref.dtype)


def _na_bias_table(rpb, kh):
    nh = rpb.shape[0]
    c = jnp.arange(GRID_W)
    cstart = jnp.clip(c - NA_KW // 2, 0, GRID_W - NA_KW)
    valid = (c[None, :] >= cstart[:, None]) & (c[None, :] < cstart[:, None] + NA_KW)
    coff = jnp.clip(c[None, :] - c[:, None], 1 - NA_KW, NA_KW - 1) + NA_KW - 1
    roff = jnp.arange(kh)[None, :] - jnp.arange(kh)[:, None] + NA_KH - 1
    pick_r = (roff[:, :, None] == jnp.arange(rpb.shape[1])).astype(F32)
    pick_c = (coff[:, :, None] == jnp.arange(rpb.shape[2])).astype(F32)
    bias = jnp.einsum('hab,dia,ckb->hdick', rpb.astype(F32), pick_r, pick_c, precision=lax.Precision.HIGHEST)
    bias = jnp.where(valid[None, None, None], bias, -jnp.inf)
    bias = bias.transpose(1, 0, 3, 2, 4)
    return bias.reshape(kh, nh // 2, 2 * GRID_W, kh * GRID_W)


def _na_call(p16, pc16, bias_tab, *, bsz, seq, n_ctx):
    gw = p16.shape[1] // 3
    rows = seq // GRID_W
    kh = bias_tab.shape[0]
    delta = lambda b, r: (r - jnp.clip(r - kh // 2, 0, rows - kh), 0, 0, 0)
    once = pl.Buffered(1)
    return pl.pallas_call(
        functools.partial(_na_kernel, rows=rows, kh=kh),
        grid=(bsz, rows),
        in_specs=[
            pl.BlockSpec((GRID_W, gw), lambda b, r: (b * rows + r, 0)),
            pl.BlockSpec((seq, gw), lambda b, r: (b, 1), pipeline_mode=once),
            pl.BlockSpec((seq, gw), lambda b, r: (b, 2), pipeline_mode=once),
            pl.BlockSpec((n_ctx, gw), lambda b, r: (b, 1)),
            pl.BlockSpec((n_ctx, gw), lambda b, r: (b, 2)),
            pl.BlockSpec((None,) + bias_tab.shape[1:], delta),
        ],
        out_specs=pl.BlockSpec((GRID_W, gw), lambda b, r: (b * rows + r, 0)),
        out_shape=jax.ShapeDtypeStruct((bsz * seq, gw), BF16),
        compiler_params=_cparams(("parallel", "arbitrary")),
        name="na_attn",
    )(p16, p16, p16, pc16, pc16, bias_tab)


def _dup_heads(t):
    lane = lax.broadcasted_iota(jnp.int32, t.shape, 1)
    sw = pltpu.roll(t, HEAD_DIM, 1)
    return jnp.concatenate([jnp.where(lane < HEAD_DIM, t, sw), jnp.where(lane < HEAD_DIM, sw, t)], axis=1)


def _rope(t, cos, sa, sb):
    q = HEAD_DIM // 4
    return t * cos + pltpu.roll(t, q, 1) * sa + pltpu.roll(t, V7X_LANES - q, 1) * sb


def _swa_prep_kernel(q_ref, k_ref, v_ref, cos_ref, sa_ref, sb_ref, qo_ref, ko_ref, vo_ref, *, rope):
    k = k_ref[...]
    if rope:
        cos, sa, sb = cos_ref[...], sa_ref[...], sb_ref[...]
        k = _rope(k, cos, sa, sb)
    for p in range(q_ref.shape[1] // PAIR_W):
        sl = slice(p * PAIR_W, (p + 1) * PAIR_W)
        q = q_ref[:, sl]
        if rope:
            q = _rope(q, cos, sa, sb)
        qo_ref[:, sl] = (q * ATTN_SCALE).astype(BF16)
    ko_ref[...] = _dup_heads(k).astype(BF16)
    vo_ref[...] = _dup_heads(v_ref[...]).astype(BF16)


def _rope_tables(seq):
    t = jnp.arange(seq)
    row = (t // GRID_W).astype(F32)
    col = (t % GRID_W).astype(F32)
    quarter = HEAD_DIM // 4
    inv = ROPE_BASE ** (-2.0 * jnp.arange(quarter, dtype=F32) / (HEAD_DIM // 2))
    ang_r = row[:, None] * inv[None, :]
    ang_c = col[:, None] * inv[None, :]
    zero = jnp.zeros_like(ang_r)
    cos_h = jnp.concatenate([jnp.cos(ang_r)] * 2 + [jnp.cos(ang_c)] * 2, axis=1)
    sa_h = jnp.concatenate([zero, jnp.sin(ang_r), zero, jnp.sin(ang_c)], axis=1)
    sb_h = jnp.concatenate([-jnp.sin(ang_r), zero, -jnp.sin(ang_c), zero], axis=1)
    two = lambda a: jnp.concatenate([a, a], axis=1)
    return two(cos_h), two(sa_h), two(sb_h)


def _swa_prep_call(p32, tables, *, col_q, col_k, col_v, rows_per_seq, rope):
    m = p32.shape[0]
    gw = 4 * PAIR_W
    tm = _pick_tile(rows_per_seq, 512)
    nseq_tiles = rows_per_seq // tm
    tab = lambda i: (i % nseq_tiles, 0)
    return pl.pallas_call(
        functools.partial(_swa_prep_kernel, rope=rope),
        grid=(m // tm,),
        in_specs=[
            pl.BlockSpec((tm, gw), lambda i: (i, col_q // gw)),
            pl.BlockSpec((tm, PAIR_W), lambda i: (i, col_k // PAIR_W)),
            pl.BlockSpec((tm, PAIR_W), lambda i: (i, col_v // PAIR_W)),
            pl.BlockSpec((tm, PAIR_W), tab),
            pl.BlockSpec((tm, PAIR_W), tab),
            pl.BlockSpec((tm, PAIR_W), tab),
        ],
        out_specs=[pl.BlockSpec((tm, gw), lambda i: (i, 0)), pl.BlockSpec((tm, 2 * PAIR_W), lambda i: (i, 0)),
                   pl.BlockSpec((tm, 2 * PAIR_W), lambda i: (i, 0))],
        out_shape=[jax.ShapeDtypeStruct((m, gw), BF16), jax.ShapeDtypeStruct((m, 2 * PAIR_W), BF16),
                   jax.ShapeDtypeStruct((m, 2 * PAIR_W), BF16)],
        compiler_params=_cparams(("parallel",)),
        name="swa_prep",
    )(p32, p32, p32, *tables)


def _sink_col(sink_ref, p, n):
    row = lax.broadcasted_iota(jnp.int32, (2 * n, 1), 0)
    return jnp.where(row < n, sink_ref[2 * p], sink_ref[2 * p + 1])


def _swa_kernel(sink_ref, q_ref, kp_ref, kc_ref, kn_ref, vp_ref, vc_ref, vn_ref, kx_ref, vx_ref, o_ref, *, nblk):
    n = pl.program_id(1)
    blk = q_ref.shape[0]
    qi = lax.broadcasted_iota(jnp.int32, (2 * blk, blk), 0) % blk
    kj = lax.broadcasted_iota(jnp.int32, (2 * blk, blk), 1)
    m_prev = kj >= qi + jnp.where(n > 0, 0, blk)
    m_next = kj <= qi - jnp.where(n < nblk - 1, 0, blk)
    npairs = q_ref.shape[1] // PAIR_W
    for p in range(npairs):
        sl = slice(p * PAIR_W, (p + 1) * PAIR_W)
        g = p // (npairs // SWA_KV_HEADS)
        gs = slice(g * PAIR_W, (g + 1) * PAIR_W)
        qs = _stack_pair(q_ref[:, sl].astype(F32))
        o = _pair_softmax_av(
            qs, [kp_ref[:, gs], kc_ref[:, gs], kn_ref[:, gs], kx_ref[:, gs]],
            [vp_ref[:, gs], vc_ref[:, gs], vn_ref[:, gs], vx_ref[:, gs]],
            [None] * 4, [m_prev, None, m_next, None], _sink_col(sink_ref, p, blk))
        o_ref[:, sl] = _unstack_pair(o).astype(o_ref.dtype)


def _swa_call(sink, qr, kd, vd, kxd, vxd, *, bsz, seq, n_ctx):
    gw = qr.shape[1]
    kw = kd.shape[1]
    nblk = seq // SWA_BLOCK
    cur = lambda b, n: (b * nblk + n, 0)
    prev = lambda b, n: (b * nblk + jnp.maximum(n - 1, 0), 0)
    nxt = lambda b, n: (b * nblk + jnp.minimum(n + 1, nblk - 1), 0)
    cx = lambda b, n: (b, 0)
    return pl.pallas_call(
        functools.partial(_swa_kernel, nblk=nblk),
        grid=(bsz, nblk),
        in_specs=[pl.BlockSpec(memory_space=pltpu.SMEM),
                  pl.BlockSpec((SWA_BLOCK, gw), cur),
                  pl.BlockSpec((SWA_BLOCK, kw), prev), pl.BlockSpec((SWA_BLOCK, kw), cur), pl.BlockSpec((SWA_BLOCK, kw), nxt),
                  pl.BlockSpec((SWA_BLOCK, kw), prev), pl.BlockSpec((SWA_BLOCK, kw), cur), pl.BlockSpec((SWA_BLOCK, kw), nxt),
                  pl.BlockSpec((n_ctx, kw), cx), pl.BlockSpec((n_ctx, kw), cx)],
        out_specs=pl.BlockSpec((SWA_BLOCK, gw), cur),
        out_shape=jax.ShapeDtypeStruct((bsz * seq, gw), BF16),
        compiler_params=_cparams(("parallel", "arbitrary")),
        name="swa_attn",
    )(sink, qr, kd, kd, kd, vd, vd, vd, kxd, vxd)


def _ctx_attn_kernel(sink_ref, qa_ref, ka_ref, va_ref, qd_ref, kd_ref, vd_ref, oa_ref, od_ref):
    n = qa_ref.shape[0]
    npairs = qa_ref.shape[1] // PAIR_W
    for p in range(npairs):
        sl = slice(p * PAIR_W, (p + 1) * PAIR_W)
        qs = _stack_pair(qa_ref[:, sl].astype(F32) * ATTN_SCALE)
        o = _pair_softmax_av(qs, [ka_ref[:, sl]], [va_ref[:, sl]], [None], [None], None)
        oa_ref[:, sl] = _unstack_pair(o).astype(oa_ref.dtype)
    for p in range(npairs):
        sl = slice(p * PAIR_W, (p + 1) * PAIR_W)
        g = p // (npairs // SWA_KV_HEADS)
        gs = slice(g * PAIR_W, (g + 1) * PAIR_W)
        qs = _stack_pair(qd_ref[:, sl].astype(F32))
        o = _pair_softmax_av(qs, [kd_ref[:, gs]], [vd_ref[:, gs]], [None], [None], _sink_col(sink_ref, p, n))
        od_ref[:, sl] = _unstack_pair(o).astype(od_ref.dtype)


def _ctx_attn_call(sink, pc16, qx, kxd, vxd, *, bsz, n_ctx):
    gw = qx.shape[1]
    kw = kxd.shape[1]
    return pl.pallas_call(
        _ctx_attn_kernel,
        grid=(bsz,),
        in_specs=[pl.BlockSpec(memory_space=pltpu.SMEM),
                  pl.BlockSpec((n_ctx, gw), lambda b: (b, 0)), pl.BlockSpec((n_ctx, gw), lambda b: (b, 1)),
                  pl.BlockSpec((n_ctx, gw), lambda b: (b, 2)), pl.BlockSpec((n_ctx, gw), lambda b: (b, 0)),
                  pl.BlockSpec((n_ctx, kw), lambda b: (b, 0)), pl.BlockSpec((n_ctx, kw), lambda b: (b, 0))],
        out_specs=[pl.BlockSpec((n_ctx, gw), lambda b: (b, 0)), pl.BlockSpec((n_ctx, gw), lambda b: (b, 0))],
        out_shape=[jax.ShapeDtypeStruct((bsz * n_ctx, gw), BF16)] * 2,
        compiler_params=_cparams(("parallel",)),
        name="ctx_attn",
    )(sink, pc16, pc16, pc16, qx, kxd, vxd)


def _conv_kernel(u_ref, b_ref, c_ref, up_ref, cp_ref, un_ref, cn_ref, w_ref, o_ref, *, tiles_per_seq):
    i = pl.program_id(0)
    tm = u_ref.shape[0]
    pos = i % tiles_per_seq
    keep_prev = jnp.where(pos == 0, 0.0, 1.0)
    keep_next = jnp.where(pos == tiles_per_seq - 1, 0.0, 1.0)
    z = c_ref[...] * u_ref[...]
    z_prev = (cp_ref[...] * up_ref[...])[CONV_HALO - 1:CONV_HALO, :] * keep_prev
    z_next = (cn_ref[...] * un_ref[...])[0:1, :] * keep_next
    row = lax.broadcasted_iota(jnp.int32, z.shape, 0)
    zm1 = jnp.where(row == 0, z_prev, pltpu.roll(z, 1, 0))
    zp1 = jnp.where(row == tm - 1, z_next, pltpu.roll(z, tm - 1, 0))
    w = w_ref[...]
    o_ref[...] = (b_ref[...] * (w[0:1] * zm1 + w[1:2] * z + w[2:3] * zp1)).astype(o_ref.dtype)


def _conv_call(p32, conv_w, *, col_u, col_b, col_c, rows_per_seq):
    m = p32.shape[0]
    gw = conv_w.shape[1]
    tm = _pick_tile(rows_per_seq, 512)
    hb = tm // CONV_HALO
    n_halo = m // CONV_HALO
    cur = lambda col: (lambda i: (i, col // gw))
    prv = lambda col: (lambda i: (jnp.maximum(i * hb - 1, 0), col // gw))
    nxt = lambda col: (lambda i: (jnp.minimum((i + 1) * hb, n_halo - 1), col // gw))
    return pl.pallas_call(
        functools.partial(_conv_kernel, tiles_per_seq=rows_per_seq // tm),
        grid=(m // tm,),
        in_specs=[pl.BlockSpec((tm, gw), cur(col_u)), pl.BlockSpec((tm, gw), cur(col_b)), pl.BlockSpec((tm, gw), cur(col_c)),
                  pl.BlockSpec((CONV_HALO, gw), prv(col_u)), pl.BlockSpec((CONV_HALO, gw), prv(col_c)),
                  pl.BlockSpec((CONV_HALO, gw), nxt(col_u)), pl.BlockSpec((CONV_HALO, gw), nxt(col_c)),
                  pl.BlockSpec((SC_KSIZE, gw), lambda i: (0, 0))],
        out_specs=pl.BlockSpec((tm, gw), lambda i: (i, 0)),
        out_shape=jax.ShapeDtypeStruct((m, gw), BF16),
        compiler_params=_cparams(("parallel",)),
        name="short_conv",
    )(p32, p32, p32, p32, p32, p32, p32, conv_w)


def _gla_kernel(*refs, reverse, nt, fuse_out):
    if fuse_out:
        (q_ref, k_ref, v_ref, z_ref, w2_ref, gb_ref, s0_ref, of_ref, g_ref, gain_ref, o_ref, sfin_ref, st_ref) = refs
    else:
        (q_ref, k_ref, v_ref, z_ref, w2_ref, gb_ref, s0_ref, o_ref, sfin_ref, st_ref) = refs
    i = pl.program_id(1)

    @pl.when(i == 0)
    def _():
        st_ref[...] = s0_ref[...]

    tile = q_ref.shape[0]
    c_len = min(GLA_C, tile)
    dk2 = PAIR_W
    dv2 = v_ref.shape[1] // (q_ref.shape[1] // dk2)
    dv = dv2 // 2
    u = jnp.dot(z_ref[...].astype(BF16), w2_ref[...], preferred_element_type=F32) + gb_ref[...]
    la = (jnp.minimum(u, 0.0) - jnp.log1p(jnp.exp(-jnp.abs(u)))) * (1.0 / GLA_TAU)

    r_i = lax.broadcasted_iota(jnp.int32, (c_len, c_len), 0)
    c_i = lax.broadcasted_iota(jnp.int32, (c_len, c_len), 1)
    tri = (r_i <= c_i) if reverse else (r_i >= c_i)
    tri_bf = jnp.where(tri, 1.0, 0.0).astype(BF16)
    r2 = lax.broadcasted_iota(jnp.int32, (2 * c_len, c_len), 0) % c_len
    c2 = lax.broadcasted_iota(jnp.int32, (2 * c_len, c_len), 1)
    tri2 = (r2 <= c2) if reverse else (r2 >= c2)
    bd_r = lax.broadcasted_iota(jnp.int32, (dv2, dk2), 0) // dv
    bd_c = lax.broadcasted_iota(jnp.int32, (dv2, dk2), 1) // HEAD_DIM
    block_diag = bd_r == bd_c
    last_row = 0 if reverse else c_len - 1
    mid_row = c_len // 2

    n_chunks = tile // c_len
    order = range(n_chunks - 1, -1, -1) if reverse else range(n_chunks)
    for c in order:
        rows = slice(c * c_len, (c + 1) * c_len)
        la_c = la[rows]
        la_hi = la_c.astype(BF16)
        la_lo = (la_c - la_hi.astype(F32)).astype(BF16)
        cum = (jnp.dot(tri_bf, la_hi, preferred_element_type=F32)
               + jnp.dot(tri_bf, la_lo, preferred_element_type=F32))
        last = cum[last_row:last_row + 1]
        cmid = cum[mid_row:mid_row + 1]
        qc = q_ref[rows, :] * ATTN_SCALE
        kc = k_ref[rows, :]
        vc = v_ref[rows, :].astype(BF16)
        q_in = (qc * jnp.exp(cum)).astype(BF16)
        q_t = qc * jnp.exp(cum - cmid)
        k_t = (kc * jnp.exp(cmid - cum)).astype(BF16)
        k_p = (kc * jnp.exp(last - cum)).astype(BF16)
        g = jnp.exp(last)
        for p in range(q_ref.shape[1] // dk2):
            ls = slice(p * dk2, (p + 1) * dk2)
            a = lax.dot_general(_stack_pair(q_t[:, ls]), k_t[:, ls], _NT, preferred_element_type=F32)
            a = jnp.where(tri2, a, 0.0).astype(BF16)
            o0 = jnp.dot(a[:c_len], vc[:, p * dv2:p * dv2 + dv], preferred_element_type=F32)
            o1 = jnp.dot(a[c_len:], vc[:, p * dv2 + dv:(p + 1) * dv2], preferred_element_type=F32)
            st = st_ref[p]
            o_int = lax.dot_general(q_in[:, ls], st.astype(BF16), _NT, preferred_element_type=F32)
            o_p = jnp.concatenate([o0, o1], axis=1) + o_int
            upd = lax.dot_general(vc[:, p * dv2:(p + 1) * dv2], k_p[:, ls], _TN, preferred_element_type=F32)
            st_ref[p] = g[:, ls] * st + jnp.where(block_diag, upd, 0.0)
            if not fuse_out:
                o_ref[rows, p * dv2:(p + 1) * dv2] = o_p
            else:
                tot = of_ref[rows, p * dv2:(p + 1) * dv2] + o_p
                for hh in range(2):
                    hs = slice(p * dv2 + hh * dv, p * dv2 + (hh + 1) * dv)
                    oh = tot[:, hh * dv:(hh + 1) * dv]
                    on = oh * lax.rsqrt(jnp.mean(oh * oh, axis=-1, keepdims=True) + LN_EPS) * gain_ref[...]
                    o_ref[rows, hs] = (on * _silu(g_ref[rows, hs])).astype(o_ref.dtype)

    @pl.when(i == nt - 1)
    def _():
        sfin_ref[...] = st_ref[...]


def _gla_call(p, w2pad, gbias, s0, fuse, *, cols, bsz, seq, reverse):
    nq = GLA_HEADS * HEAD_DIM
    nv = s0.shape[1] * s0.shape[2]
    tile = _pick_tile(seq, GLA_TILE)
    nt = seq // tile
    tix = (lambda i: nt - 1 - i) if reverse else (lambda i: i)
    blk = lambda w, col: pl.BlockSpec((tile, w), lambda b, i: (b * nt + tix(i), col // w))
    const2 = lambda b, i: (0, 0)
    st_spec = pl.BlockSpec((None,) + s0.shape[1:], lambda b, i: (b, 0, 0, 0))
    in_specs = [blk(nq, cols['gl_q']), blk(nq, cols['gl_k']), blk(nv, cols['gl_v']), blk(V7X_LANES, cols['gl_z']),
                pl.BlockSpec(w2pad.shape, const2), pl.BlockSpec(gbias.shape, const2), st_spec]
    args = [p, p, p, p, w2pad, gbias, s0]
    if fuse is not None:
        o_other, gain = fuse
        in_specs += [blk(nv, 0), blk(nv, cols['gl_g']), pl.BlockSpec(gain.shape, const2)]
        args += [o_other, p, gain]
    out_dtype = BF16 if fuse is not None else F32
    return pl.pallas_call(
        functools.partial(_gla_kernel, reverse=reverse, nt=nt, fuse_out=fuse is not None),
        grid=(bsz, nt),
        in_specs=in_specs,
        out_specs=[blk(nv, 0), st_spec],
        out_shape=[jax.ShapeDtypeStruct((bsz * seq, nv), out_dtype), jax.ShapeDtypeStruct(s0.shape, F32)],
        scratch_shapes=[pltpu.VMEM(s0.shape[1:], F32)],
        compiler_params=_cparams(("parallel", "arbitrary")),
        name="gla_bwd" if reverse else "gla_fwd",
    )(*args)


def _gla_all(p32, pc32, w2, gb, gain, *, bsz, seq, n_ctx, d):
    cols = _p32_cols(d)
    nq = GLA_HEADS * HEAD_DIM
    dv = d // 4 // GLA_HEADS
    w2pad = [jnp.zeros((V7X_LANES, nq), F32).at[GLA_RANK * k:GLA_RANK * (k + 1)].set(w2[k]).astype(BF16) for k in range(2)]
    gbias = [gb[k].reshape(1, nq) for k in range(2)]
    s0 = jnp.zeros((bsz, GLA_HEADS // 2, 2 * dv, PAIR_W), F32)
    gain2 = gain.reshape(1, dv)
    oc_f, sc_f = _gla_call(pc32, w2pad[0], gbias[0], s0, None, cols=cols, bsz=bsz, seq=n_ctx, reverse=False)
    y_ctx, sc_b = _gla_call(pc32, w2pad[1], gbias[1], s0, (oc_f, gain2), cols=cols, bsz=bsz, seq=n_ctx, reverse=True)
    o_f, _ = _gla_call(p32, w2pad[0], gbias[0], sc_f, None, cols=cols, bsz=bsz, seq=seq, reverse=False)
    y_lat, _ = _gla_call(p32, w2pad[1], gbias[1], sc_b, (o_f, gain2), cols=cols, bsz=bsz, seq=seq, reverse=True)
    return y_lat, y_ctx


def _swiglu(x, wg, wu, wd):
    g = jnp.dot(x, wg, preferred_element_type=F32)
    u = jnp.dot(x, wu, preferred_element_type=F32)
    a = (_silu(g) * u).astype(BF16)
    return jnp.dot(a, wd, preferred_element_type=F32)


def _first_argmax(vals, idx, sentinel):
    m = jnp.max(vals, axis=0, keepdims=True)
    first = jnp.min(jnp.where(vals == m, idx, sentinel), axis=0, keepdims=True)
    return m, first


def _router_kernel(h_ref, wr_ref, rb_ref, su_ref, eidx_ref, wts_ref, rank_ref, cnt_ref, run_ref):
    i = pl.program_id(0)

    @pl.when(i == 0)
    def _():
        run_ref[...] = jnp.zeros_like(run_ref)

    t = h_ref.shape[0]
    gsz = N_EXPERTS // N_GROUPS
    logits = lax.dot_general(wr_ref[...], h_ref[...], _NT, preferred_element_type=F32)
    scores = jax.nn.sigmoid(logits)
    sel = scores + rb_ref[...]
    sub = lax.broadcasted_iota(jnp.int32, (gsz, t), 0)
    gscore = []
    for g in range(N_GROUPS):
        blk = sel[g * gsz:(g + 1) * gsz]
        m1, a1 = _first_argmax(blk, sub, gsz)
        m2 = jnp.max(jnp.where(sub == a1, -jnp.inf, blk), axis=0, keepdims=True)
        gscore.append(m1 + m2)
    gcur = jnp.concatenate(gscore, axis=0)
    gid = lax.broadcasted_iota(jnp.int32, (N_GROUPS, t), 0)
    gkeep = jnp.zeros((N_GROUPS, t), F32)
    for _ in range(TOPK_GROUPS):
        _, a = _first_argmax(gcur, gid, N_GROUPS)
        hit = gid == a
        gkeep = jnp.where(hit, 1.0, gkeep)
        gcur = jnp.where(hit, -jnp.inf, gcur)
    cur = jnp.concatenate(
        [jnp.where(gkeep[g:g + 1] > 0.0, sel[g * gsz:(g + 1) * gsz], -jnp.inf) for g in range(N_GROUPS)], axis=0)
    eid = lax.broadcasted_iota(jnp.int32, (N_EXPERTS, t), 0)
    chosen = jnp.zeros((N_EXPERTS, t), F32)
    hits, picks, wraw = [], [], []
    for _ in range(TOP_K):
        _, a = _first_argmax(cur, eid, N_EXPERTS)
        hit = eid == a
        hits.append(hit)
        picks.append(a)
        wraw.append(jnp.sum(jnp.where(hit, scores, 0.0), axis=0, keepdims=True))
        chosen = jnp.where(hit, 1.0, chosen)
        cur = jnp.where(hit, -jnp.inf, cur)
    wsum = wraw[0]
    for w in wraw[1:]:
        wsum = wsum + w
    eidx_ref[...] = jnp.concatenate(picks, axis=0)
    wts_ref[...] = jnp.concatenate([w / wsum * ROUTED_SCALE for w in wraw], axis=0)
    before = jnp.dot(chosen.astype(BF16), su_ref[...], preferred_element_type=F32) + run_ref[...][:, 0:1]
    rank_ref[...] = jnp.concatenate(
        [jnp.sum(jnp.where(hit, before, 0.0), axis=0, keepdims=True) for hit in hits], axis=0).astype(jnp.int32)
    run_ref[...] = run_ref[...] + jnp.sum(chosen, axis=1, keepdims=True)
    cnt_ref[...] = run_ref[...]


def _router_call(h, w_router, router_bias):
    n, d = h.shape
    t = _pick_tile(n, ROUTER_TILE)
    wr_t = w_router.T.astype(BF16)
    strict_upper = jnp.triu(jnp.ones((t, t), F32), 1).astype(BF16)
    const2 = lambda i: (0, 0)
    tok = lambda i: (0, i)
    eidx, wts, rank, cnt = pl.pallas_call(
        _router_kernel,
        grid=(n // t,),
        in_specs=[pl.BlockSpec((t, d), lambda i: (i, 0)), pl.BlockSpec((N_EXPERTS, d), const2),
                  pl.BlockSpec((N_EXPERTS, 1), const2), pl.BlockSpec((t, t), const2)],
        out_specs=[pl.BlockSpec((TOP_K, t), tok), pl.BlockSpec((TOP_K, t), tok), pl.BlockSpec((TOP_K, t), tok),
                   pl.BlockSpec((N_EXPERTS, V7X_LANES), const2)],
        out_shape=[jax.ShapeDtypeStruct((TOP_K, n), jnp.int32), jax.ShapeDtypeStruct((TOP_K, n), F32),
                   jax.ShapeDtypeStruct((TOP_K, n), jnp.int32), jax.ShapeDtypeStruct((N_EXPERTS, V7X_LANES), F32)],
        scratch_shapes=[pltpu.VMEM((N_EXPERTS, V7X_LANES), F32)],
        compiler_params=_cparams(("arbitrary",)),
        name="router",
    )(h, wr_t, router_bias.reshape(N_EXPERTS, 1).astype(F32), strict_upper)
    return eidx, wts, rank, cnt[:, 0].astype(jnp.int32)


def _dispatch_kernel(zoff_ref, nused_ref, dest_ref, hp_ref, xs_hbm, zbuf, sem, zsem):
    i = pl.program_id(0)
    t = dest_ref.shape[1]
    n_blocks = xs_hbm.shape[0] // MOE_TB

    def clear_block(off):
        return pltpu.make_async_copy(zbuf, xs_hbm.at[pl.ds(pl.multiple_of(off, MOE_TB), MOE_TB)], zsem)

    @pl.when(i == 0)
    def _():
        zbuf[...] = jnp.zeros_like(zbuf)
        for e in range(N_EXPERTS):
            clear_block(zoff_ref[e]).start()
        for e in range(N_EXPERTS):
            clear_block(0).wait()

        def clear_tail(b, carry):
            cp = clear_block(b * MOE_TB)
            cp.start()
            cp.wait()
            return carry

        lax.fori_loop(nused_ref[0], n_blocks, clear_tail, 0)

    def row_copy(j, slot):
        return pltpu.make_async_copy(hp_ref.at[pl.ds(j, 1)], xs_hbm.at[pl.ds(slot, 1)], sem)

    def issue(j, carry):
        for k in range(TOP_K):
            row_copy(j, dest_ref[k, j]).start(priority=k % 2)
        return carry

    def drain(j, carry):
        for k in range(TOP_K):
            row_copy(0, 0).wait()
        return carry

    lax.fori_loop(0, t, issue, 0)
    lax.fori_loop(0, t, drain, 0)


def _dispatch_call(zero_off, n_used, dest, h_packed, n_rows):
    n, c = h_packed.shape
    t = _pick_tile(n, DISPATCH_TILE)
    grid_spec = pltpu.PrefetchScalarGridSpec(
        num_scalar_prefetch=2,
        grid=(n // t,),
        in_specs=[pl.BlockSpec((TOP_K, t), lambda i, z, nu: (0, i), memory_space=pltpu.SMEM),
                  pl.BlockSpec((t, c), lambda i, z, nu: (i, 0))],
        out_specs=pl.BlockSpec(memory_space=pl.ANY),
        scratch_shapes=[pltpu.VMEM((MOE_TB, c), jnp.uint32), pltpu.SemaphoreType.DMA, pltpu.SemaphoreType.DMA],
    )
    return pl.pallas_call(
        _dispatch_kernel,
        grid_spec=grid_spec,
        out_shape=jax.ShapeDtypeStruct((n_rows, c), jnp.uint32),
        compiler_params=_cparams(("arbitrary",)),
        name="moe_dispatch",
    )(zero_off, n_used, dest, h_packed)


def _expert_kernel(be_ref, nused_ref, x_ref, wg_ref, wu_ref, wd_ref, y_ref, wg_s, wu_s, wd_s):
    b = pl.program_id(0)
    prev = be_ref[jnp.maximum(b - 1, 0)]

    @pl.when((b == 0) | (be_ref[b] != prev))
    def _():
        wg_s[...] = wg_ref[...].astype(BF16)
        wu_s[...] = wu_ref[...].astype(BF16)
        wd_s[...] = wd_ref[...].astype(BF16)

    @pl.when(b < nused_ref[0])
    def _():
        y_ref[...] = _pack_halves(_swiglu(_unpack_halves(x_ref[...]), wg_s[...], wu_s[...], wd_s[...]))

    @pl.when(b >= nused_ref[0])
    def _():
        y_ref[...] = jnp.zeros_like(y_ref)


def _expert_call(block_e, n_used, x_sorted, wg, wu, wd, layer):
    n_rows, c = x_sorted.shape
    _, _, d, de = wg.shape
    n_blocks = n_rows // MOE_TB
    wsel = lambda b, be, nu: (layer, be[b], 0, 0)
    xsel = lambda b, be, nu: (jnp.minimum(b, nu[0] - 1), 0)
    grid_spec = pltpu.PrefetchScalarGridSpec(
        num_scalar_prefetch=2,
        grid=(n_blocks,),
        in_specs=[pl.BlockSpec((MOE_TB, c), xsel), pl.BlockSpec((None, None, d, de), wsel),
                  pl.BlockSpec((None, None, d, de), wsel), pl.BlockSpec((None, None, de, d), wsel)],
        out_specs=pl.BlockSpec((MOE_TB, c), lambda b, be, nu: (b, 0)),
        scratch_shapes=[pltpu.VMEM((d, de), BF16), pltpu.VMEM((d, de), BF16), pltpu.VMEM((de, d), BF16)],
    )
    return pl.pallas_call(
        _expert_kernel,
        grid_spec=grid_spec,
        out_shape=jax.ShapeDtypeStruct((n_rows, c), jnp.uint32),
        compiler_params=_cparams(("arbitrary",)),
        name="experts",
    )(block_e, n_used, x_sorted, wg, wu, wd)


def _combine_kernel(dest_ref, w_ref, h_ref, x_ref, g2_ref, wsg_ref, wsu_ref, wsd_ref, lng_ref, lnb_ref, y_hbm,
                    o_ref, ybuf, sem, *, alpha):
    t = h_ref.shape[0]

    def row_copy(slot, k, j):
        return pltpu.make_async_copy(y_hbm.at[pl.ds(slot, 1)], ybuf.at[k, pl.ds(j, 1)], sem)

    def issue(j, carry):
        for k in range(TOP_K):
            row_copy(dest_ref[k, j], k, j).start(priority=k % 2)
        return carry

    def drain(j, carry):
        for k in range(TOP_K):
            row_copy(0, k, j).wait()
        return carry

    lax.fori_loop(0, t, issue, 0)
    acc = _swiglu(h_ref[...], wsg_ref[...], wsu_ref[...], wsd_ref[...])
    lax.fori_loop(0, t, drain, 0)
    w = w_ref[...]
    half = acc.shape[1] // 2
    lo, hi = acc[:, :half], acc[:, half:]
    for k in range(TOP_K):
        yk = ybuf[k]
        wk = w[:, k:k + 1]
        lo = lo + wk * pltpu.bitcast(lax.shift_left(yk, jnp.uint32(16)), F32)
        hi = hi + wk * pltpu.bitcast(yk & jnp.uint32(0xFFFF0000), F32)
    acc = jnp.concatenate([lo, hi], axis=1)
    r = alpha * x_ref[...] + g2_ref[0] * acc
    o_ref[...] = _ln_rows(r) * lng_ref[...] + lnb_ref[...]


def _combine_call(dest, wts_t, h, x2d, gate2, wsg, wsu, wsd, ln_g, ln_b, y_sorted, *, row0, rows_per_group, alpha):
    m, d = x2d.shape
    de = wsg.shape[1]
    t = _pick_tile(min(rows_per_group, m), COMBINE_TILE)
    assert row0 % t == 0 and rows_per_group % t == 0
    t0 = row0 // t
    const2 = lambda i: (0, 0)
    once = pl.Buffered(1)
    return pl.pallas_call(
        functools.partial(_combine_kernel, alpha=alpha),
        grid=(m // t,),
        in_specs=[pl.BlockSpec((TOP_K, t), lambda i: (0, t0 + i), memory_space=pltpu.SMEM),
                  pl.BlockSpec((t, TOP_K), lambda i: (t0 + i, 0)),
                  pl.BlockSpec((t, d), lambda i: (t0 + i, 0)),
                  pl.BlockSpec((t, d), lambda i: (i, 0)),
                  pl.BlockSpec((1, 1, d), lambda i: ((i * t) // rows_per_group, 0, 0)),
                  pl.BlockSpec((d, de), const2, pipeline_mode=once),
                  pl.BlockSpec((d, de), const2, pipeline_mode=once),
                  pl.BlockSpec((de, d), const2, pipeline_mode=once),
                  pl.BlockSpec((1, d), const2), pl.BlockSpec((1, d), const2),
                  pl.BlockSpec(memory_space=pl.ANY)],
        out_specs=pl.BlockSpec((t, d), lambda i: (i, 0)),
        out_shape=jax.ShapeDtypeStruct((m, d), F32),
        scratch_shapes=[pltpu.VMEM((TOP_K, t, d // 2), jnp.uint32), pltpu.SemaphoreType.DMA],
        compiler_params=_cparams(("arbitrary",)),
        name="moe_combine",
    )(dest, wts_t, h, x2d, gate2, wsg, wsu, wsd, ln_g.reshape(1, d), ln_b.reshape(1, d), y_sorted)


def _moe_routed(h, h_packed, w_router, router_bias, w_eg, w_eu, w_ed, layer):
    n = h.shape[0]
    eidx, wts, rank, counts = _router_call(h, w_router, router_bias)
    padded = (counts + MOE_TB - 1) // MOE_TB * MOE_TB
    pad_end = jnp.cumsum(padded)
    pad_start = pad_end - padded
    n_blocks = (n * TOP_K + N_EXPERTS * (MOE_TB - 1) + MOE_TB - 1) // MOE_TB
    block_first = jnp.arange(n_blocks, dtype=jnp.int32) * MOE_TB
    block_e = jnp.minimum(jnp.sum((pad_end[None, :] <= block_first[:, None]).astype(jnp.int32), axis=1), N_EXPERTS - 1)
    n_used = (pad_end[-1:] // MOE_TB).astype(jnp.int32)
    zero_off = jnp.maximum(pad_end - MOE_TB, 0).astype(jnp.int32)
    expert_ids = jnp.arange(N_EXPERTS, dtype=jnp.int32)
    first_row = jnp.sum(jnp.where(eidx[:, :, None] == expert_ids, pad_start.astype(jnp.int32), 0), axis=-1)
    dest = first_row + rank
    x_sorted = _dispatch_call(zero_off, n_used, dest, h_packed, n_blocks * MOE_TB)
    y_sorted = _expert_call(block_e, n_used, x_sorted, w_eg, w_eu, w_ed, layer)
    return dest, wts.T, y_sorted


def _pack_w_in(w, d):
    gw = d // 4
    cuts = np.cumsum([gw, gw, gw, gw, gw, gw, gw // 2, gw // 2, gw, gw, 2 * GLA_RANK, gw, gw // 4, gw // 4])[:-1].tolist()
    (na_q, na_k, na_v, sc_u, sc_b, sc_c, gl_q, gl_k, gl_v, gl_g, gl_z, sw_q, sw_k, sw_v) = jnp.split(w, cuts, axis=1)
    zpad = jnp.zeros((d, V7X_LANES - 2 * GLA_RANK), w.dtype)
    cols = [na_q, na_k, na_v, sc_u, sc_b, sc_c, gl_v, gl_g, sw_q, gl_q, gl_k, sw_k, sw_v, gl_z, zpad]
    packed = jnp.concatenate(cols, axis=1)
    pad = (-packed.shape[1]) % PROJ_TN
    packed = jnp.pad(packed, ((0, 0), (0, pad)))
    return packed.astype(BF16)


def _p32_cols(d):
    gw = d // 4
    cols, c = {}, 0
    for name, wdt in (('sc_u', gw), ('sc_b', gw), ('sc_c', gw), ('gl_v', gw), ('gl_g', gw), ('sw_q', gw),
                      ('gl_q', gw // 2), ('gl_k', gw // 2), ('sw_k', gw // 4), ('sw_v', gw // 4), ('gl_z', V7X_LANES)):
        cols[name] = c
        c += wdt
    return cols


def _token_mixers(p16, p32, pc16, pc32, rpb, conv_w, gla_w2, gla_b, gla_norm_g, sink, with_ctx_out, *,
                  bsz, seq, n_ctx, d):
    cols = _p32_cols(d)
    kh = min(NA_KH, seq // GRID_W)
    y_na = _na_call(p16, pc16, _na_bias_table(rpb, kh), bsz=bsz, seq=seq, n_ctx=n_ctx)
    conv_cols = dict(col_u=cols['sc_u'], col_b=cols['sc_b'], col_c=cols['sc_c'])
    y_sc = _conv_call(p32, conv_w, rows_per_seq=seq, **conv_cols)
    y_gl, yc_gl = _gla_all(p32, pc32, gla_w2, gla_b, gla_norm_g, bsz=bsz, seq=seq, n_ctx=n_ctx, d=d)
    tables = _rope_tables(seq)
    swa_cols = dict(col_q=cols['sw_q'], col_k=cols['sw_k'], col_v=cols['sw_v'])
    qr, kd, vd = _swa_prep_call(p32, tables, rows_per_seq=seq, rope=True, **swa_cols)
    ctx_tables = tuple(t[:n_ctx] for t in tables)
    qx, kxd, vxd = _swa_prep_call(pc32, ctx_tables, rows_per_seq=n_ctx, rope=False, **swa_cols)
    y_sw = _swa_call(sink, qr, kd, vd, kxd, vxd, bsz=bsz, seq=seq, n_ctx=n_ctx)
    y_lat = (y_na, y_sc, y_gl, y_sw)
    if not with_ctx_out:
        return y_lat, None
    yc_na, yc_sw = _ctx_attn_call(sink, pc16, qx, kxd, vxd, bsz=bsz, n_ctx=n_ctx)
    yc_sc = _conv_call(pc32, conv_w, rows_per_seq=n_ctx, **conv_cols)
    return y_lat, (yc_na, yc_sc, yc_gl, yc_sw)


def kernel(x, c, ctx, c_ctx, w_ada, b_ada, w_in, na_rpb, conv_w, gla_w2, gla_b, gla_norm_g, swa_sink, w_out,
           ln1_g, ln1_b, w_router, router_bias, w_exp_gate, w_exp_up, w_exp_down, w_sh_gate, w_sh_up, w_sh_down,
           ln2_g, ln2_b):
    bsz, seq, d = x.shape
    n_ctx = ctx.shape[1]
    nc = bsz * n_ctx
    depth = w_in.shape[0]
    alpha = (2 * depth) ** 0.25
    n16 = 3 * (d // 4)
    x2 = x.reshape(bsz * seq, d)
    hc2 = ctx.reshape(nc, d)
    c_rows = jnp.zeros((8, d), F32).at[:bsz].set(c).at[bsz].set(c_ctx)
    for layer in range(depth):
        last = layer == depth - 1
        mod = _ada_call(c_rows, w_ada, b_ada[layer], layer)
        sh1, sc1, g1, sh2, sc2, g2 = [t[:bsz, None, :] for t in jnp.split(mod, 6, axis=-1)]
        sh1c, sc1c, g1c, sh2c, sc2c, g2c = [t[bsz:bsz + 1, None, :] for t in jnp.split(mod, 6, axis=-1)]
        w_p = _pack_w_in(w_in[layer], d)
        p16, p32 = _proj_call(x2, sh1, sc1, w_p, rows_per_group=seq, n16=n16)
        pc16, pc32 = _proj_call(hc2, sh1c, sc1c, w_p, rows_per_group=nc, n16=n16)
        y_lat, y_ctx = _token_mixers(p16, p32, pc16, pc32, na_rpb[layer], conv_w[layer], gla_w2[layer], gla_b[layer],
                                     gla_norm_g[layer], swa_sink[layer], not last, bsz=bsz, seq=seq, n_ctx=n_ctx, d=d)
        w_o = w_out[layer].astype(BF16)
        x2, h_lat, hp_lat = _outproj_call(y_lat, x2, g1, w_o, ln1_g[layer], ln1_b[layer], sh2, sc2,
                                          rows_per_group=seq, alpha=alpha)
        shared_w = (w_sh_gate[layer].astype(BF16), w_sh_up[layer].astype(BF16), w_sh_down[layer].astype(BF16))
        route_w = (w_router[layer], router_bias[layer], w_exp_gate, w_exp_up, w_exp_down, layer)
        if last:
            dest, wts_t, y_sorted = _moe_routed(h_lat, hp_lat, *route_w)
            x2 = _combine_call(dest, wts_t, h_lat, x2, g2, *shared_w, ln2_g[layer], ln2_b[layer], y_sorted,
                               row0=0, rows_per_group=seq, alpha=alpha)
        else:
            hc2, h_ctx, hp_ctx = _outproj_call(y_ctx, hc2, g1c, w_o, ln1_g[layer], ln1_b[layer], sh2c, sc2c,
                                               rows_per_group=nc, alpha=alpha)
            h_all = jnp.concatenate([h_ctx, h_lat], axis=0)
            hp_all = jnp.concatenate([hp_ctx, hp_lat], axis=0)
            dest, wts_t, y_sorted = _moe_routed(h_all, hp_all, *route_w)
            hc2 = _combine_call(dest, wts_t, h_all, hc2, g2c, *shared_w, ln2_g[layer], ln2_b[layer], y_sorted,
                                row0=0, rows_per_group=nc, alpha=alpha)
            x2 = _combine_call(dest, wts_t, h_all, x2, g2, *shared_w, ln2_g[layer], ln2_b[layer], y_sorted,
                               row0=nc, rows_per_group=seq, alpha=alpha)
    return x2.reshape(bsz, seq, d)
```

```python
import functools

import jax
import jax.numpy as jnp
import numpy as np
from jax import lax
from jax.experimental import pallas as pl
from jax.experimental.pallas import tpu as pltpu

GRID_W = 64
HEAD_DIM = 64
NA_KH = 8
NA_KW = 16
SC_KSIZE = 3
GLA_HEADS = 4
GLA_RANK = 16
GLA_TAU = 16.0
SWA_KV_HEADS = 2
SWA_BLOCK = 128
ROPE_BASE = 10000.0
N_EXPERTS = 64
TOP_K = 8
N_GROUPS = 8
TOPK_GROUPS = 4
ROUTED_SCALE = 2.5
LN_EPS = 1e-6

V7X_LANES = 128
V7X_VMEM_LIMIT_BYTES = 48 * 1024 * 1024

PROJ_TM = 1024
PROJ_TN = 512
OUT_TM = 512
MOE_TB = 256
ROUTER_TILE = 512
DISPATCH_TILE = 256
COMBINE_TILE = 128
CONV_HALO = 8
GLA_C = 64
GLA_TILE = 512
ADA_TN = 2048

F32 = jnp.float32
BF16 = jnp.bfloat16
PAIR_W = 2 * HEAD_DIM
ATTN_SCALE = HEAD_DIM ** -0.5
_NT = (((1,), (1,)), ((), ()))
_TN = (((0,), (0,)), ((), ()))


def _cparams(sem):
    return pltpu.CompilerParams(dimension_semantics=sem, vmem_limit_bytes=V7X_VMEM_LIMIT_BYTES)


def _pick_tile(m, preferred):
    t = preferred
    while t > 8 and m % t:
        t //= 2
    assert m % t == 0, (m, preferred)
    return t


def _ln_rows(x):
    mu = jnp.mean(x, axis=-1, keepdims=True)
    xc = x - mu
    var = jnp.mean(xc * xc, axis=-1, keepdims=True)
    return xc * lax.rsqrt(var + LN_EPS)


def _silu(x):
    return x * jax.nn.sigmoid(x)


def _ada_kernel(c_ref, w_ref, b_ref, o_ref):
    a = _silu(c_ref[...]).astype(BF16)
    o_ref[...] = jnp.dot(a, w_ref[...].astype(BF16), preferred_element_type=F32) + b_ref[...]


def _ada_call(c_rows, w_ada, b_ada, layer):
    r, d = c_rows.shape
    n = w_ada.shape[2]
    tn = _pick_tile(n, ADA_TN)
    return pl.pallas_call(
        _ada_kernel,
        grid=(n // tn,),
        in_specs=[pl.BlockSpec((r, d), lambda j: (0, 0)), pl.BlockSpec((None, d, tn), lambda j: (layer, 0, j)),
                  pl.BlockSpec((1, tn), lambda j: (0, j))],
        out_specs=pl.BlockSpec((r, tn), lambda j: (0, j)),
        out_shape=jax.ShapeDtypeStruct((r, n), F32),
        compiler_params=_cparams(("parallel",)),
        name="ada_mod",
    )(c_rows, w_ada, b_ada.reshape(1, n))


def _proj_kernel(x_ref, sh_ref, sc_ref, w_ref, o16_ref, o32_ref, xn_ref, *, nb16):
    j = pl.program_id(1)

    @pl.when(j == 0)
    def _():
        y = _ln_rows(x_ref[...]) * (1.0 + sc_ref[0]) + sh_ref[0]
        xn_ref[...] = y.astype(BF16)

    acc = jnp.dot(xn_ref[...], w_ref[...], preferred_element_type=F32)

    @pl.when(j < nb16)
    def _():
        o16_ref[...] = acc.astype(BF16)

    @pl.when(j >= nb16)
    def _():
        o32_ref[...] = acc


def _proj_call(x2d, shift, scale, w_packed, *, rows_per_group, n16):
    m, d = x2d.shape
    ntot = w_packed.shape[1]
    tm = min(PROJ_TM, rows_per_group)
    assert m % tm == 0 and rows_per_group % tm == 0 and ntot % PROJ_TN == 0 and n16 % PROJ_TN == 0
    nb16 = n16 // PROJ_TN
    nb = ntot // PROJ_TN
    grp = lambda i, j: ((i * tm) // rows_per_group, 0, 0)
    return pl.pallas_call(
        functools.partial(_proj_kernel, nb16=nb16),
        grid=(m // tm, nb),
        in_specs=[
            pl.BlockSpec((tm, d), lambda i, j: (i, 0)),
            pl.BlockSpec((1, 1, d), grp),
            pl.BlockSpec((1, 1, d), grp),
            pl.BlockSpec((d, PROJ_TN), lambda i, j: (0, j)),
        ],
        out_specs=[
            pl.BlockSpec((tm, PROJ_TN), lambda i, j: (i, jnp.minimum(j, nb16 - 1))),
            pl.BlockSpec((tm, PROJ_TN), lambda i, j: (i, jnp.maximum(j - nb16, 0))),
        ],
        out_shape=[jax.ShapeDtypeStruct((m, n16), BF16), jax.ShapeDtypeStruct((m, ntot - n16), F32)],
        scratch_shapes=[pltpu.VMEM((tm, d), BF16)],
        compiler_params=_cparams(("parallel", "arbitrary")),
        name="proj",
    )(x2d, shift, scale, w_packed)


def _pack_halves(h):
    c = h.shape[1] // 2
    lo = pltpu.bitcast(h[:, :c].astype(BF16).astype(F32), jnp.uint32)
    hi = pltpu.bitcast(h[:, c:].astype(BF16).astype(F32), jnp.uint32)
    return lax.shift_right_logical(lo, jnp.uint32(16)) | (hi & jnp.uint32(0xFFFF0000))


def _unpack_halves(w):
    lo = pltpu.bitcast(lax.shift_left(w, jnp.uint32(16)), F32)
    hi = pltpu.bitcast(w & jnp.uint32(0xFFFF0000), F32)
    return jnp.concatenate([lo, hi], axis=1).astype(BF16)


def _outproj_kernel(y0_ref, y1_ref, y2_ref, y3_ref, x_ref, g1_ref, w_ref, lng_ref, lnb_ref,
                    sh2_ref, sc2_ref, xo_ref, h_ref, hp_ref, *, alpha):
    gw = y0_ref.shape[1]
    acc = jnp.dot(y0_ref[...], w_ref[0:gw, :], preferred_element_type=F32)
    acc += jnp.dot(y1_ref[...], w_ref[gw:2 * gw, :], preferred_element_type=F32)
    acc += jnp.dot(y2_ref[...], w_ref[2 * gw:3 * gw, :], preferred_element_type=F32)
    acc += jnp.dot(y3_ref[...], w_ref[3 * gw:4 * gw, :], preferred_element_type=F32)
    r = alpha * x_ref[...] + g1_ref[0] * acc
    xn = _ln_rows(r) * lng_ref[...] + lnb_ref[...]
    xo_ref[...] = xn
    h = _ln_rows(xn) * (1.0 + sc2_ref[0]) + sh2_ref[0]
    h_ref[...] = h.astype(BF16)
    hp_ref[...] = _pack_halves(h)


def _outproj_call(ys, x2d, gate1, w_out_bf16, ln_g, ln_b, shift2, scale2, *, rows_per_group, alpha):
    m, d = x2d.shape
    gw = d // 4
    tm = min(OUT_TM, rows_per_group)
    assert m % tm == 0 and rows_per_group % tm == 0
    grp = lambda i: ((i * tm) // rows_per_group, 0, 0)
    row = lambda i: (i, 0)
    const2 = lambda i: (0, 0)
    return pl.pallas_call(
        functools.partial(_outproj_kernel, alpha=alpha),
        grid=(m // tm,),
        in_specs=[pl.BlockSpec((tm, gw), row)] * 4 + [
            pl.BlockSpec((tm, d), row),
            pl.BlockSpec((1, 1, d), grp),
            pl.BlockSpec((d, d), const2, pipeline_mode=pl.Buffered(1)),
            pl.BlockSpec((1, d), const2),
            pl.BlockSpec((1, d), const2),
            pl.BlockSpec((1, 1, d), grp),
            pl.BlockSpec((1, 1, d), grp),
        ],
        out_specs=[pl.BlockSpec((tm, d), row), pl.BlockSpec((tm, d), row), pl.BlockSpec((tm, d // 2), row)],
        out_shape=[jax.ShapeDtypeStruct((m, d), F32), jax.ShapeDtypeStruct((m, d), BF16),
                   jax.ShapeDtypeStruct((m, d // 2), jnp.uint32)],
        compiler_params=_cparams(("parallel",)),
        name="outproj",
    )(*ys, x2d, gate1, w_out_bf16, ln_g.reshape(1, d), ln_b.reshape(1, d), shift2, scale2)


def _stack_pair(q2):
    lane = lax.broadcasted_iota(jnp.int32, q2.shape, 1)
    lo = jnp.where(lane < HEAD_DIM, q2, 0.0)
    hi = jnp.where(lane >= HEAD_DIM, q2, 0.0)
    return jnp.concatenate([lo, hi], axis=0).astype(BF16)


def _unstack_pair(o):
    n = o.shape[0] // 2
    lane = lax.broadcasted_iota(jnp.int32, (n, o.shape[1]), 1)
    return jnp.where(lane < HEAD_DIM, o[:n], o[n:])


def _pair_softmax_av(qs, ks, vs, biases, masks, sink_col):
    ss = []
    for k, bia, msk in zip(ks, biases, masks):
        s = lax.dot_general(qs, k, _NT, preferred_element_type=F32)
        if bia is not None:
            s = s + bia
        if msk is not None:
            s = jnp.where(msk, s, -jnp.inf)
        ss.append(s)
    m = jnp.max(ss[0], axis=-1, keepdims=True)
    for s in ss[1:]:
        m = jnp.maximum(m, jnp.max(s, axis=-1, keepdims=True))
    if sink_col is not None:
        m = jnp.maximum(m, sink_col)
        l = jnp.exp(sink_col - m)
    else:
        l = jnp.zeros_like(m)
    o = None
    for s, v in zip(ss, vs):
        e = jnp.exp(s - m)
        l = l + jnp.sum(e, axis=-1, keepdims=True)
        pv = jnp.dot(e.astype(BF16), v, preferred_element_type=F32)
        o = pv if o is None else o + pv
    return o / l


def _na_kernel(q_ref, k_ref, v_ref, kc_ref, vc_ref, bias_ref, o_ref, *, rows, kh):
    r = pl.program_id(1)
    rs = jnp.clip(r - kh // 2, 0, rows - kh)
    start = pl.multiple_of(rs * GRID_W, GRID_W)
    nwin = kh * GRID_W
    for p in range(q_ref.shape[1] // PAIR_W):
        sl = slice(p * PAIR_W, (p + 1) * PAIR_W)
        qs = _stack_pair(q_ref[:, sl].astype(F32) * ATTN_SCALE)
        kw = k_ref[pl.ds(start, nwin), sl]
        vw = v_ref[pl.ds(start, nwin), sl]
        o = _pair_softmax_av(qs, [kw, kc_ref[:, sl]], [vw, vc_ref[:, sl]], [bias_ref[p], None], [None, None], None)
        o_ref[:, sl] = _unstack_pair(o).astype(o_ref.dtype)


def _na_bias_table(rpb, kh):
    nh = rpb.shape[0]
    c = jnp.arange(GRID_W)
    cstart = jnp.clip(c - NA_KW // 2, 0, GRID_W - NA_KW)
    valid = (c[None, :] >= cstart[:, None]) & (c[None, :] < cstart[:, None] + NA_KW)
    coff = jnp.clip(c[None, :] - c[:, None], 1 - NA_KW, NA_KW - 1) + NA_KW - 1
    roff = jnp.arange(kh)[None, :] - jnp.arange(kh)[:, None] + NA_KH - 1
    pick_r = (roff[:, :, None] == jnp.arange(rpb.shape[1])).astype(F32)
    pick_c = (coff[:, :, None] == jnp.arange(rpb.shape[2])).astype(F32)
    bias = jnp.einsum('hab,dia,ckb->hdick', rpb.astype(F32), pick_r, pick_c, precision=lax.Precision.HIGHEST)
    bias = jnp.where(valid[None, None, None], bias, -jnp.inf)
    bias = bias.transpose(1, 0, 3, 2, 4)
    return bias.reshape(kh, nh // 2, 2 * GRID_W, kh * GRID_W)


def _na_call(p16, pc16, bias_tab, *, bsz, seq, n_ctx):
    gw = p16.shape[1] // 3
    rows = seq // GRID_W
    kh = bias_tab.shape[0]
    delta = lambda b, r: (r - jnp.clip(r - kh // 2, 0, rows - kh), 0, 0, 0)
    once = pl.Buffered(1)
    return pl.pallas_call(
        functools.partial(_na_kernel, rows=rows, kh=kh),
        grid=(bsz, rows),
        in_specs=[
            pl.BlockSpec((GRID_W, gw), lambda b, r: (b * rows + r, 0)),
            pl.BlockSpec((seq, gw), lambda b, r: (b, 1), pipeline_mode=once),
            pl.BlockSpec((seq, gw), lambda b, r: (b, 2), pipeline_mode=once),
            pl.BlockSpec((n_ctx, gw), lambda b, r: (b, 1)),
            pl.BlockSpec((n_ctx, gw), lambda b, r: (b, 2)),
            pl.BlockSpec((None,) + bias_tab.shape[1:], delta),
        ],
        out_specs=pl.BlockSpec((GRID_W, gw), lambda b, r: (b * rows + r, 0)),
        out_shape=jax.ShapeDtypeStruct((bsz * seq, gw), BF16),
        compiler_params=_cparams(("parallel", "arbitrary")),
        name="na_attn",
    )(p16, p16, p16, pc16, pc16, bias_tab)


def _dup_heads(t):
    lane = lax.broadcasted_iota(jnp.int32, t.shape, 1)
    sw = pltpu.roll(t, HEAD_DIM, 1)
    return jnp.concatenate([jnp.where(lane < HEAD_DIM, t, sw), jnp.where(lane < HEAD_DIM, sw, t)], axis=1)


def _rope(t, cos, sa, sb):
    q = HEAD_DIM // 4
    return t * cos + pltpu.roll(t, q, 1) * sa + pltpu.roll(t, V7X_LANES - q, 1) * sb


def _swa_prep_kernel(q_ref, k_ref, v_ref, cos_ref, sa_ref, sb_ref, qo_ref, ko_ref, vo_ref, *, rope):
    k = k_ref[...]
    if rope:
        cos, sa, sb = cos_ref[...], sa_ref[...], sb_ref[...]
        k = _rope(k, cos, sa, sb)
    for p in range(q_ref.shape[1] // PAIR_W):
        sl = slice(p * PAIR_W, (p + 1) * PAIR_W)
        q = q_ref[:, sl]
        if rope:
            q = _rope(q, cos, sa, sb)
        qo_ref[:, sl] = (q * ATTN_SCALE).astype(BF16)
    ko_ref[...] = _dup_heads(k).astype(BF16)
    vo_ref[...] = _dup_heads(v_ref[...]).astype(BF16)


def _rope_tables(seq):
    t = jnp.arange(seq)
    row = (t // GRID_W).astype(F32)
    col = (t % GRID_W).astype(F32)
    quarter = HEAD_DIM // 4
    inv = ROPE_BASE ** (-2.0 * jnp.arange(quarter, dtype=F32) / (HEAD_DIM // 2))
    ang_r = row[:, None] * inv[None, :]
    ang_c = col[:, None] * inv[None, :]
    zero = jnp.zeros_like(ang_r)
    cos_h = jnp.concatenate([jnp.cos(ang_r)] * 2 + [jnp.cos(ang_c)] * 2, axis=1)
    sa_h = jnp.concatenate([zero, jnp.sin(ang_r), zero, jnp.sin(ang_c)], axis=1)
    sb_h = jnp.concatenate([-jnp.sin(ang_r), zero, -jnp.sin(ang_c), zero], axis=1)
    two = lambda a: jnp.concatenate([a, a], axis=1)
    return two(cos_h), two(sa_h), two(sb_h)


def _swa_prep_call(p32, tables, *, col_q, col_k, col_v, rows_per_seq, rope):
    m = p32.shape[0]
    gw = 4 * PAIR_W
    tm = _pick_tile(rows_per_seq, 512)
    nseq_tiles = rows_per_seq // tm
    tab = lambda i: (i % nseq_tiles, 0)
    return pl.pallas_call(
        functools.partial(_swa_prep_kernel, rope=rope),
        grid=(m // tm,),
        in_specs=[
            pl.BlockSpec((tm, gw), lambda i: (i, col_q // gw)),
            pl.BlockSpec((tm, PAIR_W), lambda i: (i, col_k // PAIR_W)),
            pl.BlockSpec((tm, PAIR_W), lambda i: (i, col_v // PAIR_W)),
            pl.BlockSpec((tm, PAIR_W), tab),
            pl.BlockSpec((tm, PAIR_W), tab),
            pl.BlockSpec((tm, PAIR_W), tab),
        ],
        out_specs=[pl.BlockSpec((tm, gw), lambda i: (i, 0)), pl.BlockSpec((tm, 2 * PAIR_W), lambda i: (i, 0)),
                   pl.BlockSpec((tm, 2 * PAIR_W), lambda i: (i, 0))],
        out_shape=[jax.ShapeDtypeStruct((m, gw), BF16), jax.ShapeDtypeStruct((m, 2 * PAIR_W), BF16),
                   jax.ShapeDtypeStruct((m, 2 * PAIR_W), BF16)],
        compiler_params=_cparams(("parallel",)),
        name="swa_prep",
    )(p32, p32, p32, *tables)


def _sink_col(sink_ref, p, n):
    row = lax.broadcasted_iota(jnp.int32, (2 * n, 1), 0)
    return jnp.where(row < n, sink_ref[2 * p], sink_ref[2 * p + 1])


def _swa_kernel(sink_ref, q_ref, kp_ref, kc_ref, kn_ref, vp_ref, vc_ref, vn_ref, kx_ref, vx_ref, o_ref, *, nblk):
    n = pl.program_id(1)
    blk = q_ref.shape[0]
    qi = lax.broadcasted_iota(jnp.int32, (2 * blk, blk), 0) % blk
    kj = lax.broadcasted_iota(jnp.int32, (2 * blk, blk), 1)
    m_prev = kj >= qi + jnp.where(n > 0, 0, blk)
    m_next = kj <= qi - jnp.where(n < nblk - 1, 0, blk)
    npairs = q_ref.shape[1] // PAIR_W
    for p in range(npairs):
        sl = slice(p * PAIR_W, (p + 1) * PAIR_W)
        g = p // (npairs // SWA_KV_HEADS)
        gs = slice(g * PAIR_W, (g + 1) * PAIR_W)
        qs = _stack_pair(q_ref[:, sl].astype(F32))
        o = _pair_softmax_av(
            qs, [kp_ref[:, gs], kc_ref[:, gs], kn_ref[:, gs], kx_ref[:, gs]],
            [vp_ref[:, gs], vc_ref[:, gs], vn_ref[:, gs], vx_ref[:, gs]],
            [None] * 4, [m_prev, None, m_next, None], _sink_col(sink_ref, p, blk))
        o_ref[:, sl] = _unstack_pair(o).astype(o_ref.dtype)


def _swa_call(sink, qr, kd, vd, kxd, vxd, *, bsz, seq, n_ctx):
    gw = qr.shape[1]
    kw = kd.shape[1]
    nblk = seq // SWA_BLOCK
    cur = lambda b, n: (b * nblk + n, 0)
    prev = lambda b, n: (b * nblk + jnp.maximum(n - 1, 0), 0)
    nxt = lambda b, n: (b * nblk + jnp.minimum(n + 1, nblk - 1), 0)
    cx = lambda b, n: (b, 0)
    return pl.pallas_call(
        functools.partial(_swa_kernel, nblk=nblk),
        grid=(bsz, nblk),
        in_specs=[pl.BlockSpec(memory_space=pltpu.SMEM),
                  pl.BlockSpec((SWA_BLOCK, gw), cur),
                  pl.BlockSpec((SWA_BLOCK, kw), prev), pl.BlockSpec((SWA_BLOCK, kw), cur), pl.BlockSpec((SWA_BLOCK, kw), nxt),
                  pl.BlockSpec((SWA_BLOCK, kw), prev), pl.BlockSpec((SWA_BLOCK, kw), cur), pl.BlockSpec((SWA_BLOCK, kw), nxt),
                  pl.BlockSpec((n_ctx, kw), cx), pl.BlockSpec((n_ctx, kw), cx)],
        out_specs=pl.BlockSpec((SWA_BLOCK, gw), cur),
        out_shape=jax.ShapeDtypeStruct((bsz * seq, gw), BF16),
        compiler_params=_cparams(("parallel", "arbitrary")),
        name="swa_attn",
    )(sink, qr, kd, kd, kd, vd, vd, vd, kxd, vxd)


def _ctx_attn_kernel(sink_ref, qa_ref, ka_ref, va_ref, qd_ref, kd_ref, vd_ref, oa_ref, od_ref):
    n = qa_ref.shape[0]
    npairs = qa_ref.shape[1] // PAIR_W
    for p in range(npairs):
        sl = slice(p * PAIR_W, (p + 1) * PAIR_W)
        qs = _stack_pair(qa_ref[:, sl].astype(F32) * ATTN_SCALE)
        o = _pair_softmax_av(qs, [ka_ref[:, sl]], [va_ref[:, sl]], [None], [None], None)
        oa_ref[:, sl] = _unstack_pair(o).astype(oa_ref.dtype)
    for p in range(npairs):
        sl = slice(p * PAIR_W, (p + 1) * PAIR_W)
        g = p // (npairs // SWA_KV_HEADS)
        gs = slice(g * PAIR_W, (g + 1) * PAIR_W)
        qs = _stack_pair(qd_ref[:, sl].astype(F32))
        o = _pair_softmax_av(qs, [kd_ref[:, gs]], [vd_ref[:, gs]], [None], [None], _sink_col(sink_ref, p, n))
        od_ref[:, sl] = _unstack_pair(o).astype(od_ref.dtype)


def _ctx_attn_call(sink, pc16, qx, kxd, vxd, *, bsz, n_ctx):
    gw = qx.shape[1]
    kw = kxd.shape[1]
    return pl.pallas_call(
        _ctx_attn_kernel,
        grid=(bsz,),
        in_specs=[pl.BlockSpec(memory_space=pltpu.SMEM),
                  pl.BlockSpec((n_ctx, gw), lambda b: (b, 0)), pl.BlockSpec((n_ctx, gw), lambda b: (b, 1)),
                  pl.BlockSpec((n_ctx, gw), lambda b: (b, 2)), pl.BlockSpec((n_ctx, gw), lambda b: (b, 0)),
                  pl.BlockSpec((n_ctx, kw), lambda b: (b, 0)), pl.BlockSpec((n_ctx, kw), lambda b: (b, 0))],
        out_specs=[pl.BlockSpec((n_ctx, gw), lambda b: (b, 0)), pl.BlockSpec((n_ctx, gw), lambda b: (b, 0))],
        out_shape=[jax.ShapeDtypeStruct((bsz * n_ctx, gw), BF16)] * 2,
        compiler_params=_cparams(("parallel",)),
        name="ctx_attn",
    )(sink, pc16, pc16, pc16, qx, kxd, vxd)


def _conv_kernel(u_ref, b_ref, c_ref, up_ref, cp_ref, un_ref, cn_ref, w_ref, o_ref, *, tiles_per_seq):
    i = pl.program_id(0)
    tm = u_ref.shape[0]
    pos = i % tiles_per_seq
    keep_prev = jnp.where(pos == 0, 0.0, 1.0)
    keep_next = jnp.where(pos == tiles_per_seq - 1, 0.0, 1.0)
    z = c_ref[...] * u_ref[...]
    z_prev = (cp_ref[...] * up_ref[...])[CONV_HALO - 1:CONV_HALO, :] * keep_prev
    z_next = (cn_ref[...] * un_ref[...])[0:1, :] * keep_next
    row = lax.broadcasted_iota(jnp.int32, z.shape, 0)
    zm1 = jnp.where(row == 0, z_prev, pltpu.roll(z, 1, 0))
    zp1 = jnp.where(row == tm - 1, z_next, pltpu.roll(z, tm - 1, 0))
    w = w_ref[...]
    o_ref[...] = (b_ref[...] * (w[0:1] * zm1 + w[1:2] * z + w[2:3] * zp1)).astype(o_ref.dtype)


def _conv_call(p32, conv_w, *, col_u, col_b, col_c, rows_per_seq):
    m = p32.shape[0]
    gw = conv_w.shape[1]
    tm = _pick_tile(rows_per_seq, 512)
    hb = tm // CONV_HALO
    n_halo = m // CONV_HALO
    cur = lambda col: (lambda i: (i, col // gw))
    prv = lambda col: (lambda i: (jnp.maximum(i * hb - 1, 0), col // gw))
    nxt = lambda col: (lambda i: (jnp.minimum((i + 1) * hb, n_halo - 1), col // gw))
    return pl.pallas_call(
        functools.partial(_conv_kernel, tiles_per_seq=rows_per_seq // tm),
        grid=(m // tm,),
        in_specs=[pl.BlockSpec((tm, gw), cur(col_u)), pl.BlockSpec((tm, gw), cur(col_b)), pl.BlockSpec((tm, gw), cur(col_c)),
                  pl.BlockSpec((CONV_HALO, gw), prv(col_u)), pl.BlockSpec((CONV_HALO, gw), prv(col_c)),
                  pl.BlockSpec((CONV_HALO, gw), nxt(col_u)), pl.BlockSpec((CONV_HALO, gw), nxt(col_c)),
                  pl.BlockSpec((SC_KSIZE, gw), lambda i: (0, 0))],
        out_specs=pl.BlockSpec((tm, gw), lambda i: (i, 0)),
        out_shape=jax.ShapeDtypeStruct((m, gw), BF16),
        compiler_params=_cparams(("parallel",)),
        name="short_conv",
    )(p32, p32, p32, p32, p32, p32, p32, conv_w)


def _gla_kernel(*refs, reverse, nt, fuse_out):
    if fuse_out:
        (q_ref, k_ref, v_ref, z_ref, w2_ref, gb_ref, s0_ref, of_ref, g_ref, gain_ref, o_ref, sfin_ref, st_ref) = refs
    else:
        (q_ref, k_ref, v_ref, z_ref, w2_ref, gb_ref, s0_ref, o_ref, sfin_ref, st_ref) = refs
    i = pl.program_id(1)

    @pl.when(i == 0)
    def _():
        st_ref[...] = s0_ref[...]

    tile = q_ref.shape[0]
    c_len = min(GLA_C, tile)
    dk2 = PAIR_W
    dv2 = v_ref.shape[1] // (q_ref.shape[1] // dk2)
    dv = dv2 // 2
    u = jnp.dot(z_ref[...].astype(BF16), w2_ref[...], preferred_element_type=F32) + gb_ref[...]
    la = (jnp.minimum(u, 0.0) - jnp.log1p(jnp.exp(-jnp.abs(u)))) * (1.0 / GLA_TAU)

    r_i = lax.broadcasted_iota(jnp.int32, (c_len, c_len), 0)
    c_i = lax.broadcasted_iota(jnp.int32, (c_len, c_len), 1)
    tri = (r_i <= c_i) if reverse else (r_i >= c_i)
    tri_bf = jnp.where(tri, 1.0, 0.0).astype(BF16)
    r2 = lax.broadcasted_iota(jnp.int32, (2 * c_len, c_len), 0) % c_len
    c2 = lax.broadcasted_iota(jnp.int32, (2 * c_len, c_len), 1)
    tri2 = (r2 <= c2) if reverse else (r2 >= c2)
    bd_r = lax.broadcasted_iota(jnp.int32, (dv2, dk2), 0) // dv
    bd_c = lax.broadcasted_iota(jnp.int32, (dv2, dk2), 1) // HEAD_DIM
    block_diag = bd_r == bd_c
    last_row = 0 if reverse else c_len - 1
    mid_row = c_len // 2

    n_chunks = tile // c_len
    order = range(n_chunks - 1, -1, -1) if reverse else range(n_chunks)
    for c in order:
        rows = slice(c * c_len, (c + 1) * c_len)
        la_c = la[rows]
        la_hi = la_c.astype(BF16)
        la_lo = (la_c - la_hi.astype(F32)).astype(BF16)
        cum = (jnp.dot(tri_bf, la_hi, preferred_element_type=F32)
               + jnp.dot(tri_bf, la_lo, preferred_element_type=F32))
        last = cum[last_row:last_row + 1]
        cmid = cum[mid_row:mid_row + 1]
        qc = q_ref[rows, :] * ATTN_SCALE
        kc = k_ref[rows, :]
        vc = v_ref[rows, :].astype(BF16)
        q_in = (qc * jnp.exp(cum)).astype(BF16)
        q_t = qc * jnp.exp(cum - cmid)
        k_t = (kc * jnp.exp(cmid - cum)).astype(BF16)
        k_p = (kc * jnp.exp(last - cum)).astype(BF16)
        g = jnp.exp(last)
        for p in range(q_ref.shape[1] // dk2):
            ls = slice(p * dk2, (p + 1) * dk2)
            a = lax.dot_general(_stack_pair(q_t[:, ls]), k_t[:, ls], _NT, preferred_element_type=F32)
            a = jnp.where(tri2, a, 0.0).astype(BF16)
            o0 = jnp.dot(a[:c_len], vc[:, p * dv2:p * dv2 + dv], preferred_element_type=F32)
            o1 = jnp.dot(a[c_len:], vc[:, p * dv2 + dv:(p + 1) * dv2], preferred_element_type=F32)
            st = st_ref[p]
            o_int = lax.dot_general(q_in[:, ls], st.astype(BF16), _NT, preferred_element_type=F32)
            o_p = jnp.concatenate([o0, o1], axis=1) + o_int
            upd = lax.dot_general(vc[:, p * dv2:(p + 1) * dv2], k_p[:, ls], _TN, preferred_element_type=F32)
            st_ref[p] = g[:, ls] * st + jnp.where(block_diag, upd, 0.0)
            if not fuse_out:
                o_ref[rows, p * dv2:(p + 1) * dv2] = o_p
            else:
                tot = of_ref[rows, p * dv2:(p + 1) * dv2] + o_p
                for hh in range(2):
                    hs = slice(p * dv2 + hh * dv, p * dv2 + (hh + 1) * dv)
                    oh = tot[:, hh * dv:(hh + 1) * dv]
                    on = oh * lax.rsqrt(jnp.mean(oh * oh, axis=-1, keepdims=True) + LN_EPS) * gain_ref[...]
                    o_ref[rows, hs] = (on * _silu(g_ref[rows, hs])).astype(o_ref.dtype)

    @pl.when(i == nt - 1)
    def _():
        sfin_ref[...] = st_ref[...]


def _gla_call(p, w2pad, gbias, s0, fuse, *, cols, bsz, seq, reverse):
    nq = GLA_HEADS * HEAD_DIM
    nv = s0.shape[1] * s0.shape[2]
    tile = _pick_tile(seq, GLA_TILE)
    nt = seq // tile
    tix = (lambda i: nt - 1 - i) if reverse else (lambda i: i)
    blk = lambda w, col: pl.BlockSpec((tile, w), lambda b, i: (b * nt + tix(i), col // w))
    const2 = lambda b, i: (0, 0)
    st_spec = pl.BlockSpec((None,) + s0.shape[1:], lambda b, i: (b, 0, 0, 0))
    in_specs = [blk(nq, cols['gl_q']), blk(nq, cols['gl_k']), blk(nv, cols['gl_v']), blk(V7X_LANES, cols['gl_z']),
                pl.BlockSpec(w2pad.shape, const2), pl.BlockSpec(gbias.shape, const2), st_spec]
    args = [p, p, p, p, w2pad, gbias, s0]
    if fuse is not None:
        o_other, gain = fuse
        in_specs += [blk(nv, 0), blk(nv, cols['gl_g']), pl.BlockSpec(gain.shape, const2)]
        args += [o_other, p, gain]
    out_dtype = BF16 if fuse is not None else F32
    return pl.pallas_call(
        functools.partial(_gla_kernel, reverse=reverse, nt=nt, fuse_out=fuse is not None),
        grid=(bsz, nt),
        in_specs=in_specs,
        out_specs=[blk(nv, 0), st_spec],
        out_shape=[jax.ShapeDtypeStruct((bsz * seq, nv), out_dtype), jax.ShapeDtypeStruct(s0.shape, F32)],
        scratch_shapes=[pltpu.VMEM(s0.shape[1:], F32)],
        compiler_params=_cparams(("parallel", "arbitrary")),
        name="gla_bwd" if reverse else "gla_fwd",
    )(*args)


def _gla_all(p32, pc32, w2, gb, gain, *, bsz, seq, n_ctx, d):
    cols = _p32_cols(d)
    nq = GLA_HEADS * HEAD_DIM
    dv = d // 4 // GLA_HEADS
    w2pad = [jnp.zeros((V7X_LANES, nq), F32).at[GLA_RANK * k:GLA_RANK * (k + 1)].set(w2[k]).astype(BF16) for k in range(2)]
    gbias = [gb[k].reshape(1, nq) for k in range(2)]
    s0 = jnp.zeros((bsz, GLA_HEADS // 2, 2 * dv, PAIR_W), F32)
    gain2 = gain.reshape(1, dv)
    oc_f, sc_f = _gla_call(pc32, w2pad[0], gbias[0], s0, None, cols=cols, bsz=bsz, seq=n_ctx, reverse=False)
    y_ctx, sc_b = _gla_call(pc32, w2pad[1], gbias[1], s0, (oc_f, gain2), cols=cols, bsz=bsz, seq=n_ctx, reverse=True)
    o_f, _ = _gla_call(p32, w2pad[0], gbias[0], sc_f, None, cols=cols, bsz=bsz, seq=seq, reverse=False)
    y_lat, _ = _gla_call(p32, w2pad[1], gbias[1], sc_b, (o_f, gain2), cols=cols, bsz=bsz, seq=seq, reverse=True)
    return y_lat, y_ctx


def _swiglu(x, wg, wu, wd):
    g = jnp.dot(x, wg, preferred_element_type=F32)
    u = jnp.dot(x, wu, preferred_element_type=F32)
    a = (_silu(g) * u).astype(BF16)
    return jnp.dot(a, wd, preferred_element_type=F32)


def _first_argmax(vals, idx, sentinel):
    m = jnp.max(vals, axis=0, keepdims=True)
    first = jnp.min(jnp.where(vals == m, idx, sentinel), axis=0, keepdims=True)
    return m, first


def _router_kernel(h_ref, wr_ref, rb_ref, su_ref, eidx_ref, wts_ref, rank_ref, cnt_ref, run_ref):
    i = pl.program_id(0)

    @pl.when(i == 0)
    def _():
        run_ref[...] = jnp.zeros_like(run_ref)

    t = h_ref.shape[0]
    gsz = N_EXPERTS // N_GROUPS
    logits = lax.dot_general(wr_ref[...], h_ref[...], _NT, preferred_element_type=F32)
    scores = jax.nn.sigmoid(logits)
    sel = scores + rb_ref[...]
    sub = lax.broadcasted_iota(jnp.int32, (gsz, t), 0)
    gscore = []
    for g in range(N_GROUPS):
        blk = sel[g * gsz:(g + 1) * gsz]
        m1, a1 = _first_argmax(blk, sub, gsz)
        m2 = jnp.max(jnp.where(sub == a1, -jnp.inf, blk), axis=0, keepdims=True)
        gscore.append(m1 + m2)
    gcur = jnp.concatenate(gscore, axis=0)
    gid = lax.broadcasted_iota(jnp.int32, (N_GROUPS, t), 0)
    gkeep = jnp.zeros((N_GROUPS, t), F32)
    for _ in range(TOPK_GROUPS):
        _, a = _first_argmax(gcur, gid, N_GROUPS)
        hit = gid == a
        gkeep = jnp.where(hit, 1.0, gkeep)
        gcur = jnp.where(hit, -jnp.inf, gcur)
    cur = jnp.concatenate(
        [jnp.where(gkeep[g:g + 1] > 0.0, sel[g * gsz:(g + 1) * gsz], -jnp.inf) for g in range(N_GROUPS)], axis=0)
    eid = lax.broadcasted_iota(jnp.int32, (N_EXPERTS, t), 0)
    chosen = jnp.zeros((N_EXPERTS, t), F32)
    hits, picks, wraw = [], [], []
    for _ in range(TOP_K):
        _, a = _first_argmax(cur, eid, N_EXPERTS)
        hit = eid == a
        hits.append(hit)
        picks.append(a)
        wraw.append(jnp.sum(jnp.where(hit, scores, 0.0), axis=0, keepdims=True))
        chosen = jnp.where(hit, 1.0, chosen)
        cur = jnp.where(hit, -jnp.inf, cur)
    wsum = wraw[0]
    for w in wraw[1:]:
        wsum = wsum + w
    eidx_ref[...] = jnp.concatenate(picks, axis=0)
    wts_ref[...] = jnp.concatenate([w / wsum * ROUTED_SCALE for w in wraw], axis=0)
    before = jnp.dot(chosen.astype(BF16), su_ref[...], preferred_element_type=F32) + run_ref[...][:, 0:1]
    rank_ref[...] = jnp.concatenate(
        [jnp.sum(jnp.where(hit, before, 0.0), axis=0, keepdims=True) for hit in hits], axis=0).astype(jnp.int32)
    run_ref[...] = run_ref[...] + jnp.sum(chosen, axis=1, keepdims=True)
    cnt_ref[...] = run_ref[...]


def _router_call(h, w_router, router_bias):
    n, d = h.shape
    t = _pick_tile(n, ROUTER_TILE)
    wr_t = w_router.T.astype(BF16)
    strict_upper = jnp.triu(jnp.ones((t, t), F32), 1).astype(BF16)
    const2 = lambda i: (0, 0)
    tok = lambda i: (0, i)
    eidx, wts, rank, cnt = pl.pallas_call(
        _router_kernel,
        grid=(n // t,),
        in_specs=[pl.BlockSpec((t, d), lambda i: (i, 0)), pl.BlockSpec((N_EXPERTS, d), const2),
                  pl.BlockSpec((N_EXPERTS, 1), const2), pl.BlockSpec((t, t), const2)],
        out_specs=[pl.BlockSpec((TOP_K, t), tok), pl.BlockSpec((TOP_K, t), tok), pl.BlockSpec((TOP_K, t), tok),
                   pl.BlockSpec((N_EXPERTS, V7X_LANES), const2)],
        out_shape=[jax.ShapeDtypeStruct((TOP_K, n), jnp.int32), jax.ShapeDtypeStruct((TOP_K, n), F32),
                   jax.ShapeDtypeStruct((TOP_K, n), jnp.int32), jax.ShapeDtypeStruct((N_EXPERTS, V7X_LANES), F32)],
        scratch_shapes=[pltpu.VMEM((N_EXPERTS, V7X_LANES), F32)],
        compiler_params=_cparams(("arbitrary",)),
        name="router",
    )(h, wr_t, router_bias.reshape(N_EXPERTS, 1).astype(F32), strict_upper)
    return eidx, wts, rank, cnt[:, 0].astype(jnp.int32)


def _dispatch_kernel(zoff_ref, nused_ref, dest_ref, hp_ref, xs_hbm, zbuf, sem, zsem):
    i = pl.program_id(0)
    t = dest_ref.shape[1]
    n_blocks = xs_hbm.shape[0] // MOE_TB

    def clear_block(off):
        return pltpu.make_async_copy(zbuf, xs_hbm.at[pl.ds(pl.multiple_of(off, MOE_TB), MOE_TB)], zsem)

    @pl.when(i == 0)
    def _():
        zbuf[...] = jnp.zeros_like(zbuf)
        for e in range(N_EXPERTS):
            clear_block(zoff_ref[e]).start()
        for e in range(N_EXPERTS):
            clear_block(0).wait()

        def clear_tail(b, carry):
            cp = clear_block(b * MOE_TB)
            cp.start()
            cp.wait()
            return carry

        lax.fori_loop(nused_ref[0], n_blocks, clear_tail, 0)

    def row_copy(j, slot):
        return pltpu.make_async_copy(hp_ref.at[pl.ds(j, 1)], xs_hbm.at[pl.ds(slot, 1)], sem)

    def issue(j, carry):
        for k in range(TOP_K):
            row_copy(j, dest_ref[k, j]).start(priority=k % 2)
        return carry

    def drain(j, carry):
        for k in range(TOP_K):
            row_copy(0, 0).wait()
        return carry

    lax.fori_loop(0, t, issue, 0)
    lax.fori_loop(0, t, drain, 0)


def _dispatch_call(zero_off, n_used, dest, h_packed, n_rows):
    n, c = h_packed.shape
    t = _pick_tile(n, DISPATCH_TILE)
    grid_spec = pltpu.PrefetchScalarGridSpec(
        num_scalar_prefetch=2,
        grid=(n // t,),
        in_specs=[pl.BlockSpec((TOP_K, t), lambda i, z, nu: (0, i), memory_space=pltpu.SMEM),
                  pl.BlockSpec((t, c), lambda i, z, nu: (i, 0))],
        out_specs=pl.BlockSpec(memory_space=pl.ANY),
        scratch_shapes=[pltpu.VMEM((MOE_TB, c), jnp.uint32), pltpu.SemaphoreType.DMA, pltpu.SemaphoreType.DMA],
    )
    return pl.pallas_call(
        _dispatch_kernel,
        grid_spec=grid_spec,
        out_shape=jax.ShapeDtypeStruct((n_rows, c), jnp.uint32),
        compiler_params=_cparams(("arbitrary",)),
        name="moe_dispatch",
    )(zero_off, n_used, dest, h_packed)


def _expert_kernel(be_ref, nused_ref, first_ref, slot_ref, nexte_ref, x_ref, wg_hbm, wu_hbm, wd_hbm, y_ref,
                   wg_f, wu_f, wd_f, wg_s, wu_s, wd_s, sem, *, layer):
    b = pl.program_id(0)

    def weight_copies(e, slot):
        return (pltpu.make_async_copy(wg_hbm.at[layer, e], wg_f.at[slot], sem.at[slot, 0]),
                pltpu.make_async_copy(wu_hbm.at[layer, e], wu_f.at[slot], sem.at[slot, 1]),
                pltpu.make_async_copy(wd_hbm.at[layer, e], wd_f.at[slot], sem.at[slot, 2]))

    @pl.when(b == 0)
    def _():
        for cp in weight_copies(be_ref[0], 0):
            cp.start()

    @pl.when(first_ref[b] == 1)
    def _():
        slot = slot_ref[b]
        for cp in weight_copies(be_ref[b], slot):
            cp.wait()

        @pl.when(nexte_ref[b] >= 0)
        def _():
            for cp in weight_copies(nexte_ref[b], 1 - slot):
                cp.start()

        wg_s[...] = wg_f[slot].astype(BF16)
        wu_s[...] = wu_f[slot].astype(BF16)
        wd_s[...] = wd_f[slot].astype(BF16)

    @pl.when(b < nused_ref[0])
    def _():
        y_ref[...] = _pack_halves(_swiglu(_unpack_halves(x_ref[...]), wg_s[...], wu_s[...], wd_s[...]))

    @pl.when(b >= nused_ref[0])
    def _():
        y_ref[...] = jnp.zeros_like(y_ref)


def _expert_call(block_e, n_used, run_first, run_slot, next_e, x_sorted, wg, wu, wd, layer):
    n_rows, c = x_sorted.shape
    _, _, d, de = wg.shape
    n_blocks = n_rows // MOE_TB
    xsel = lambda b, be, nu, *_: (jnp.minimum(b, nu[0] - 1), 0)
    hbm = pl.BlockSpec(memory_space=pl.ANY)
    grid_spec = pltpu.PrefetchScalarGridSpec(
        num_scalar_prefetch=5,
        grid=(n_blocks,),
        in_specs=[pl.BlockSpec((MOE_TB, c), xsel), hbm, hbm, hbm],
        out_specs=pl.BlockSpec((MOE_TB, c), lambda b, *_: (b, 0)),
        scratch_shapes=[pltpu.VMEM((2, d, de), F32), pltpu.VMEM((2, d, de), F32), pltpu.VMEM((2, de, d), F32),
                        pltpu.VMEM((d, de), BF16), pltpu.VMEM((d, de), BF16), pltpu.VMEM((de, d), BF16),
                        pltpu.SemaphoreType.DMA((2, 3))],
    )
    return pl.pallas_call(
        functools.partial(_expert_kernel, layer=layer),
        grid_spec=grid_spec,
        out_shape=jax.ShapeDtypeStruct((n_rows, c), jnp.uint32),
        compiler_params=_cparams(("arbitrary",)),
        name="experts",
    )(block_e, n_used, run_first, run_slot, next_e, x_sorted, wg, wu, wd)


def _combine_kernel(dest_ref, dnext_ref, w_ref, h_ref, x_ref, g2_ref, wsg_ref, wsu_ref, wsd_ref, lng_ref, lnb_ref,
                    y_hbm, o_ref, ybuf, sem, *, alpha, nsteps):
    i = pl.program_id(0)
    t = h_ref.shape[0]
    slot = i % 2

    def row_copy(src_row, s, k, j):
        return pltpu.make_async_copy(y_hbm.at[pl.ds(src_row, 1)], ybuf.at[s, k, pl.ds(j, 1)], sem.at[s])

    def issue(d_ref, s):
        def body(j, carry):
            for k in range(TOP_K):
                row_copy(d_ref[k, j], s, k, j).start(priority=k % 2)
            return carry
        lax.fori_loop(0, t, body, 0)

    def drain(s):
        def body(j, carry):
            for k in range(TOP_K):
                row_copy(0, s, k, j).wait()
            return carry
        lax.fori_loop(0, t, body, 0)

    @pl.when(i == 0)
    def _():
        issue(dest_ref, 0)

    @pl.when(i + 1 < nsteps)
    def _():
        issue(dnext_ref, 1 - slot)

    acc = _swiglu(h_ref[...], wsg_ref[...], wsu_ref[...], wsd_ref[...])
    drain(slot)
    w = w_ref[...]
    half = acc.shape[1] // 2
    lo, hi = acc[:, :half], acc[:, half:]
    for k in range(TOP_K):
        yk = ybuf[slot, k]
        wk = w[:, k:k + 1]
        lo = lo + wk * pltpu.bitcast(lax.shift_left(yk, jnp.uint32(16)), F32)
        hi = hi + wk * pltpu.bitcast(yk & jnp.uint32(0xFFFF0000), F32)
    acc = jnp.concatenate([lo, hi], axis=1)
    r = alpha * x_ref[...] + g2_ref[0] * acc
    o_ref[...] = _ln_rows(r) * lng_ref[...] + lnb_ref[...]


def _combine_call(dest, wts_t, h, x2d, gate2, wsg, wsu, wsd, ln_g, ln_b, y_sorted, *, row0, rows_per_group, alpha):
    m, d = x2d.shape
    de = wsg.shape[1]
    t = _pick_tile(min(rows_per_group, m), COMBINE_TILE)
    assert row0 % t == 0 and rows_per_group % t == 0
    t0 = row0 // t
    nsteps = m // t
    const2 = lambda i: (0, 0)
    once = pl.Buffered(1)
    return pl.pallas_call(
        functools.partial(_combine_kernel, alpha=alpha, nsteps=nsteps),
        grid=(nsteps,),
        in_specs=[pl.BlockSpec((TOP_K, t), lambda i: (0, t0 + i), memory_space=pltpu.SMEM),
                  pl.BlockSpec((TOP_K, t), lambda i: (0, t0 + jnp.minimum(i + 1, nsteps - 1)), memory_space=pltpu.SMEM),
                  pl.BlockSpec((t, TOP_K), lambda i: (t0 + i, 0)),
                  pl.BlockSpec((t, d), lambda i: (t0 + i, 0)),
                  pl.BlockSpec((t, d), lambda i: (i, 0)),
                  pl.BlockSpec((1, 1, d), lambda i: ((i * t) // rows_per_group, 0, 0)),
                  pl.BlockSpec((d, de), const2, pipeline_mode=once),
                  pl.BlockSpec((d, de), const2, pipeline_mode=once),
                  pl.BlockSpec((de, d), const2, pipeline_mode=once),
                  pl.BlockSpec((1, d), const2), pl.BlockSpec((1, d), const2),
                  pl.BlockSpec(memory_space=pl.ANY)],
        out_specs=pl.BlockSpec((t, d), lambda i: (i, 0)),
        out_shape=jax.ShapeDtypeStruct((m, d), F32),
        scratch_shapes=[pltpu.VMEM((2, TOP_K, t, d // 2), jnp.uint32), pltpu.SemaphoreType.DMA((2,))],
        compiler_params=_cparams(("arbitrary",)),
        name="moe_combine",
    )(dest, dest, wts_t, h, x2d, gate2, wsg, wsu, wsd, ln_g.reshape(1, d), ln_b.reshape(1, d), y_sorted)


def _moe_routed(h, h_packed, w_router, router_bias, w_eg, w_eu, w_ed, layer):
    n = h.shape[0]
    eidx, wts, rank, counts = _router_call(h, w_router, router_bias)
    padded = (counts + MOE_TB - 1) // MOE_TB * MOE_TB
    pad_end = jnp.cumsum(padded)
    pad_start = pad_end - padded
    n_blocks = (n * TOP_K + N_EXPERTS * (MOE_TB - 1) + MOE_TB - 1) // MOE_TB
    block_first = jnp.arange(n_blocks, dtype=jnp.int32) * MOE_TB
    block_e = jnp.minimum(jnp.sum((pad_end[None, :] <= block_first[:, None]).astype(jnp.int32), axis=1), N_EXPERTS - 1)
    n_used = (pad_end[-1:] // MOE_TB).astype(jnp.int32)
    zero_off = jnp.maximum(pad_end - MOE_TB, 0).astype(jnp.int32)
    expert_ids = jnp.arange(N_EXPERTS, dtype=jnp.int32)
    first_row = jnp.sum(jnp.where(eidx[:, :, None] == expert_ids, pad_start.astype(jnp.int32), 0), axis=-1)
    dest = first_row + rank
    x_sorted = _dispatch_call(zero_off, n_used, dest, h_packed, n_blocks * MOE_TB)
    blk = jnp.arange(n_blocks, dtype=jnp.int32)
    used = blk < n_used[0]
    run_first = ((block_e != jnp.concatenate([jnp.full((1,), -1, jnp.int32), block_e[:-1]])) & used).astype(jnp.int32)
    run_slot = ((jnp.cumsum(run_first) - 1) % 2).astype(jnp.int32)
    run_end = jnp.sum(jnp.where(block_e[:, None] == expert_ids, pad_end.astype(jnp.int32), 0), axis=1) // MOE_TB
    next_e = jnp.sum(jnp.where(blk[None, :] == run_end[:, None], block_e[None, :], 0), axis=1)
    next_e = jnp.where(used & (run_end < n_used[0]), next_e, -1).astype(jnp.int32)
    y_sorted = _expert_call(block_e, n_used, run_first, run_slot, next_e, x_sorted, w_eg, w_eu, w_ed, layer)
    return dest, wts.T, y_sorted


def _pack_w_in(w, d):
    gw = d // 4
    cuts = np.cumsum([gw, gw, gw, gw, gw, gw, gw // 2, gw // 2, gw, gw, 2 * GLA_RANK, gw, gw // 4, gw // 4])[:-1].tolist()
    (na_q, na_k, na_v, sc_u, sc_b, sc_c, gl_q, gl_k, gl_v, gl_g, gl_z, sw_q, sw_k, sw_v) = jnp.split(w, cuts, axis=1)
    zpad = jnp.zeros((d, V7X_LANES - 2 * GLA_RANK), w.dtype)
    cols = [na_q, na_k, na_v, sc_u, sc_b, sc_c, gl_v, gl_g, sw_q, gl_q, gl_k, sw_k, sw_v, gl_z, zpad]
    packed = jnp.concatenate(cols, axis=1)
    pad = (-packed.shape[1]) % PROJ_TN
    packed = jnp.pad(packed, ((0, 0), (0, pad)))
    return packed.astype(BF16)


def _p32_cols(d):
    gw = d // 4
    cols, c = {}, 0
    for name, wdt in (('sc_u', gw), ('sc_b', gw), ('sc_c', gw), ('gl_v', gw), ('gl_g', gw), ('sw_q', gw),
                      ('gl_q', gw // 2), ('gl_k', gw // 2), ('sw_k', gw // 4), ('sw_v', gw // 4), ('gl_z', V7X_LANES)):
        cols[name] = c
        c += wdt
    return cols


def _token_mixers(p16, p32, pc16, pc32, rpb, conv_w, gla_w2, gla_b, gla_norm_g, sink, with_ctx_out, *,
                  bsz, seq, n_ctx, d):
    cols = _p32_cols(d)
    kh = min(NA_KH, seq // GRID_W)
    y_na = _na_call(p16, pc16, _na_bias_table(rpb, kh), bsz=bsz, seq=seq, n_ctx=n_ctx)
    conv_cols = dict(col_u=cols['sc_u'], col_b=cols['sc_b'], col_c=cols['sc_c'])
    y_sc = _conv_call(p32, conv_w, rows_per_seq=seq, **conv_cols)
    y_gl, yc_gl = _gla_all(p32, pc32, gla_w2, gla_b, gla_norm_g, bsz=bsz, seq=seq, n_ctx=n_ctx, d=d)
    tables = _rope_tables(seq)
    swa_cols = dict(col_q=cols['sw_q'], col_k=cols['sw_k'], col_v=cols['sw_v'])
    qr, kd, vd = _swa_prep_call(p32, tables, rows_per_seq=seq, rope=True, **swa_cols)
    ctx_tables = tuple(t[:n_ctx] for t in tables)
    qx, kxd, vxd = _swa_prep_call(pc32, ctx_tables, rows_per_seq=n_ctx, rope=False, **swa_cols)
    y_sw = _swa_call(sink, qr, kd, vd, kxd, vxd, bsz=bsz, seq=seq, n_ctx=n_ctx)
    y_lat = (y_na, y_sc, y_gl, y_sw)
    if not with_ctx_out:
        return y_lat, None
    yc_na, yc_sw = _ctx_attn_call(sink, pc16, qx, kxd, vxd, bsz=bsz, n_ctx=n_ctx)
    yc_sc = _conv_call(pc32, conv_w, rows_per_seq=n_ctx, **conv_cols)
    return y_lat, (yc_na, yc_sc, yc_gl, yc_sw)


def kernel(x, c, ctx, c_ctx, w_ada, b_ada, w_in, na_rpb, conv_w, gla_w2, gla_b, gla_norm_g, swa_sink, w_out,
           ln1_g, ln1_b, w_router, router_bias, w_exp_gate, w_exp_up, w_exp_down, w_sh_gate, w_sh_up, w_sh_down,
           ln2_g, ln2_b):
    bsz, seq, d = x.shape
    n_ctx = ctx.shape[1]
    nc = bsz * n_ctx
    depth = w_in.shape[0]
    alpha = (2 * depth) ** 0.25
    n16 = 3 * (d // 4)
    x2 = x.reshape(bsz * seq, d)
    hc2 = ctx.reshape(nc, d)
    c_rows = jnp.zeros((8, d), F32).at[:bsz].set(c).at[bsz].set(c_ctx)
    for layer in range(depth):
        last = layer == depth - 1
        mod = _ada_call(c_rows, w_ada, b_ada[layer], layer)
        sh1, sc1, g1, sh2, sc2, g2 = [t[:bsz, None, :] for t in jnp.split(mod, 6, axis=-1)]
        sh1c, sc1c, g1c, sh2c, sc2c, g2c = [t[bsz:bsz + 1, None, :] for t in jnp.split(mod, 6, axis=-1)]
        w_p = _pack_w_in(w_in[layer], d)
        p16, p32 = _proj_call(x2, sh1, sc1, w_p, rows_per_group=seq, n16=n16)
        pc16, pc32 = _proj_call(hc2, sh1c, sc1c, w_p, rows_per_group=nc, n16=n16)
        y_lat, y_ctx = _token_mixers(p16, p32, pc16, pc32, na_rpb[layer], conv_w[layer], gla_w2[layer], gla_b[layer],
                                     gla_norm_g[layer], swa_sink[layer], not last, bsz=bsz, seq=seq, n_ctx=n_ctx, d=d)
        w_o = w_out[layer].astype(BF16)
        x2, h_lat, hp_lat = _outproj_call(y_lat, x2, g1, w_o, ln1_g[layer], ln1_b[layer], sh2, sc2,
                                          rows_per_group=seq, alpha=alpha)
        shared_w = (w_sh_gate[layer].astype(BF16), w_sh_up[layer].astype(BF16), w_sh_down[layer].astype(BF16))
        route_w = (w_router[layer], router_bias[layer], w_exp_gate, w_exp_up, w_exp_down, layer)
        if last:
            dest, wts_t, y_sorted = _moe_routed(h_lat, hp_lat, *route_w)
            x2 = _combine_call(dest, wts_t, h_lat, x2, g2, *shared_w, ln2_g[layer], ln2_b[layer], y_sorted,
                               row0=0, rows_per_group=seq, alpha=alpha)
        else:
            hc2, h_ctx, hp_ctx = _outproj_call(y_ctx, hc2, g1c, w_o, ln1_g[layer], ln1_b[layer], sh2c, sc2c,
                                               rows_per_group=nc, alpha=alpha)
            h_all = jnp.concatenate([h_ctx, h_lat], axis=0)
            hp_all = jnp.concatenate([hp_ctx, hp_lat], axis=0)
            dest, wts_t, y_sorted = _moe_routed(h_all, hp_all, *route_w)
            hc2 = _combine_call(dest, wts_t, h_all, hc2, g2c, *shared_w, ln2_g[layer], ln2_b[layer], y_sorted,
                                row0=0, rows_per_group=nc, alpha=alpha)
            x2 = _combine_call(dest, wts_t, h_all, x2, g2, *shared_w, ln2_g[layer], ln2_b[layer], y_sorted,
                               row0=nc, rows_per_group=seq, alpha=alpha)
    return x2.reshape(bsz, seq, d)
```

```python
import functools

import jax
import jax.numpy as jnp
import numpy as np
from jax import lax
from jax.experimental import pallas as pl
from jax.experimental.pallas import tpu as pltpu

GRID_W = 64
HEAD_DIM = 64
NA_KH = 8
NA_KW = 16
SC_KSIZE = 3
GLA_HEADS = 4
GLA_RANK = 16
GLA_TAU = 16.0
SWA_KV_HEADS = 2
SWA_BLOCK = 128
ROPE_BASE = 10000.0
N_EXPERTS = 64
TOP_K = 8
N_GROUPS = 8
TOPK_GROUPS = 4
ROUTED_SCALE = 2.5
LN_EPS = 1e-6

V7X_LANES = 128
V7X_VMEM_LIMIT_BYTES = 48 * 1024 * 1024

PROJ_TM = 1024
PROJ_TN = 512
OUT_TM = 512
MOE_TB = 256
ROUTER_TILE = 512
DISPATCH_TILE = 256
COMBINE_TILE = 128
CONV_HALO = 8
GLA_C = 64
GLA_TILE = 512
ADA_TN = 2048

F32 = jnp.float32
BF16 = jnp.bfloat16
PAIR_W = 2 * HEAD_DIM
ATTN_SCALE = HEAD_DIM ** -0.5
_NT = (((1,), (1,)), ((), ()))
_TN = (((0,), (0,)), ((), ()))


def _cparams(sem):
    return pltpu.CompilerParams(dimension_semantics=sem, vmem_limit_bytes=V7X_VMEM_LIMIT_BYTES)


def _pick_tile(m, preferred):
    t = preferred
    while t > 8 and m % t:
        t //= 2
    assert m % t == 0, (m, preferred)
    return t


def _ln_rows(x):
    mu = jnp.mean(x, axis=-1, keepdims=True)
    xc = x - mu
    var = jnp.mean(xc * xc, axis=-1, keepdims=True)
    return xc * lax.rsqrt(var + LN_EPS)


def _silu(x):
    return x * jax.nn.sigmoid(x)


def _ada_kernel(c_ref, w_ref, b_ref, o_ref):
    a = _silu(c_ref[...]).astype(BF16)
    o_ref[...] = jnp.dot(a, w_ref[...].astype(BF16), preferred_element_type=F32) + b_ref[...]


def _ada_call(c_rows, w_ada, b_ada, layer):
    r, d = c_rows.shape
    n = w_ada.shape[2]
    tn = _pick_tile(n, ADA_TN)
    return pl.pallas_call(
        _ada_kernel,
        grid=(n // tn,),
        in_specs=[pl.BlockSpec((r, d), lambda j: (0, 0)), pl.BlockSpec((None, d, tn), lambda j: (layer, 0, j)),
                  pl.BlockSpec((1, tn), lambda j: (0, j))],
        out_specs=pl.BlockSpec((r, tn), lambda j: (0, j)),
        out_shape=jax.ShapeDtypeStruct((r, n), F32),
        compiler_params=_cparams(("parallel",)),
        name="ada_mod",
    )(c_rows, w_ada, b_ada.reshape(1, n))


def _proj_kernel(x_ref, sh_ref, sc_ref, w_ref, o16_ref, o32_ref, xn_ref, *, nb16):
    j = pl.program_id(1)

    @pl.when(j == 0)
    def _():
        y = _ln_rows(x_ref[...]) * (1.0 + sc_ref[0]) + sh_ref[0]
        xn_ref[...] = y.astype(BF16)

    acc = jnp.dot(xn_ref[...], w_ref[...], preferred_element_type=F32)

    @pl.when(j < nb16)
    def _():
        o16_ref[...] = acc.astype(BF16)

    @pl.when(j >= nb16)
    def _():
        o32_ref[...] = acc


def _proj_call(x2d, shift, scale, w_packed, *, rows_per_group, n16):
    m, d = x2d.shape
    ntot = w_packed.shape[1]
    tm = min(PROJ_TM, rows_per_group)
    assert m % tm == 0 and rows_per_group % tm == 0 and ntot % PROJ_TN == 0 and n16 % PROJ_TN == 0
    nb16 = n16 // PROJ_TN
    nb = ntot // PROJ_TN
    grp = lambda i, j: ((i * tm) // rows_per_group, 0, 0)
    return pl.pallas_call(
        functools.partial(_proj_kernel, nb16=nb16),
        grid=(m // tm, nb),
        in_specs=[
            pl.BlockSpec((tm, d), lambda i, j: (i, 0)),
            pl.BlockSpec((1, 1, d), grp),
            pl.BlockSpec((1, 1, d), grp),
            pl.BlockSpec((d, PROJ_TN), lambda i, j: (0, j)),
        ],
        out_specs=[
            pl.BlockSpec((tm, PROJ_TN), lambda i, j: (i, jnp.minimum(j, nb16 - 1))),
            pl.BlockSpec((tm, PROJ_TN), lambda i, j: (i, jnp.maximum(j - nb16, 0))),
        ],
        out_shape=[jax.ShapeDtypeStruct((m, n16), BF16), jax.ShapeDtypeStruct((m, ntot - n16), F32)],
        scratch_shapes=[pltpu.VMEM((tm, d), BF16)],
        compiler_params=_cparams(("parallel", "arbitrary")),
        name="proj",
    )(x2d, shift, scale, w_packed)


def _pack_halves(h):
    c = h.shape[1] // 2
    lo = pltpu.bitcast(h[:, :c].astype(BF16).astype(F32), jnp.uint32)
    hi = pltpu.bitcast(h[:, c:].astype(BF16).astype(F32), jnp.uint32)
    return lax.shift_right_logical(lo, jnp.uint32(16)) | (hi & jnp.uint32(0xFFFF0000))


def _unpack_halves(w):
    lo = pltpu.bitcast(lax.shift_left(w, jnp.uint32(16)), F32)
    hi = pltpu.bitcast(w & jnp.uint32(0xFFFF0000), F32)
    return jnp.concatenate([lo, hi], axis=1).astype(BF16)


def _outproj_kernel(y0_ref, y1_ref, y2_ref, y3_ref, x_ref, g1_ref, w_ref, lng_ref, lnb_ref,
                    sh2_ref, sc2_ref, xo_ref, h_ref, hp_ref, *, alpha):
    gw = y0_ref.shape[1]
    acc = jnp.dot(y0_ref[...], w_ref[0:gw, :], preferred_element_type=F32)
    acc += jnp.dot(y1_ref[...], w_ref[gw:2 * gw, :], preferred_element_type=F32)
    acc += jnp.dot(y2_ref[...], w_ref[2 * gw:3 * gw, :], preferred_element_type=F32)
    acc += jnp.dot(y3_ref[...], w_ref[3 * gw:4 * gw, :], preferred_element_type=F32)
    r = alpha * x_ref[...] + g1_ref[0] * acc
    xn = _ln_rows(r) * lng_ref[...] + lnb_ref[...]
    xo_ref[...] = xn
    h = _ln_rows(xn) * (1.0 + sc2_ref[0]) + sh2_ref[0]
    h_ref[...] = h.astype(BF16)
    hp_ref[...] = _pack_halves(h)


def _outproj_call(ys, x2d, gate1, w_out_bf16, ln_g, ln_b, shift2, scale2, *, rows_per_group, alpha):
    m, d = x2d.shape
    gw = d // 4
    tm = min(OUT_TM, rows_per_group)
    assert m % tm == 0 and rows_per_group % tm == 0
    grp = lambda i: ((i * tm) // rows_per_group, 0, 0)
    row = lambda i: (i, 0)
    const2 = lambda i: (0, 0)
    return pl.pallas_call(
        functools.partial(_outproj_kernel, alpha=alpha),
        grid=(m // tm,),
        in_specs=[pl.BlockSpec((tm, gw), row)] * 4 + [
            pl.BlockSpec((tm, d), row),
            pl.BlockSpec((1, 1, d), grp),
            pl.BlockSpec((d, d), const2, pipeline_mode=pl.Buffered(1)),
            pl.BlockSpec((1, d), const2),
            pl.BlockSpec((1, d), const2),
            pl.BlockSpec((1, 1, d), grp),
            pl.BlockSpec((1, 1, d), grp),
        ],
        out_specs=[pl.BlockSpec((tm, d), row), pl.BlockSpec((tm, d), row), pl.BlockSpec((tm, d // 2), row)],
        out_shape=[jax.ShapeDtypeStruct((m, d), F32), jax.ShapeDtypeStruct((m, d), BF16),
                   jax.ShapeDtypeStruct((m, d // 2), jnp.uint32)],
        compiler_params=_cparams(("parallel",)),
        name="outproj",
    )(*ys, x2d, gate1, w_out_bf16, ln_g.reshape(1, d), ln_b.reshape(1, d), shift2, scale2)


def _stack_pair(q2):
    lane = lax.broadcasted_iota(jnp.int32, q2.shape, 1)
    lo = jnp.where(lane < HEAD_DIM, q2, 0.0)
    hi = jnp.where(lane >= HEAD_DIM, q2, 0.0)
    return jnp.concatenate([lo, hi], axis=0).astype(BF16)


def _unstack_pair(o):
    n = o.shape[0] // 2
    lane = lax.broadcasted_iota(jnp.int32, (n, o.shape[1]), 1)
    return jnp.where(lane < HEAD_DIM, o[:n], o[n:])


def _pair_softmax_av(qs, ks, vs, biases, masks, sink_col):
    ss = []
    for k, bia, msk in zip(ks, biases, masks):
        s = lax.dot_general(qs, k, _NT, preferred_element_type=F32)
        if bia is not None:
            s = s + bia
        if msk is not None:
            s = jnp.where(msk, s, -jnp.inf)
        ss.append(s)
    m = jnp.max(ss[0], axis=-1, keepdims=True)
    for s in ss[1:]:
        m = jnp.maximum(m, jnp.max(s, axis=-1, keepdims=True))
    if sink_col is not None:
        m = jnp.maximum(m, sink_col)
        l = jnp.exp(sink_col - m)
    else:
        l = jnp.zeros_like(m)
    o = None
    for s, v in zip(ss, vs):
        e = jnp.exp(s - m)
        l = l + jnp.sum(e, axis=-1, keepdims=True)
        pv = jnp.dot(e.astype(BF16), v, preferred_element_type=F32)
        o = pv if o is None else o + pv
    return o / l


def _na_kernel(q_ref, k_ref, v_ref, kc_ref, vc_ref, bias_ref, o_ref, *, rows, kh):
    r = pl.program_id(1)
    rs = jnp.clip(r - kh // 2, 0, rows - kh)
    start = pl.multiple_of(rs * GRID_W, GRID_W)
    nwin = kh * GRID_W
    for p in range(q_ref.shape[1] // PAIR_W):
        sl = slice(p * PAIR_W, (p + 1) * PAIR_W)
        qs = _stack_pair(q_ref[:, sl].astype(F32) * ATTN_SCALE)
        kw = k_ref[pl.ds(start, nwin), sl]
        vw = v_ref[pl.ds(start, nwin), sl]
        o = _pair_softmax_av(qs, [kw, kc_ref[:, sl]], [vw, vc_ref[:, sl]], [bias_ref[p], None], [None, None], None)
        o_ref[:, sl] = _unstack_pair(o).astype(o_ref.dtype)


def _na_bias_table(rpb, kh):
    nh = rpb.shape[0]
    c = jnp.arange(GRID_W)
    cstart = jnp.clip(c - NA_KW // 2, 0, GRID_W - NA_KW)
    valid = (c[None, :] >= cstart[:, None]) & (c[None, :] < cstart[:, None] + NA_KW)
    coff = jnp.clip(c[None, :] - c[:, None], 1 - NA_KW, NA_KW - 1) + NA_KW - 1
    roff = jnp.arange(kh)[None, :] - jnp.arange(kh)[:, None] + NA_KH - 1
    pick_r = (roff[:, :, None] == jnp.arange(rpb.shape[1])).astype(F32)
    pick_c = (coff[:, :, None] == jnp.arange(rpb.shape[2])).astype(F32)
    bias = jnp.einsum('hab,dia,ckb->hdick', rpb.astype(F32), pick_r, pick_c, precision=lax.Precision.HIGHEST)
    bias = jnp.where(valid[None, None, None], bias, -jnp.inf)
    bias = bias.transpose(1, 0, 3, 2, 4)
    return bias.reshape(kh, nh // 2, 2 * GRID_W, kh * GRID_W)


def _na_call(p16, pc16, bias_tab, *, bsz, seq, n_ctx):
    gw = p16.shape[1] // 3
    rows = seq // GRID_W
    kh = bias_tab.shape[0]
    delta = lambda b, r: (r - jnp.clip(r - kh // 2, 0, rows - kh), 0, 0, 0)
    once = pl.Buffered(1)
    return pl.pallas_call(
        functools.partial(_na_kernel, rows=rows, kh=kh),
        grid=(bsz, rows),
        in_specs=[
            pl.BlockSpec((GRID_W, gw), lambda b, r: (b * rows + r, 0)),
            pl.BlockSpec((seq, gw), lambda b, r: (b, 1), pipeline_mode=once),
            pl.BlockSpec((seq, gw), lambda b, r: (b, 2), pipeline_mode=once),
            pl.BlockSpec((n_ctx, gw), lambda b, r: (b, 1)),
            pl.BlockSpec((n_ctx, gw), lambda b, r: (b, 2)),
            pl.BlockSpec((None,) + bias_tab.shape[1:], delta),
        ],
        out_specs=pl.BlockSpec((GRID_W, gw), lambda b, r: (b * rows + r, 0)),
        out_shape=jax.ShapeDtypeStruct((bsz * seq, gw), BF16),
        compiler_params=_cparams(("parallel", "arbitrary")),
        name="na_attn",
    )(p16, p16, p16, pc16, pc16, bias_tab)


def _dup_heads(t):
    lane = lax.broadcasted_iota(jnp.int32, t.shape, 1)
    sw = pltpu.roll(t, HEAD_DIM, 1)
    return jnp.concatenate([jnp.where(lane < HEAD_DIM, t, sw), jnp.where(lane < HEAD_DIM, sw, t)], axis=1)


def _rope(t, cos, sa, sb):
    q = HEAD_DIM // 4
    return t * cos + pltpu.roll(t, q, 1) * sa + pltpu.roll(t, V7X_LANES - q, 1) * sb


def _swa_prep_kernel(q_ref, k_ref, v_ref, cos_ref, sa_ref, sb_ref, qo_ref, ko_ref, vo_ref, *, rope):
    k = k_ref[...]
    if rope:
        cos, sa, sb = cos_ref[...], sa_ref[...], sb_ref[...]
        k = _rope(k, cos, sa, sb)
    for p in range(q_ref.shape[1] // PAIR_W):
        sl = slice(p * PAIR_W, (p + 1) * PAIR_W)
        q = q_ref[:, sl]
        if rope:
            q = _rope(q, cos, sa, sb)
        qo_ref[:, sl] = (q * ATTN_SCALE).astype(BF16)
    ko_ref[...] = _dup_heads(k).astype(BF16)
    vo_ref[...] = _dup_heads(v_ref[...]).astype(BF16)


def _rope_tables(seq):
    t = jnp.arange(seq)
    row = (t // GRID_W).astype(F32)
    col = (t % GRID_W).astype(F32)
    quarter = HEAD_DIM // 4
    inv = ROPE_BASE ** (-2.0 * jnp.arange(quarter, dtype=F32) / (HEAD_DIM // 2))
    ang_r = row[:, None] * inv[None, :]
    ang_c = col[:, None] * inv[None, :]
    zero = jnp.zeros_like(ang_r)
    cos_h = jnp.concatenate([jnp.cos(ang_r)] * 2 + [jnp.cos(ang_c)] * 2, axis=1)
    sa_h = jnp.concatenate([zero, jnp.sin(ang_r), zero, jnp.sin(ang_c)], axis=1)
    sb_h = jnp.concatenate([-jnp.sin(ang_r), zero, -jnp.sin(ang_c), zero], axis=1)
    two = lambda a: jnp.concatenate([a, a], axis=1)
    return two(cos_h), two(sa_h), two(sb_h)


def _swa_prep_call(p32, tables, *, col_q, col_k, col_v, rows_per_seq, rope):
    m = p32.shape[0]
    gw = 4 * PAIR_W
    tm = _pick_tile(rows_per_seq, 512)
    nseq_tiles = rows_per_seq // tm
    tab = lambda i: (i % nseq_tiles, 0)
    return pl.pallas_call(
        functools.partial(_swa_prep_kernel, rope=rope),
        grid=(m // tm,),
        in_specs=[
            pl.BlockSpec((tm, gw), lambda i: (i, col_q // gw)),
            pl.BlockSpec((tm, PAIR_W), lambda i: (i, col_k // PAIR_W)),
            pl.BlockSpec((tm, PAIR_W), lambda i: (i, col_v // PAIR_W)),
            pl.BlockSpec((tm, PAIR_W), tab),
            pl.BlockSpec((tm, PAIR_W), tab),
            pl.BlockSpec((tm, PAIR_W), tab),
        ],
        out_specs=[pl.BlockSpec((tm, gw), lambda i: (i, 0)), pl.BlockSpec((tm, 2 * PAIR_W), lambda i: (i, 0)),
                   pl.BlockSpec((tm, 2 * PAIR_W), lambda i: (i, 0))],
        out_shape=[jax.ShapeDtypeStruct((m, gw), BF16), jax.ShapeDtypeStruct((m, 2 * PAIR_W), BF16),
                   jax.ShapeDtypeStruct((m, 2 * PAIR_W), BF16)],
        compiler_params=_cparams(("parallel",)),
        name="swa_prep",
    )(p32, p32, p32, *tables)


def _sink_col(sink_ref, p, n):
    row = lax.broadcasted_iota(jnp.int32, (2 * n, 1), 0)
    return jnp.where(row < n, sink_ref[2 * p], sink_ref[2 * p + 1])


def _swa_kernel(sink_ref, q_ref, kp_ref, kc_ref, kn_ref, vp_ref, vc_ref, vn_ref, kx_ref, vx_ref, o_ref, *, nblk):
    n = pl.program_id(1)
    blk = q_ref.shape[0]
    qi = lax.broadcasted_iota(jnp.int32, (2 * blk, blk), 0) % blk
    kj = lax.broadcasted_iota(jnp.int32, (2 * blk, blk), 1)
    m_prev = kj >= qi + jnp.where(n > 0, 0, blk)
    m_next = kj <= qi - jnp.where(n < nblk - 1, 0, blk)
    npairs = q_ref.shape[1] // PAIR_W
    for p in range(npairs):
        sl = slice(p * PAIR_W, (p + 1) * PAIR_W)
        g = p // (npairs // SWA_KV_HEADS)
        gs = slice(g * PAIR_W, (g + 1) * PAIR_W)
        qs = _stack_pair(q_ref[:, sl].astype(F32))
        o = _pair_softmax_av(
            qs, [kp_ref[:, gs], kc_ref[:, gs], kn_ref[:, gs], kx_ref[:, gs]],
            [vp_ref[:, gs], vc_ref[:, gs], vn_ref[:, gs], vx_ref[:, gs]],
            [None] * 4, [m_prev, None, m_next, None], _sink_col(sink_ref, p, blk))
        o_ref[:, sl] = _unstack_pair(o).astype(o_ref.dtype)


def _swa_call(sink, qr, kd, vd, kxd, vxd, *, bsz, seq, n_ctx):
    gw = qr.shape[1]
    kw = kd.shape[1]
    nblk = seq // SWA_BLOCK
    cur = lambda b, n: (b * nblk + n, 0)
    prev = lambda b, n: (b * nblk + jnp.maximum(n - 1, 0), 0)
    nxt = lambda b, n: (b * nblk + jnp.minimum(n + 1, nblk - 1), 0)
    cx = lambda b, n: (b, 0)
    return pl.pallas_call(
        functools.partial(_swa_kernel, nblk=nblk),
        grid=(bsz, nblk),
        in_specs=[pl.BlockSpec(memory_space=pltpu.SMEM),
                  pl.BlockSpec((SWA_BLOCK, gw), cur),
                  pl.BlockSpec((SWA_BLOCK, kw), prev), pl.BlockSpec((SWA_BLOCK, kw), cur), pl.BlockSpec((SWA_BLOCK, kw), nxt),
                  pl.BlockSpec((SWA_BLOCK, kw), prev), pl.BlockSpec((SWA_BLOCK, kw), cur), pl.BlockSpec((SWA_BLOCK, kw), nxt),
                  pl.BlockSpec((n_ctx, kw), cx), pl.BlockSpec((n_ctx, kw), cx)],
        out_specs=pl.BlockSpec((SWA_BLOCK, gw), cur),
        out_shape=jax.ShapeDtypeStruct((bsz * seq, gw), BF16),
        compiler_params=_cparams(("parallel", "arbitrary")),
        name="swa_attn",
    )(sink, qr, kd, kd, kd, vd, vd, vd, kxd, vxd)


def _ctx_attn_kernel(sink_ref, qa_ref, ka_ref, va_ref, qd_ref, kd_ref, vd_ref, oa_ref, od_ref):
    n = qa_ref.shape[0]
    npairs = qa_ref.shape[1] // PAIR_W
    for p in range(npairs):
        sl = slice(p * PAIR_W, (p + 1) * PAIR_W)
        qs = _stack_pair(qa_ref[:, sl].astype(F32) * ATTN_SCALE)
        o = _pair_softmax_av(qs, [ka_ref[:, sl]], [va_ref[:, sl]], [None], [None], None)
        oa_ref[:, sl] = _unstack_pair(o).astype(oa_ref.dtype)
    for p in range(npairs):
        sl = slice(p * PAIR_W, (p + 1) * PAIR_W)
        g = p // (npairs // SWA_KV_HEADS)
        gs = slice(g * PAIR_W, (g + 1) * PAIR_W)
        qs = _stack_pair(qd_ref[:, sl].astype(F32))
        o = _pair_softmax_av(qs, [kd_ref[:, gs]], [vd_ref[:, gs]], [None], [None], _sink_col(sink_ref, p, n))
        od_ref[:, sl] = _unstack_pair(o).astype(od_ref.dtype)


def _ctx_attn_call(sink, pc16, qx, kxd, vxd, *, bsz, n_ctx):
    gw = qx.shape[1]
    kw = kxd.shape[1]
    return pl.pallas_call(
        _ctx_attn_kernel,
        grid=(bsz,),
        in_specs=[pl.BlockSpec(memory_space=pltpu.SMEM),
                  pl.BlockSpec((n_ctx, gw), lambda b: (b, 0)), pl.BlockSpec((n_ctx, gw), lambda b: (b, 1)),
                  pl.BlockSpec((n_ctx, gw), lambda b: (b, 2)), pl.BlockSpec((n_ctx, gw), lambda b: (b, 0)),
                  pl.BlockSpec((n_ctx, kw), lambda b: (b, 0)), pl.BlockSpec((n_ctx, kw), lambda b: (b, 0))],
        out_specs=[pl.BlockSpec((n_ctx, gw), lambda b: (b, 0)), pl.BlockSpec((n_ctx, gw), lambda b: (b, 0))],
        out_shape=[jax.ShapeDtypeStruct((bsz * n_ctx, gw), BF16)] * 2,
        compiler_params=_cparams(("parallel",)),
        name="ctx_attn",
    )(sink, pc16, pc16, pc16, qx, kxd, vxd)


def _conv_kernel(u_ref, b_ref, c_ref, up_ref, cp_ref, un_ref, cn_ref, w_ref, o_ref, *, tiles_per_seq):
    i = pl.program_id(0)
    tm = u_ref.shape[0]
    pos = i % tiles_per_seq
    keep_prev = jnp.where(pos == 0, 0.0, 1.0)
    keep_next = jnp.where(pos == tiles_per_seq - 1, 0.0, 1.0)
    z = c_ref[...] * u_ref[...]
    z_prev = (cp_ref[...] * up_ref[...])[CONV_HALO - 1:CONV_HALO, :] * keep_prev
    z_next = (cn_ref[...] * un_ref[...])[0:1, :] * keep_next
    row = lax.broadcasted_iota(jnp.int32, z.shape, 0)
    zm1 = jnp.where(row == 0, z_prev, pltpu.roll(z, 1, 0))
    zp1 = jnp.where(row == tm - 1, z_next, pltpu.roll(z, tm - 1, 0))
    w = w_ref[...]
    o_ref[...] = (b_ref[...] * (w[0:1] * zm1 + w[1:2] * z + w[2:3] * zp1)).astype(o_ref.dtype)


def _conv_call(p32, conv_w, *, col_u, col_b, col_c, rows_per_seq):
    m = p32.shape[0]
    gw = conv_w.shape[1]
    tm = _pick_tile(rows_per_seq, 512)
    hb = tm // CONV_HALO
    n_halo = m // CONV_HALO
    cur = lambda col: (lambda i: (i, col // gw))
    prv = lambda col: (lambda i: (jnp.maximum(i * hb - 1, 0), col // gw))
    nxt = lambda col: (lambda i: (jnp.minimum((i + 1) * hb, n_halo - 1), col // gw))
    return pl.pallas_call(
        functools.partial(_conv_kernel, tiles_per_seq=rows_per_seq // tm),
        grid=(m // tm,),
        in_specs=[pl.BlockSpec((tm, gw), cur(col_u)), pl.BlockSpec((tm, gw), cur(col_b)), pl.BlockSpec((tm, gw), cur(col_c)),
                  pl.BlockSpec((CONV_HALO, gw), prv(col_u)), pl.BlockSpec((CONV_HALO, gw), prv(col_c)),
                  pl.BlockSpec((CONV_HALO, gw), nxt(col_u)), pl.BlockSpec((CONV_HALO, gw), nxt(col_c)),
                  pl.BlockSpec((SC_KSIZE, gw), lambda i: (0, 0))],
        out_specs=pl.BlockSpec((tm, gw), lambda i: (i, 0)),
        out_shape=jax.ShapeDtypeStruct((m, gw), BF16),
        compiler_params=_cparams(("parallel",)),
        name="short_conv",
    )(p32, p32, p32, p32, p32, p32, p32, conv_w)


def _gla_kernel(*refs, reverse, nt, fuse_out):
    if fuse_out:
        (q_ref, k_ref, v_ref, z_ref, w2_ref, gb_ref, s0_ref, of_ref, g_ref, gain_ref, o_ref, sfin_ref, st_ref) = refs
    else:
        (q_ref, k_ref, v_ref, z_ref, w2_ref, gb_ref, s0_ref, o_ref, sfin_ref, st_ref) = refs
    i = pl.program_id(1)

    @pl.when(i == 0)
    def _():
        st_ref[...] = s0_ref[...]

    tile = q_ref.shape[0]
    c_len = min(GLA_C, tile)
    dk2 = PAIR_W
    dv2 = v_ref.shape[1] // (q_ref.shape[1] // dk2)
    dv = dv2 // 2
    u = jnp.dot(z_ref[...].astype(BF16), w2_ref[...], preferred_element_type=F32) + gb_ref[...]
    la = (jnp.minimum(u, 0.0) - jnp.log1p(jnp.exp(-jnp.abs(u)))) * (1.0 / GLA_TAU)

    r_i = lax.broadcasted_iota(jnp.int32, (c_len, c_len), 0)
    c_i = lax.broadcasted_iota(jnp.int32, (c_len, c_len), 1)
    tri = (r_i <= c_i) if reverse else (r_i >= c_i)
    tri_bf = jnp.where(tri, 1.0, 0.0).astype(BF16)
    r2 = lax.broadcasted_iota(jnp.int32, (2 * c_len, c_len), 0) % c_len
    c2 = lax.broadcasted_iota(jnp.int32, (2 * c_len, c_len), 1)
    tri2 = (r2 <= c2) if reverse else (r2 >= c2)
    bd_r = lax.broadcasted_iota(jnp.int32, (dv2, dk2), 0) // dv
    bd_c = lax.broadcasted_iota(jnp.int32, (dv2, dk2), 1) // HEAD_DIM
    block_diag = bd_r == bd_c
    last_row = 0 if reverse else c_len - 1
    mid_row = c_len // 2

    n_chunks = tile // c_len
    order = range(n_chunks - 1, -1, -1) if reverse else range(n_chunks)
    for c in order:
        rows = slice(c * c_len, (c + 1) * c_len)
        la_c = la[rows]
        la_hi = la_c.astype(BF16)
        la_lo = (la_c - la_hi.astype(F32)).astype(BF16)
        cum = (jnp.dot(tri_bf, la_hi, preferred_element_type=F32)
               + jnp.dot(tri_bf, la_lo, preferred_element_type=F32))
        last = cum[last_row:last_row + 1]
        cmid = cum[mid_row:mid_row + 1]
        qc = q_ref[rows, :] * ATTN_SCALE
        kc = k_ref[rows, :]
        vc = v_ref[rows, :].astype(BF16)
        q_in = (qc * jnp.exp(cum)).astype(BF16)
        q_t = qc * jnp.exp(cum - cmid)
        k_t = (kc * jnp.exp(cmid - cum)).astype(BF16)
        k_p = (kc * jnp.exp(last - cum)).astype(BF16)
        g = jnp.exp(last)
        for p in range(q_ref.shape[1] // dk2):
            ls = slice(p * dk2, (p + 1) * dk2)
            a = lax.dot_general(_stack_pair(q_t[:, ls]), k_t[:, ls], _NT, preferred_element_type=F32)
            a = jnp.where(tri2, a, 0.0).astype(BF16)
            o0 = jnp.dot(a[:c_len], vc[:, p * dv2:p * dv2 + dv], preferred_element_type=F32)
            o1 = jnp.dot(a[c_len:], vc[:, p * dv2 + dv:(p + 1) * dv2], preferred_element_type=F32)
            st = st_ref[p]
            o_int = lax.dot_general(q_in[:, ls], st.astype(BF16), _NT, preferred_element_type=F32)
            o_p = jnp.concatenate([o0, o1], axis=1) + o_int
            upd = lax.dot_general(vc[:, p * dv2:(p + 1) * dv2], k_p[:, ls], _TN, preferred_element_type=F32)
            st_ref[p] = g[:, ls] * st + jnp.where(block_diag, upd, 0.0)
            if not fuse_out:
                o_ref[rows, p * dv2:(p + 1) * dv2] = o_p
            else:
                tot = of_ref[rows, p * dv2:(p + 1) * dv2] + o_p
                for hh in range(2):
                    hs = slice(p * dv2 + hh * dv, p * dv2 + (hh + 1) * dv)
                    oh = tot[:, hh * dv:(hh + 1) * dv]
                    on = oh * lax.rsqrt(jnp.mean(oh * oh, axis=-1, keepdims=True) + LN_EPS) * gain_ref[...]
                    o_ref[rows, hs] = (on * _silu(g_ref[rows, hs])).astype(o_ref.dtype)

    @pl.when(i == nt - 1)
    def _():
        sfin_ref[...] = st_ref[...]


def _gla_call(p, w2pad, gbias, s0, fuse, *, cols, bsz, seq, reverse):
    nq = GLA_HEADS * HEAD_DIM
    nv = s0.shape[1] * s0.shape[2]
    tile = _pick_tile(seq, GLA_TILE)
    nt = seq // tile
    tix = (lambda i: nt - 1 - i) if reverse else (lambda i: i)
    blk = lambda w, col: pl.BlockSpec((tile, w), lambda b, i: (b * nt + tix(i), col // w))
    const2 = lambda b, i: (0, 0)
    st_spec = pl.BlockSpec((None,) + s0.shape[1:], lambda b, i: (b, 0, 0, 0))
    in_specs = [blk(nq, cols['gl_q']), blk(nq, cols['gl_k']), blk(nv, cols['gl_v']), blk(V7X_LANES, cols['gl_z']),
                pl.BlockSpec(w2pad.shape, const2), pl.BlockSpec(gbias.shape, const2), st_spec]
    args = [p, p, p, p, w2pad, gbias, s0]
    if fuse is not None:
        o_other, gain = fuse
        in_specs += [blk(nv, 0), blk(nv, cols['gl_g']), pl.BlockSpec(gain.shape, const2)]
        args += [o_other, p, gain]
    out_dtype = BF16 if fuse is not None else F32
    return pl.pallas_call(
        functools.partial(_gla_kernel, reverse=reverse, nt=nt, fuse_out=fuse is not None),
        grid=(bsz, nt),
        in_specs=in_specs,
        out_specs=[blk(nv, 0), st_spec],
        out_shape=[jax.ShapeDtypeStruct((bsz * seq, nv), out_dtype), jax.ShapeDtypeStruct(s0.shape, F32)],
        scratch_shapes=[pltpu.VMEM(s0.shape[1:], F32)],
        compiler_params=_cparams(("parallel", "arbitrary")),
        name="gla_bwd" if reverse else "gla_fwd",
    )(*args)


def _gla_all(p32, pc32, w2, gb, gain, *, bsz, seq, n_ctx, d):
    cols = _p32_cols(d)
    nq = GLA_HEADS * HEAD_DIM
    dv = d // 4 // GLA_HEADS
    w2pad = [jnp.zeros((V7X_LANES, nq), F32).at[GLA_RANK * k:GLA_RANK * (k + 1)].set(w2[k]).astype(BF16) for k in range(2)]
    gbias = [gb[k].reshape(1, nq) for k in range(2)]
    s0 = jnp.zeros((bsz, GLA_HEADS // 2, 2 * dv, PAIR_W), F32)
    gain2 = gain.reshape(1, dv)
    oc_f, sc_f = _gla_call(pc32, w2pad[0], gbias[0], s0, None, cols=cols, bsz=bsz, seq=n_ctx, reverse=False)
    y_ctx, sc_b = _gla_call(pc32, w2pad[1], gbias[1], s0, (oc_f, gain2), cols=cols, bsz=bsz, seq=n_ctx, reverse=True)
    o_f, _ = _gla_call(p32, w2pad[0], gbias[0], sc_f, None, cols=cols, bsz=bsz, seq=seq, reverse=False)
    y_lat, _ = _gla_call(p32, w2pad[1], gbias[1], sc_b, (o_f, gain2), cols=cols, bsz=bsz, seq=seq, reverse=True)
    return y_lat, y_ctx


def _swiglu(x, wg, wu, wd):
    g = jnp.dot(x, wg, preferred_element_type=F32)
    u = jnp.dot(x, wu, preferred_element_type=F32)
    a = (_silu(g) * u).astype(BF16)
    return jnp.dot(a, wd, preferred_element_type=F32)


def _first_argmax(vals, idx, sentinel):
    m = jnp.max(vals, axis=0, keepdims=True)
    first = jnp.min(jnp.where(vals == m, idx, sentinel), axis=0, keepdims=True)
    return m, first


def _router_kernel(h_ref, wr_ref, rb_ref, su_ref, eidx_ref, wts_ref, rank_ref, cnt_ref, run_ref):
    i = pl.program_id(0)

    @pl.when(i == 0)
    def _():
        run_ref[...] = jnp.zeros_like(run_ref)

    t = h_ref.shape[0]
    gsz = N_EXPERTS // N_GROUPS
    logits = lax.dot_general(wr_ref[...], h_ref[...], _NT, preferred_element_type=F32)
    scores = jax.nn.sigmoid(logits)
    sel = scores + rb_ref[...]
    sub = lax.broadcasted_iota(jnp.int32, (gsz, t), 0)
    gscore = []
    for g in range(N_GROUPS):
        blk = sel[g * gsz:(g + 1) * gsz]
        m1, a1 = _first_argmax(blk, sub, gsz)
        m2 = jnp.max(jnp.where(sub == a1, -jnp.inf, blk), axis=0, keepdims=True)
        gscore.append(m1 + m2)
    gcur = jnp.concatenate(gscore, axis=0)
    gid = lax.broadcasted_iota(jnp.int32, (N_GROUPS, t), 0)
    gkeep = jnp.zeros((N_GROUPS, t), F32)
    for _ in range(TOPK_GROUPS):
        _, a = _first_argmax(gcur, gid, N_GROUPS)
        hit = gid == a
        gkeep = jnp.where(hit, 1.0, gkeep)
        gcur = jnp.where(hit, -jnp.inf, gcur)
    cur = jnp.concatenate(
        [jnp.where(gkeep[g:g + 1] > 0.0, sel[g * gsz:(g + 1) * gsz], -jnp.inf) for g in range(N_GROUPS)], axis=0)
    eid = lax.broadcasted_iota(jnp.int32, (N_EXPERTS, t), 0)
    chosen = jnp.zeros((N_EXPERTS, t), F32)
    hits, picks, wraw = [], [], []
    for _ in range(TOP_K):
        _, a = _first_argmax(cur, eid, N_EXPERTS)
        hit = eid == a
        hits.append(hit)
        picks.append(a)
        wraw.append(jnp.sum(jnp.where(hit, scores, 0.0), axis=0, keepdims=True))
        chosen = jnp.where(hit, 1.0, chosen)
        cur = jnp.where(hit, -jnp.inf, cur)
    wsum = wraw[0]
    for w in wraw[1:]:
        wsum = wsum + w
    eidx_ref[...] = jnp.concatenate(picks, axis=0)
    wts_ref[...] = jnp.concatenate([w / wsum * ROUTED_SCALE for w in wraw], axis=0)
    before = jnp.dot(chosen.astype(BF16), su_ref[...], preferred_element_type=F32) + run_ref[...][:, 0:1]
    rank_ref[...] = jnp.concatenate(
        [jnp.sum(jnp.where(hit, before, 0.0), axis=0, keepdims=True) for hit in hits], axis=0).astype(jnp.int32)
    run_ref[...] = run_ref[...] + jnp.sum(chosen, axis=1, keepdims=True)
    cnt_ref[...] = run_ref[...]


def _router_call(h, w_router, router_bias):
    n, d = h.shape
    t = _pick_tile(n, ROUTER_TILE)
    wr_t = w_router.T.astype(BF16)
    strict_upper = jnp.triu(jnp.ones((t, t), F32), 1).astype(BF16)
    const2 = lambda i: (0, 0)
    tok = lambda i: (0, i)
    eidx, wts, rank, cnt = pl.pallas_call(
        _router_kernel,
        grid=(n // t,),
        in_specs=[pl.BlockSpec((t, d), lambda i: (i, 0)), pl.BlockSpec((N_EXPERTS, d), const2),
                  pl.BlockSpec((N_EXPERTS, 1), const2), pl.BlockSpec((t, t), const2)],
        out_specs=[pl.BlockSpec((TOP_K, t), tok), pl.BlockSpec((TOP_K, t), tok), pl.BlockSpec((TOP_K, t), tok),
                   pl.BlockSpec((N_EXPERTS, V7X_LANES), const2)],
        out_shape=[jax.ShapeDtypeStruct((TOP_K, n), jnp.int32), jax.ShapeDtypeStruct((TOP_K, n), F32),
                   jax.ShapeDtypeStruct((TOP_K, n), jnp.int32), jax.ShapeDtypeStruct((N_EXPERTS, V7X_LANES), F32)],
        scratch_shapes=[pltpu.VMEM((N_EXPERTS, V7X_LANES), F32)],
        compiler_params=_cparams(("arbitrary",)),
        name="router",
    )(h, wr_t, router_bias.reshape(N_EXPERTS, 1).astype(F32), strict_upper)
    return eidx, wts, rank, cnt[:, 0].astype(jnp.int32)


def _dispatch_kernel(zoff_ref, nused_ref, dest_ref, hp_ref, xs_hbm, zbuf, sem, zsem):
    i = pl.program_id(0)
    t = dest_ref.shape[1]
    n_blocks = xs_hbm.shape[0] // MOE_TB

    def clear_block(off):
        return pltpu.make_async_copy(zbuf, xs_hbm.at[pl.ds(pl.multiple_of(off, MOE_TB), MOE_TB)], zsem)

    @pl.when(i == 0)
    def _():
        zbuf[...] = jnp.zeros_like(zbuf)
        for e in range(N_EXPERTS):
            clear_block(zoff_ref[e]).start()
        for e in range(N_EXPERTS):
            clear_block(0).wait()

        def clear_tail(b, carry):
            cp = clear_block(b * MOE_TB)
            cp.start()
            cp.wait()
            return carry

        lax.fori_loop(nused_ref[0], n_blocks, clear_tail, 0)

    def row_copy(j, slot):
        return pltpu.make_async_copy(hp_ref.at[pl.ds(j, 1)], xs_hbm.at[pl.ds(slot, 1)], sem)

    def issue(j, carry):
        for k in range(TOP_K):
            row_copy(j, dest_ref[k, j]).start(priority=k % 2)
        return carry

    def drain(j, carry):
        for k in range(TOP_K):
            row_copy(0, 0).wait()
        return carry

    lax.fori_loop(0, t, issue, 0)
    lax.fori_loop(0, t, drain, 0)


def _dispatch_call(zero_off, n_used, dest, h_packed, n_rows):
    n, c = h_packed.shape
    t = _pick_tile(n, DISPATCH_TILE)
    grid_spec = pltpu.PrefetchScalarGridSpec(
        num_scalar_prefetch=2,
        grid=(n // t,),
        in_specs=[pl.BlockSpec((TOP_K, t), lambda i, z, nu: (0, i), memory_space=pltpu.SMEM),
                  pl.BlockSpec((t, c), lambda i, z, nu: (i, 0))],
        out_specs=pl.BlockSpec(memory_space=pl.ANY),
        scratch_shapes=[pltpu.VMEM((MOE_TB, c), jnp.uint32), pltpu.SemaphoreType.DMA, pltpu.SemaphoreType.DMA],
    )
    return pl.pallas_call(
        _dispatch_kernel,
        grid_spec=grid_spec,
        out_shape=jax.ShapeDtypeStruct((n_rows, c), jnp.uint32),
        compiler_params=_cparams(("arbitrary",)),
        name="moe_dispatch",
    )(zero_off, n_used, dest, h_packed)


def _expert_kernel(be_ref, nused_ref, first_ref, slot_ref, nexte_ref, x_ref, oprev_ref, wg_hbm, wu_hbm, wd_hbm,
                   y_out, wg_f, wu_f, wd_f, wg_s, wu_s, wd_s, sem, ybuf, ssem, *, layer, dump_row):
    b = pl.program_id(0)

    def weight_copies(e, slot):
        return (pltpu.make_async_copy(wg_hbm.at[layer, e], wg_f.at[slot], sem.at[slot, 0]),
                pltpu.make_async_copy(wu_hbm.at[layer, e], wu_f.at[slot], sem.at[slot, 1]),
                pltpu.make_async_copy(wd_hbm.at[layer, e], wd_f.at[slot], sem.at[slot, 2]))

    @pl.when(b == 0)
    def _():
        for cp in weight_copies(be_ref[0], 0):
            cp.start()

    @pl.when(first_ref[b] == 1)
    def _():
        slot = slot_ref[b]
        for cp in weight_copies(be_ref[b], slot):
            cp.wait()

        @pl.when(nexte_ref[b] >= 0)
        def _():
            for cp in weight_copies(nexte_ref[b], 1 - slot):
                cp.start()

        wg_s[...] = wg_f[slot].astype(BF16)
        wu_s[...] = wu_f[slot].astype(BF16)
        wd_s[...] = wd_f[slot].astype(BF16)

    nu = nused_ref[0]
    tb = x_ref.shape[0]

    def scatter_copy(src_slot, r, dst_row):
        return pltpu.make_async_copy(ybuf.at[src_slot, pl.ds(r, 1)], y_out.at[pl.ds(dst_row, 1)], ssem.at[src_slot])

    def wait_scatter(src_slot):
        for r in range(tb):
            scatter_copy(src_slot, r, 0).wait()

    @pl.when(b == 0)
    def _():
        ybuf[...] = jnp.zeros_like(ybuf)

    @pl.when((b >= 1) & (b <= nu))
    def _():
        wait_scatter(b % 2)

    @pl.when(b < nu)
    def _():
        prev_slot = (b + 1) % 2

        def start_rows(lo, hi):
            for r in range(lo, hi):
                dst = jnp.where(b == 0, dump_row, oprev_ref[0, 0, r])
                scatter_copy(prev_slot, r, dst).start(priority=r % 2)

        n_pieces = 6
        cuts = [tb * p // n_pieces for p in range(n_pieces + 1)]
        x = _unpack_halves(x_ref[...])
        half_e = wg_s.shape[1] // 2
        acts = []
        for p in range(2):
            start_rows(cuts[p], cuts[p + 1])
            cs = slice(p * half_e, (p + 1) * half_e)
            g = jnp.dot(x, wg_s[:, cs], preferred_element_type=F32)
            u = jnp.dot(x, wu_s[:, cs], preferred_element_type=F32)
            acts.append((_silu(g) * u).astype(BF16))
        a = jnp.concatenate(acts, axis=1)
        quarter = wd_s.shape[1] // 4
        ys = []
        for p in range(4):
            start_rows(cuts[2 + p], cuts[3 + p])
            ys.append(jnp.dot(a, wd_s[:, p * quarter:(p + 1) * quarter], preferred_element_type=F32))
        ybuf[b % 2] = _pack_halves(jnp.concatenate(ys, axis=1))

    @pl.when(b == nu)
    def _():
        last_slot = (b + 1) % 2
        for r in range(tb):
            scatter_copy(last_slot, r, oprev_ref[0, 0, r]).start(priority=r % 2)
        wait_scatter(last_slot)


def _expert_call(block_e, n_used, run_first, run_slot, next_e, x_sorted, out_rows, n_out_rows, wg, wu, wd, layer):
    n_rows, c = x_sorted.shape
    _, _, d, de = wg.shape
    n_blocks = n_rows // MOE_TB
    xsel = lambda b, be, nu, *_: (jnp.minimum(b, nu[0] - 1), 0)
    osel = lambda b, *_: (jnp.clip(b - 1, 0, n_blocks - 1), 0, 0)
    hbm = pl.BlockSpec(memory_space=pl.ANY)
    grid_spec = pltpu.PrefetchScalarGridSpec(
        num_scalar_prefetch=5,
        grid=(n_blocks + 1,),
        in_specs=[pl.BlockSpec((MOE_TB, c), xsel), pl.BlockSpec((1, 1, MOE_TB), osel, memory_space=pltpu.SMEM),
                  hbm, hbm, hbm],
        out_specs=hbm,
        scratch_shapes=[pltpu.VMEM((2, d, de), F32), pltpu.VMEM((2, d, de), F32), pltpu.VMEM((2, de, d), F32),
                        pltpu.VMEM((d, de), BF16), pltpu.VMEM((d, de), BF16), pltpu.VMEM((de, d), BF16),
                        pltpu.SemaphoreType.DMA((2, 3)),
                        pltpu.VMEM((2, MOE_TB, c), jnp.uint32), pltpu.SemaphoreType.DMA((2,))],
    )
    return pl.pallas_call(
        functools.partial(_expert_kernel, layer=layer, dump_row=n_out_rows - 8),
        grid_spec=grid_spec,
        out_shape=jax.ShapeDtypeStruct((n_out_rows, c), jnp.uint32),
        compiler_params=_cparams(("arbitrary",)),
        name="experts",
    )(*(jnp.pad(a, (0, 1)) for a in (block_e,)), n_used, *(jnp.pad(a, (0, 1)) for a in (run_first, run_slot, next_e)),
      x_sorted, out_rows, wg, wu, wd)


def _combine_kernel(*refs, alpha):
    (w_ref, h_ref, x_ref, g2_ref, wsg_ref, wsu_ref, wsd_ref, lng_ref, lnb_ref), y_refs, o_ref = (
        refs[:9], refs[9:9 + TOP_K], refs[9 + TOP_K])
    acc = _swiglu(h_ref[...], wsg_ref[...], wsu_ref[...], wsd_ref[...])
    w = w_ref[...]
    half = acc.shape[1] // 2
    lo, hi = acc[:, :half], acc[:, half:]
    for k in range(TOP_K):
        yk = y_refs[k][...]
        wk = w[:, k:k + 1]
        lo = lo + wk * pltpu.bitcast(lax.shift_left(yk, jnp.uint32(16)), F32)
        hi = hi + wk * pltpu.bitcast(yk & jnp.uint32(0xFFFF0000), F32)
    acc = jnp.concatenate([lo, hi], axis=1)
    r = alpha * x_ref[...] + g2_ref[0] * acc
    o_ref[...] = _ln_rows(r) * lng_ref[...] + lnb_ref[...]


def _combine_call(wts_t, h, x2d, gate2, wsg, wsu, wsd, ln_g, ln_b, y_rows, *, row0, rows_per_group, alpha):
    m, d = x2d.shape
    n = h.shape[0]
    de = wsg.shape[1]
    t = _pick_tile(min(rows_per_group, m), COMBINE_TILE)
    assert row0 % t == 0 and rows_per_group % t == 0 and n % t == 0
    t0 = row0 // t
    const2 = lambda i: (0, 0)
    once = pl.Buffered(1)
    y_specs = [pl.BlockSpec((t, d // 2), functools.partial(lambda i, k: (k * (n // t) + t0 + i, 0), k=k))
               for k in range(TOP_K)]
    return pl.pallas_call(
        functools.partial(_combine_kernel, alpha=alpha),
        grid=(m // t,),
        in_specs=[pl.BlockSpec((t, TOP_K), lambda i: (t0 + i, 0)),
                  pl.BlockSpec((t, d), lambda i: (t0 + i, 0)),
                  pl.BlockSpec((t, d), lambda i: (i, 0)),
                  pl.BlockSpec((1, 1, d), lambda i: ((i * t) // rows_per_group, 0, 0)),
                  pl.BlockSpec((d, de), const2, pipeline_mode=once),
                  pl.BlockSpec((d, de), const2, pipeline_mode=once),
                  pl.BlockSpec((de, d), const2, pipeline_mode=once),
                  pl.BlockSpec((1, d), const2), pl.BlockSpec((1, d), const2)] + y_specs,
        out_specs=pl.BlockSpec((t, d), lambda i: (i, 0)),
        out_shape=jax.ShapeDtypeStruct((m, d), F32),
        compiler_params=_cparams(("parallel",)),
        name="moe_combine",
    )(wts_t, h, x2d, gate2, wsg, wsu, wsd, ln_g.reshape(1, d), ln_b.reshape(1, d), *([y_rows] * TOP_K))


def _moe_routed(h, h_packed, w_router, router_bias, w_eg, w_eu, w_ed, layer):
    n = h.shape[0]
    eidx, wts, rank, counts = _router_call(h, w_router, router_bias)
    padded = (counts + MOE_TB - 1) // MOE_TB * MOE_TB
    pad_end = jnp.cumsum(padded)
    pad_start = pad_end - padded
    n_blocks = (n * TOP_K + N_EXPERTS * (MOE_TB - 1) + MOE_TB - 1) // MOE_TB
    block_first = jnp.arange(n_blocks, dtype=jnp.int32) * MOE_TB
    block_e = jnp.minimum(jnp.sum((pad_end[None, :] <= block_first[:, None]).astype(jnp.int32), axis=1), N_EXPERTS - 1)
    n_used = (pad_end[-1:] // MOE_TB).astype(jnp.int32)
    zero_off = jnp.maximum(pad_end - MOE_TB, 0).astype(jnp.int32)
    expert_ids = jnp.arange(N_EXPERTS, dtype=jnp.int32)
    first_row = jnp.sum(jnp.where(eidx[:, :, None] == expert_ids, pad_start.astype(jnp.int32), 0), axis=-1)
    dest = first_row + rank
    x_sorted = _dispatch_call(zero_off, n_used, dest, h_packed, n_blocks * MOE_TB)
    blk = jnp.arange(n_blocks, dtype=jnp.int32)
    used = blk < n_used[0]
    run_first = ((block_e != jnp.concatenate([jnp.full((1,), -1, jnp.int32), block_e[:-1]])) & used).astype(jnp.int32)
    run_slot = ((jnp.cumsum(run_first) - 1) % 2).astype(jnp.int32)
    run_end = jnp.sum(jnp.where(block_e[:, None] == expert_ids, pad_end.astype(jnp.int32), 0), axis=1) // MOE_TB
    next_e = jnp.sum(jnp.where(blk[None, :] == run_end[:, None], block_e[None, :], 0), axis=1)
    next_e = jnp.where(used & (run_end < n_used[0]), next_e, -1).astype(jnp.int32)
    nk = n * TOP_K
    n_rows = n_blocks * MOE_TB
    n_pad = n_rows - nk
    pad_need = (padded - counts).astype(jnp.int32)
    need_end = jnp.cumsum(pad_need)
    j = jnp.arange(n_pad, dtype=jnp.int32)
    owner = jnp.sum((need_end[None, :] <= j[:, None]).astype(jnp.int32), axis=1)
    sel = owner[:, None] == expert_ids
    pick = lambda tab: jnp.sum(jnp.where(sel, tab.astype(jnp.int32), 0), axis=1)
    pad_slot = jnp.where(owner < N_EXPERTS,
                         pick(pad_start) + pick(counts) + j - (pick(need_end) - pick(pad_need)),
                         pad_end[-1].astype(jnp.int32) + j - need_end[-1])
    slots = jnp.concatenate([dest.reshape(-1), pad_slot])
    rows = jnp.concatenate([jnp.arange(nk, dtype=jnp.int32), jnp.full((n_pad,), nk, jnp.int32)])
    _, out_rows = lax.sort((slots, rows), num_keys=1)
    y_rows = _expert_call(block_e, n_used, run_first, run_slot, next_e, x_sorted,
                          out_rows.reshape(n_blocks, 1, MOE_TB), nk + 8, w_eg, w_eu, w_ed, layer)
    return wts.T, y_rows


def _pack_w_in(w, d):
    gw = d // 4
    cuts = np.cumsum([gw, gw, gw, gw, gw, gw, gw // 2, gw // 2, gw, gw, 2 * GLA_RANK, gw, gw // 4, gw // 4])[:-1].tolist()
    (na_q, na_k, na_v, sc_u, sc_b, sc_c, gl_q, gl_k, gl_v, gl_g, gl_z, sw_q, sw_k, sw_v) = jnp.split(w, cuts, axis=1)
    zpad = jnp.zeros((d, V7X_LANES - 2 * GLA_RANK), w.dtype)
    cols = [na_q, na_k, na_v, sc_u, sc_b, sc_c, gl_v, gl_g, sw_q, gl_q, gl_k, sw_k, sw_v, gl_z, zpad]
    packed = jnp.concatenate(cols, axis=1)
    pad = (-packed.shape[1]) % PROJ_TN
    packed = jnp.pad(packed, ((0, 0), (0, pad)))
    return packed.astype(BF16)


def _p32_cols(d):
    gw = d // 4
    cols, c = {}, 0
    for name, wdt in (('sc_u', gw), ('sc_b', gw), ('sc_c', gw), ('gl_v', gw), ('gl_g', gw), ('sw_q', gw),
                      ('gl_q', gw // 2), ('gl_k', gw // 2), ('sw_k', gw // 4), ('sw_v', gw // 4), ('gl_z', V7X_LANES)):
        cols[name] = c
        c += wdt
    return cols


def _token_mixers(p16, p32, pc16, pc32, rpb, conv_w, gla_w2, gla_b, gla_norm_g, sink, with_ctx_out, *,
                  bsz, seq, n_ctx, d):
    cols = _p32_cols(d)
    kh = min(NA_KH, seq // GRID_W)
    y_na = _na_call(p16, pc16, _na_bias_table(rpb, kh), bsz=bsz, seq=seq, n_ctx=n_ctx)
    conv_cols = dict(col_u=cols['sc_u'], col_b=cols['sc_b'], col_c=cols['sc_c'])
    y_sc = _conv_call(p32, conv_w, rows_per_seq=seq, **conv_cols)
    y_gl, yc_gl = _gla_all(p32, pc32, gla_w2, gla_b, gla_norm_g, bsz=bsz, seq=seq, n_ctx=n_ctx, d=d)
    tables = _rope_tables(seq)
    swa_cols = dict(col_q=cols['sw_q'], col_k=cols['sw_k'], col_v=cols['sw_v'])
    qr, kd, vd = _swa_prep_call(p32, tables, rows_per_seq=seq, rope=True, **swa_cols)
    ctx_tables = tuple(t[:n_ctx] for t in tables)
    qx, kxd, vxd = _swa_prep_call(pc32, ctx_tables, rows_per_seq=n_ctx, rope=False, **swa_cols)
    y_sw = _swa_call(sink, qr, kd, vd, kxd, vxd, bsz=bsz, seq=seq, n_ctx=n_ctx)
    y_lat = (y_na, y_sc, y_gl, y_sw)
    if not with_ctx_out:
        return y_lat, None
    yc_na, yc_sw = _ctx_attn_call(sink, pc16, qx, kxd, vxd, bsz=bsz, n_ctx=n_ctx)
    yc_sc = _conv_call(pc32, conv_w, rows_per_seq=n_ctx, **conv_cols)
    return y_lat, (yc_na, yc_sc, yc_gl, yc_sw)


def kernel(x, c, ctx, c_ctx, w_ada, b_ada, w_in, na_rpb, conv_w, gla_w2, gla_b, gla_norm_g, swa_sink, w_out,
           ln1_g, ln1_b, w_router, router_bias, w_exp_gate, w_exp_up, w_exp_down, w_sh_gate, w_sh_up, w_sh_down,
           ln2_g, ln2_b):
    bsz, seq, d = x.shape
    n_ctx = ctx.shape[1]
    nc = bsz * n_ctx
    depth = w_in.shape[0]
    alpha = (2 * depth) ** 0.25
    n16 = 3 * (d // 4)
    x2 = x.reshape(bsz * seq, d)
    hc2 = ctx.reshape(nc, d)
    c_rows = jnp.zeros((8, d), F32).at[:bsz].set(c).at[bsz].set(c_ctx)
    for layer in range(depth):
        last = layer == depth - 1
        mod = _ada_call(c_rows, w_ada, b_ada[layer], layer)
        sh1, sc1, g1, sh2, sc2, g2 = [t[:bsz, None, :] for t in jnp.split(mod, 6, axis=-1)]
        sh1c, sc1c, g1c, sh2c, sc2c, g2c = [t[bsz:bsz + 1, None, :] for t in jnp.split(mod, 6, axis=-1)]
        w_p = _pack_w_in(w_in[layer], d)
        p16, p32 = _proj_call(x2, sh1, sc1, w_p, rows_per_group=seq, n16=n16)
        pc16, pc32 = _proj_call(hc2, sh1c, sc1c, w_p, rows_per_group=nc, n16=n16)
        y_lat, y_ctx = _token_mixers(p16, p32, pc16, pc32, na_rpb[layer], conv_w[layer], gla_w2[layer], gla_b[layer],
                                     gla_norm_g[layer], swa_sink[layer], not last, bsz=bsz, seq=seq, n_ctx=n_ctx, d=d)
        w_o = w_out[layer].astype(BF16)
        x2, h_lat, hp_lat = _outproj_call(y_lat, x2, g1, w_o, ln1_g[layer], ln1_b[layer], sh2, sc2,
                                          rows_per_group=seq, alpha=alpha)
        shared_w = (w_sh_gate[layer].astype(BF16), w_sh_up[layer].astype(BF16), w_sh_down[layer].astype(BF16))
        route_w = (w_router[layer], router_bias[layer], w_exp_gate, w_exp_up, w_exp_down, layer)
        if last:
            wts_t, y_rows = _moe_routed(h_lat, hp_lat, *route_w)
            x2 = _combine_call(wts_t, h_lat, x2, g2, *shared_w, ln2_g[layer], ln2_b[layer], y_rows,
                               row0=0, rows_per_group=seq, alpha=alpha)
        else:
            hc2, h_ctx, hp_ctx = _outproj_call(y_ctx, hc2, g1c, w_o, ln1_g[layer], ln1_b[layer], sh2c, sc2c,
                                               rows_per_group=nc, alpha=alpha)
            h_all = jnp.concatenate([h_ctx, h_lat], axis=0)
            hp_all = jnp.concatenate([hp_ctx, hp_lat], axis=0)
            wts_t, y_rows = _moe_routed(h_all, hp_all, *route_w)
            hc2 = _combine_call(wts_t, h_all, hc2, g2c, *shared_w, ln2_g[layer], ln2_b[layer], y_rows,
                                row0=0, rows_per_group=nc, alpha=alpha)
            x2 = _combine_call(wts_t, h_all, x2, g2, *shared_w, ln2_g[layer], ln2_b[layer], y_rows,
                               row0=nc, rows_per_group=seq, alpha=alpha)
    return x2.reshape(bsz, seq, d)
```

```python
import functools

import jax
import jax.numpy as jnp
import numpy as np
from jax import lax
from jax.experimental import pallas as pl
from jax.experimental.pallas import tpu as pltpu

GRID_W = 64
HEAD_DIM = 64
NA_KH = 8
NA_KW = 16
SC_KSIZE = 3
GLA_HEADS = 4
GLA_RANK = 16
GLA_TAU = 16.0
SWA_KV_HEADS = 2
SWA_BLOCK = 128
ROPE_BASE = 10000.0
N_EXPERTS = 64
TOP_K = 8
N_GROUPS = 8
TOPK_GROUPS = 4
ROUTED_SCALE = 2.5
LN_EPS = 1e-6

V7X_LANES = 128
V7X_SUBLANES = 8
V7X_VMEM_LIMIT_BYTES = 48 * 1024 * 1024

PROJ_TM = 1024
PROJ_TN = 512
OUT_TM = 512
MOE_TB = 256
ROUTER_TILE = 512
DISPATCH_TILE = 256
COMBINE_TILE = 256
ELEMWISE_TM = 512
NA_ROWS_PER_STEP = 4
CONV_HALO = 8
GLA_C = 64
GLA_TILE = 512
ADA_TN = 2048

F32 = jnp.float32
BF16 = jnp.bfloat16
PAIR_W = 2 * HEAD_DIM
ATTN_SCALE = HEAD_DIM ** -0.5
_NT = (((1,), (1,)), ((), ()))
_TN = (((0,), (0,)), ((), ()))


def _cparams(sem):
    return pltpu.CompilerParams(dimension_semantics=sem, vmem_limit_bytes=V7X_VMEM_LIMIT_BYTES)


def _pick_tile(m, preferred):
    t = preferred
    while t > 8 and m % t:
        t //= 2
    assert m % t == 0, (m, preferred)
    return t


def _ln_rows(x):
    mu = jnp.mean(x, axis=-1, keepdims=True)
    xc = x - mu
    var = jnp.mean(xc * xc, axis=-1, keepdims=True)
    return xc * lax.rsqrt(var + LN_EPS)


def _silu(x):
    return x * jax.nn.sigmoid(x)


def _ada_kernel(c_ref, w_ref, b_ref, o_ref):
    a = _silu(c_ref[...]).astype(BF16)
    o_ref[...] = jnp.dot(a, w_ref[...].astype(BF16), preferred_element_type=F32) + b_ref[...]


def _ada_call(c_rows, w_ada, b_ada, layer):
    r, d = c_rows.shape
    n = w_ada.shape[2]
    tn = _pick_tile(n, ADA_TN)
    return pl.pallas_call(
        _ada_kernel,
        grid=(n // tn,),
        in_specs=[pl.BlockSpec((r, d), lambda j: (0, 0)), pl.BlockSpec((None, d, tn), lambda j: (layer, 0, j)),
                  pl.BlockSpec((1, tn), lambda j: (0, j))],
        out_specs=pl.BlockSpec((r, tn), lambda j: (0, j)),
        out_shape=jax.ShapeDtypeStruct((r, n), F32),
        compiler_params=_cparams(("parallel",)),
        name="ada_mod",
    )(c_rows, w_ada, b_ada.reshape(1, n))


def _proj_kernel(x_ref, sh_ref, sc_ref, w_ref, o16_ref, o32_ref, xn_ref, *, nb16):
    j = pl.program_id(1)

    @pl.when(j == 0)
    def _():
        y = _ln_rows(x_ref[...]) * (1.0 + sc_ref[0]) + sh_ref[0]
        xn_ref[...] = y.astype(BF16)

    acc = jnp.dot(xn_ref[...], w_ref[...], preferred_element_type=F32)

    @pl.when(j < nb16)
    def _():
        o16_ref[...] = acc.astype(BF16)

    @pl.when(j >= nb16)
    def _():
        o32_ref[...] = acc


def _proj_call(x2d, shift, scale, w_packed, *, rows_per_group, n16):
    m, d = x2d.shape
    ntot = w_packed.shape[1]
    tm = min(PROJ_TM, rows_per_group)
    assert m % tm == 0 and rows_per_group % tm == 0 and ntot % PROJ_TN == 0 and n16 % PROJ_TN == 0
    nb16 = n16 // PROJ_TN
    nb = ntot // PROJ_TN
    grp = lambda i, j: ((i * tm) // rows_per_group, 0, 0)
    return pl.pallas_call(
        functools.partial(_proj_kernel, nb16=nb16),
        grid=(m // tm, nb),
        in_specs=[
            pl.BlockSpec((tm, d), lambda i, j: (i, 0)),
            pl.BlockSpec((1, 1, d), grp),
            pl.BlockSpec((1, 1, d), grp),
            pl.BlockSpec((d, PROJ_TN), lambda i, j: (0, j)),
        ],
        out_specs=[
            pl.BlockSpec((tm, PROJ_TN), lambda i, j: (i, jnp.minimum(j, nb16 - 1))),
            pl.BlockSpec((tm, PROJ_TN), lambda i, j: (i, jnp.maximum(j - nb16, 0))),
        ],
        out_shape=[jax.ShapeDtypeStruct((m, n16), BF16), jax.ShapeDtypeStruct((m, ntot - n16), F32)],
        scratch_shapes=[pltpu.VMEM((tm, d), BF16)],
        compiler_params=_cparams(("parallel", "arbitrary")),
        name="proj",
    )(x2d, shift, scale, w_packed)


def _pack_halves(h):
    c = h.shape[1] // 2
    lo = pltpu.bitcast(h[:, :c].astype(BF16).astype(F32), jnp.uint32)
    hi = pltpu.bitcast(h[:, c:].astype(BF16).astype(F32), jnp.uint32)
    return lax.shift_right_logical(lo, jnp.uint32(16)) | (hi & jnp.uint32(0xFFFF0000))


def _unpack_halves(w):
    lo = pltpu.bitcast(lax.shift_left(w, jnp.uint32(16)), F32)
    hi = pltpu.bitcast(w & jnp.uint32(0xFFFF0000), F32)
    return jnp.concatenate([lo, hi], axis=1).astype(BF16)


def _outproj_kernel(y0_ref, y1_ref, y2_ref, y3_ref, x_ref, g1_ref, w_ref, lng_ref, lnb_ref,
                    sh2_ref, sc2_ref, xo_ref, h_ref, hp_ref, *, alpha):
    gw = y0_ref.shape[1]
    acc = jnp.dot(y0_ref[...], w_ref[0:gw, :], preferred_element_type=F32)
    acc += jnp.dot(y1_ref[...], w_ref[gw:2 * gw, :], preferred_element_type=F32)
    acc += jnp.dot(y2_ref[...], w_ref[2 * gw:3 * gw, :], preferred_element_type=F32)
    acc += jnp.dot(y3_ref[...], w_ref[3 * gw:4 * gw, :], preferred_element_type=F32)
    r = alpha * x_ref[...] + g1_ref[0] * acc
    xn = _ln_rows(r) * lng_ref[...] + lnb_ref[...]
    xo_ref[...] = xn
    h = _ln_rows(xn) * (1.0 + sc2_ref[0]) + sh2_ref[0]
    h_ref[...] = h.astype(BF16)
    hp_ref[...] = _pack_halves(h)


def _outproj_call(ys, x2d, gate1, w_out_bf16, ln_g, ln_b, shift2, scale2, *, rows_per_group, alpha):
    m, d = x2d.shape
    gw = d // 4
    tm = min(OUT_TM, rows_per_group)
    assert m % tm == 0 and rows_per_group % tm == 0
    grp = lambda i: ((i * tm) // rows_per_group, 0, 0)
    row = lambda i: (i, 0)
    const2 = lambda i: (0, 0)
    return pl.pallas_call(
        functools.partial(_outproj_kernel, alpha=alpha),
        grid=(m // tm,),
        in_specs=[pl.BlockSpec((tm, gw), row)] * 4 + [
            pl.BlockSpec((tm, d), row),
            pl.BlockSpec((1, 1, d), grp),
            pl.BlockSpec((d, d), const2, pipeline_mode=pl.Buffered(1)),
            pl.BlockSpec((1, d), const2),
            pl.BlockSpec((1, d), const2),
            pl.BlockSpec((1, 1, d), grp),
            pl.BlockSpec((1, 1, d), grp),
        ],
        out_specs=[pl.BlockSpec((tm, d), row), pl.BlockSpec((tm, d), row), pl.BlockSpec((tm, d // 2), row)],
        out_shape=[jax.ShapeDtypeStruct((m, d), F32), jax.ShapeDtypeStruct((m, d), BF16),
                   jax.ShapeDtypeStruct((m, d // 2), jnp.uint32)],
        compiler_params=_cparams(("parallel",)),
        name="outproj",
    )(*ys, x2d, gate1, w_out_bf16, ln_g.reshape(1, d), ln_b.reshape(1, d), shift2, scale2)


def _stack_pair(q2):
    lane = lax.broadcasted_iota(jnp.int32, q2.shape, 1)
    lo = jnp.where(lane < HEAD_DIM, q2, 0.0)
    hi = jnp.where(lane >= HEAD_DIM, q2, 0.0)
    return jnp.concatenate([lo, hi], axis=0).astype(BF16)


def _unstack_pair(o):
    n = o.shape[0] // 2
    lane = lax.broadcasted_iota(jnp.int32, (n, o.shape[1]), 1)
    return jnp.where(lane < HEAD_DIM, o[:n], o[n:])


def _pair_softmax_av(qs, ks, vs, biases, masks, sink_col):
    ss = []
    for k, bia, msk in zip(ks, biases, masks):
        s = lax.dot_general(qs, k, _NT, preferred_element_type=F32)
        if bia is not None:
            s = s + bia
        if msk is not None:
            s = jnp.where(msk, s, -jnp.inf)
        ss.append(s)
    m = jnp.max(ss[0], axis=-1, keepdims=True)
    for s in ss[1:]:
        m = jnp.maximum(m, jnp.max(s, axis=-1, keepdims=True))
    if sink_col is not None:
        m = jnp.maximum(m, sink_col)
        l = jnp.exp(sink_col - m)
    else:
        l = jnp.zeros_like(m)
    o = None
    for s, v in zip(ss, vs):
        e = jnp.exp(s - m)
        l = l + jnp.sum(e, axis=-1, keepdims=True)
        pv = jnp.dot(e.astype(BF16), v, preferred_element_type=F32)
        o = pv if o is None else o + pv
    return o / l


def _na_kernel(q_ref, k_ref, v_ref, kc_ref, vc_ref, *rest, rows, kh):
    bias_refs, o_ref = rest[:-1], rest[-1]
    nwin = kh * GRID_W
    for i, bias_ref in enumerate(bias_refs):
        r = pl.program_id(1) * len(bias_refs) + i
        rs = jnp.clip(r - kh // 2, 0, rows - kh)
        start = pl.multiple_of(rs * GRID_W, GRID_W)
        qrows = slice(i * GRID_W, (i + 1) * GRID_W)
        for p in range(q_ref.shape[1] // PAIR_W):
            sl = slice(p * PAIR_W, (p + 1) * PAIR_W)
            qs = _stack_pair(q_ref[qrows, sl].astype(F32) * ATTN_SCALE)
            kw = k_ref[pl.ds(start, nwin), sl]
            vw = v_ref[pl.ds(start, nwin), sl]
            o = _pair_softmax_av(qs, [kw, kc_ref[:, sl]], [vw, vc_ref[:, sl]], [bias_ref[p], None], [None, None], None)
            o_ref[qrows, sl] = _unstack_pair(o).astype(o_ref.dtype)


def _na_bias_table(rpb, kh):
    nh = rpb.shape[0]
    c = jnp.arange(GRID_W)
    cstart = jnp.clip(c - NA_KW // 2, 0, GRID_W - NA_KW)
    valid = (c[None, :] >= cstart[:, None]) & (c[None, :] < cstart[:, None] + NA_KW)
    coff = jnp.clip(c[None, :] - c[:, None], 1 - NA_KW, NA_KW - 1) + NA_KW - 1
    roff = jnp.arange(kh)[None, :] - jnp.arange(kh)[:, None] + NA_KH - 1
    pick_r = (roff[:, :, None] == jnp.arange(rpb.shape[1])).astype(F32)
    pick_c = (coff[:, :, None] == jnp.arange(rpb.shape[2])).astype(F32)
    bias = jnp.einsum('hab,dia,ckb->hdick', rpb.astype(F32), pick_r, pick_c, precision=lax.Precision.HIGHEST)
    bias = jnp.where(valid[None, None, None], bias, -jnp.inf)
    bias = bias.transpose(1, 0, 3, 2, 4)
    return bias.reshape(kh, nh // 2, 2 * GRID_W, kh * GRID_W)


def _na_call(p16, pc16, bias_tab, *, bsz, seq, n_ctx):
    gw = p16.shape[1] // 3
    rows = seq // GRID_W
    kh = bias_tab.shape[0]
    nr = NA_ROWS_PER_STEP if rows % NA_ROWS_PER_STEP == 0 else 1
    steps = rows // nr

    def delta(i):
        def index_map(b, s):
            r = s * nr + i
            return (r - jnp.clip(r - kh // 2, 0, rows - kh), 0, 0, 0)
        return index_map

    once = pl.Buffered(1)
    return pl.pallas_call(
        functools.partial(_na_kernel, rows=rows, kh=kh),
        grid=(bsz, steps),
        in_specs=[
            pl.BlockSpec((nr * GRID_W, gw), lambda b, s: (b * steps + s, 0)),
            pl.BlockSpec((seq, gw), lambda b, s: (b, 1), pipeline_mode=once),
            pl.BlockSpec((seq, gw), lambda b, s: (b, 2), pipeline_mode=once),
            pl.BlockSpec((n_ctx, gw), lambda b, s: (b, 1)),
            pl.BlockSpec((n_ctx, gw), lambda b, s: (b, 2)),
        ] + [pl.BlockSpec((None,) + bias_tab.shape[1:], delta(i)) for i in range(nr)],
        out_specs=pl.BlockSpec((nr * GRID_W, gw), lambda b, s: (b * steps + s, 0)),
        out_shape=jax.ShapeDtypeStruct((bsz * seq, gw), BF16),
        compiler_params=_cparams(("parallel", "arbitrary")),
        name="na_attn",
    )(p16, p16, p16, pc16, pc16, *([bias_tab] * nr))


def _dup_heads(t):
    lane = lax.broadcasted_iota(jnp.int32, t.shape, 1)
    sw = pltpu.roll(t, HEAD_DIM, 1)
    return jnp.concatenate([jnp.where(lane < HEAD_DIM, t, sw), jnp.where(lane < HEAD_DIM, sw, t)], axis=1)


def _rope(t, cos, sa, sb):
    q = HEAD_DIM // 4
    return t * cos + pltpu.roll(t, q, 1) * sa + pltpu.roll(t, V7X_LANES - q, 1) * sb


def _swa_prep_kernel(q_ref, k_ref, v_ref, cos_ref, sa_ref, sb_ref, qo_ref, ko_ref, vo_ref, *, rope):
    k = k_ref[...]
    if rope:
        cos, sa, sb = cos_ref[...], sa_ref[...], sb_ref[...]
        k = _rope(k, cos, sa, sb)
    for p in range(q_ref.shape[1] // PAIR_W):
        sl = slice(p * PAIR_W, (p + 1) * PAIR_W)
        q = q_ref[:, sl]
        if rope:
            q = _rope(q, cos, sa, sb)
        qo_ref[:, sl] = (q * ATTN_SCALE).astype(BF16)
    ko_ref[...] = _dup_heads(k).astype(BF16)
    vo_ref[...] = _dup_heads(v_ref[...]).astype(BF16)


def _rope_tables(seq):
    t = jnp.arange(seq)
    row = (t // GRID_W).astype(F32)
    col = (t % GRID_W).astype(F32)
    quarter = HEAD_DIM // 4
    inv = ROPE_BASE ** (-2.0 * jnp.arange(quarter, dtype=F32) / (HEAD_DIM // 2))
    ang_r = row[:, None] * inv[None, :]
    ang_c = col[:, None] * inv[None, :]
    zero = jnp.zeros_like(ang_r)
    cos_h = jnp.concatenate([jnp.cos(ang_r)] * 2 + [jnp.cos(ang_c)] * 2, axis=1)
    sa_h = jnp.concatenate([zero, jnp.sin(ang_r), zero, jnp.sin(ang_c)], axis=1)
    sb_h = jnp.concatenate([-jnp.sin(ang_r), zero, -jnp.sin(ang_c), zero], axis=1)
    two = lambda a: jnp.concatenate([a, a], axis=1)
    return two(cos_h), two(sa_h), two(sb_h)


def _swa_prep_call(p32, tables, *, col_q, col_k, col_v, rows_per_seq, rope):
    m = p32.shape[0]
    gw = 4 * PAIR_W
    tm = _pick_tile(rows_per_seq, ELEMWISE_TM)
    nseq_tiles = rows_per_seq // tm
    tab = lambda i: (i % nseq_tiles, 0)
    return pl.pallas_call(
        functools.partial(_swa_prep_kernel, rope=rope),
        grid=(m // tm,),
        in_specs=[
            pl.BlockSpec((tm, gw), lambda i: (i, col_q // gw)),
            pl.BlockSpec((tm, PAIR_W), lambda i: (i, col_k // PAIR_W)),
            pl.BlockSpec((tm, PAIR_W), lambda i: (i, col_v // PAIR_W)),
            pl.BlockSpec((tm, PAIR_W), tab),
            pl.BlockSpec((tm, PAIR_W), tab),
            pl.BlockSpec((tm, PAIR_W), tab),
        ],
        out_specs=[pl.BlockSpec((tm, gw), lambda i: (i, 0)), pl.BlockSpec((tm, 2 * PAIR_W), lambda i: (i, 0)),
                   pl.BlockSpec((tm, 2 * PAIR_W), lambda i: (i, 0))],
        out_shape=[jax.ShapeDtypeStruct((m, gw), BF16), jax.ShapeDtypeStruct((m, 2 * PAIR_W), BF16),
                   jax.ShapeDtypeStruct((m, 2 * PAIR_W), BF16)],
        compiler_params=_cparams(("parallel",)),
        name="swa_prep",
    )(p32, p32, p32, *tables)


def _sink_col(sink_ref, p, n):
    row = lax.broadcasted_iota(jnp.int32, (2 * n, 1), 0)
    return jnp.where(row < n, sink_ref[2 * p], sink_ref[2 * p + 1])


def _swa_kernel(sink_ref, q_ref, kp_ref, kc_ref, kn_ref, vp_ref, vc_ref, vn_ref, kx_ref, vx_ref, o_ref, *, nblk):
    n = pl.program_id(1)
    blk = q_ref.shape[0]
    qi = lax.broadcasted_iota(jnp.int32, (2 * blk, blk), 0) % blk
    kj = lax.broadcasted_iota(jnp.int32, (2 * blk, blk), 1)
    m_prev = kj >= qi + jnp.where(n > 0, 0, blk)
    m_next = kj <= qi - jnp.where(n < nblk - 1, 0, blk)
    npairs = q_ref.shape[1] // PAIR_W
    for p in range(npairs):
        sl = slice(p * PAIR_W, (p + 1) * PAIR_W)
        g = p // (npairs // SWA_KV_HEADS)
        gs = slice(g * PAIR_W, (g + 1) * PAIR_W)
        qs = _stack_pair(q_ref[:, sl].astype(F32))
        o = _pair_softmax_av(
            qs, [kp_ref[:, gs], kc_ref[:, gs], kn_ref[:, gs], kx_ref[:, gs]],
            [vp_ref[:, gs], vc_ref[:, gs], vn_ref[:, gs], vx_ref[:, gs]],
            [None] * 4, [m_prev, None, m_next, None], _sink_col(sink_ref, p, blk))
        o_ref[:, sl] = _unstack_pair(o).astype(o_ref.dtype)


def _swa_call(sink, qr, kd, vd, kxd, vxd, *, bsz, seq, n_ctx):
    gw = qr.shape[1]
    kw = kd.shape[1]
    nblk = seq // SWA_BLOCK
    cur = lambda b, n: (b * nblk + n, 0)
    prev = lambda b, n: (b * nblk + jnp.maximum(n - 1, 0), 0)
    nxt = lambda b, n: (b * nblk + jnp.minimum(n + 1, nblk - 1), 0)
    cx = lambda b, n: (b, 0)
    return pl.pallas_call(
        functools.partial(_swa_kernel, nblk=nblk),
        grid=(bsz, nblk),
        in_specs=[pl.BlockSpec(memory_space=pltpu.SMEM),
                  pl.BlockSpec((SWA_BLOCK, gw), cur),
                  pl.BlockSpec((SWA_BLOCK, kw), prev), pl.BlockSpec((SWA_BLOCK, kw), cur), pl.BlockSpec((SWA_BLOCK, kw), nxt),
                  pl.BlockSpec((SWA_BLOCK, kw), prev), pl.BlockSpec((SWA_BLOCK, kw), cur), pl.BlockSpec((SWA_BLOCK, kw), nxt),
                  pl.BlockSpec((n_ctx, kw), cx), pl.BlockSpec((n_ctx, kw), cx)],
        out_specs=pl.BlockSpec((SWA_BLOCK, gw), cur),
        out_shape=jax.ShapeDtypeStruct((bsz * seq, gw), BF16),
        compiler_params=_cparams(("parallel", "arbitrary")),
        name="swa_attn",
    )(sink, qr, kd, kd, kd, vd, vd, vd, kxd, vxd)


def _ctx_attn_kernel(sink_ref, qa_ref, ka_ref, va_ref, qd_ref, kd_ref, vd_ref, oa_ref, od_ref):
    n = qa_ref.shape[0]
    npairs = qa_ref.shape[1] // PAIR_W
    for p in range(npairs):
        sl = slice(p * PAIR_W, (p + 1) * PAIR_W)
        qs = _stack_pair(qa_ref[:, sl].astype(F32) * ATTN_SCALE)
        o = _pair_softmax_av(qs, [ka_ref[:, sl]], [va_ref[:, sl]], [None], [None], None)
        oa_ref[:, sl] = _unstack_pair(o).astype(oa_ref.dtype)
    for p in range(npairs):
        sl = slice(p * PAIR_W, (p + 1) * PAIR_W)
        g = p // (npairs // SWA_KV_HEADS)
        gs = slice(g * PAIR_W, (g + 1) * PAIR_W)
        qs = _stack_pair(qd_ref[:, sl].astype(F32))
        o = _pair_softmax_av(qs, [kd_ref[:, gs]], [vd_ref[:, gs]], [None], [None], _sink_col(sink_ref, p, n))
        od_ref[:, sl] = _unstack_pair(o).astype(od_ref.dtype)


def _ctx_attn_call(sink, pc16, qx, kxd, vxd, *, bsz, n_ctx):
    gw = qx.shape[1]
    kw = kxd.shape[1]
    return pl.pallas_call(
        _ctx_attn_kernel,
        grid=(bsz,),
        in_specs=[pl.BlockSpec(memory_space=pltpu.SMEM),
                  pl.BlockSpec((n_ctx, gw), lambda b: (b, 0)), pl.BlockSpec((n_ctx, gw), lambda b: (b, 1)),
                  pl.BlockSpec((n_ctx, gw), lambda b: (b, 2)), pl.BlockSpec((n_ctx, gw), lambda b: (b, 0)),
                  pl.BlockSpec((n_ctx, kw), lambda b: (b, 0)), pl.BlockSpec((n_ctx, kw), lambda b: (b, 0))],
        out_specs=[pl.BlockSpec((n_ctx, gw), lambda b: (b, 0)), pl.BlockSpec((n_ctx, gw), lambda b: (b, 0))],
        out_shape=[jax.ShapeDtypeStruct((bsz * n_ctx, gw), BF16)] * 2,
        compiler_params=_cparams(("parallel",)),
        name="ctx_attn",
    )(sink, pc16, pc16, pc16, qx, kxd, vxd)


def _conv_kernel(u_ref, b_ref, c_ref, up_ref, cp_ref, un_ref, cn_ref, w_ref, o_ref, *, tiles_per_seq):
    i = pl.program_id(0)
    tm = u_ref.shape[0]
    pos = i % tiles_per_seq
    keep_prev = jnp.where(pos == 0, 0.0, 1.0)
    keep_next = jnp.where(pos == tiles_per_seq - 1, 0.0, 1.0)
    z = c_ref[...] * u_ref[...]
    z_prev = (cp_ref[...] * up_ref[...])[CONV_HALO - 1:CONV_HALO, :] * keep_prev
    z_next = (cn_ref[...] * un_ref[...])[0:1, :] * keep_next
    row = lax.broadcasted_iota(jnp.int32, z.shape, 0)
    zm1 = jnp.where(row == 0, z_prev, pltpu.roll(z, 1, 0))
    zp1 = jnp.where(row == tm - 1, z_next, pltpu.roll(z, tm - 1, 0))
    w = w_ref[...]
    o_ref[...] = (b_ref[...] * (w[0:1] * zm1 + w[1:2] * z + w[2:3] * zp1)).astype(o_ref.dtype)


def _conv_call(p32, conv_w, *, col_u, col_b, col_c, rows_per_seq):
    m = p32.shape[0]
    gw = conv_w.shape[1]
    tm = _pick_tile(rows_per_seq, ELEMWISE_TM)
    hb =tm // CONV_HALO
    n_halo = m // CONV_HALO
    cur = lambda col: (lambda i: (i, col // gw))
    prv = lambda col: (lambda i: (jnp.maximum(i * hb - 1, 0), col // gw))
    nxt = lambda col: (lambda i: (jnp.minimum((i + 1) * hb, n_halo - 1), col // gw))
    return pl.pallas_call(
        functools.partial(_conv_kernel, tiles_per_seq=rows_per_seq // tm),
        grid=(m // tm,),
        in_specs=[pl.BlockSpec((tm, gw), cur(col_u)), pl.BlockSpec((tm, gw), cur(col_b)), pl.BlockSpec((tm, gw), cur(col_c)),
                  pl.BlockSpec((CONV_HALO, gw), prv(col_u)), pl.BlockSpec((CONV_HALO, gw), prv(col_c)),
                  pl.BlockSpec((CONV_HALO, gw), nxt(col_u)), pl.BlockSpec((CONV_HALO, gw), nxt(col_c)),
                  pl.BlockSpec((SC_KSIZE, gw), lambda i: (0, 0))],
        out_specs=pl.BlockSpec((tm, gw), lambda i: (i, 0)),
        out_shape=jax.ShapeDtypeStruct((m, gw), BF16),
        compiler_params=_cparams(("parallel",)),
        name="short_conv",
    )(p32, p32, p32, p32, p32, p32, p32, conv_w)


def _gla_kernel(*refs, reverse, nt, fuse_out):
    if fuse_out:
        (q_ref, k_ref, v_ref, z_ref, w2_ref, gb_ref, s0_ref, of_ref, g_ref, gain_ref, o_ref, sfin_ref, st_ref) = refs
    else:
        (q_ref, k_ref, v_ref, z_ref, w2_ref, gb_ref, s0_ref, o_ref, sfin_ref, st_ref) = refs
    i = pl.program_id(1)

    @pl.when(i == 0)
    def _():
        st_ref[...] = s0_ref[...]

    tile = q_ref.shape[0]
    c_len = min(GLA_C, tile)
    dk2 = PAIR_W
    dv2 = v_ref.shape[1] // (q_ref.shape[1] // dk2)
    dv = dv2 // 2
    u = jnp.dot(z_ref[...].astype(BF16), w2_ref[...], preferred_element_type=F32) + gb_ref[...]
    la = (jnp.minimum(u, 0.0) - jnp.log1p(jnp.exp(-jnp.abs(u)))) * (1.0 / GLA_TAU)

    r_i = lax.broadcasted_iota(jnp.int32, (c_len, c_len), 0)
    c_i = lax.broadcasted_iota(jnp.int32, (c_len, c_len), 1)
    tri = (r_i <= c_i) if reverse else (r_i >= c_i)
    tri_bf = jnp.where(tri, 1.0, 0.0).astype(BF16)
    r2 = lax.broadcasted_iota(jnp.int32, (2 * c_len, c_len), 0) % c_len
    c2 = lax.broadcasted_iota(jnp.int32, (2 * c_len, c_len), 1)
    tri2 = (r2 <= c2) if reverse else (r2 >= c2)
    bd_r = lax.broadcasted_iota(jnp.int32, (dv2, dk2), 0) // dv
    bd_c = lax.broadcasted_iota(jnp.int32, (dv2, dk2), 1) // HEAD_DIM
    block_diag = bd_r == bd_c
    last_row = 0 if reverse else c_len - 1
    mid_row = c_len // 2

    n_chunks = tile // c_len
    order = range(n_chunks - 1, -1, -1) if reverse else range(n_chunks)
    for c in order:
        rows = slice(c * c_len, (c + 1) * c_len)
        la_c = la[rows]
        la_hi = la_c.astype(BF16)
        la_lo = (la_c - la_hi.astype(F32)).astype(BF16)
        cum = (jnp.dot(tri_bf, la_hi, preferred_element_type=F32)
               + jnp.dot(tri_bf, la_lo, preferred_element_type=F32))
        last = cum[last_row:last_row + 1]
        cmid = cum[mid_row:mid_row + 1]
        qc = q_ref[rows, :] * ATTN_SCALE
        kc = k_ref[rows, :]
        vc = v_ref[rows, :].astype(BF16)
        q_in = (qc * jnp.exp(cum)).astype(BF16)
        q_t = qc * jnp.exp(cum - cmid)
        k_t = (kc * jnp.exp(cmid - cum)).astype(BF16)
        k_p = (kc * jnp.exp(last - cum)).astype(BF16)
        g = jnp.exp(last)
        for p in range(q_ref.shape[1] // dk2):
            ls = slice(p * dk2, (p + 1) * dk2)
            a = lax.dot_general(_stack_pair(q_t[:, ls]), k_t[:, ls], _NT, preferred_element_type=F32)
            a = jnp.where(tri2, a, 0.0).astype(BF16)
            o0 = jnp.dot(a[:c_len], vc[:, p * dv2:p * dv2 + dv], preferred_element_type=F32)
            o1 = jnp.dot(a[c_len:], vc[:, p * dv2 + dv:(p + 1) * dv2], preferred_element_type=F32)
            st = st_ref[p]
            o_int = lax.dot_general(q_in[:, ls], st.astype(BF16), _NT, preferred_element_type=F32)
            o_p = jnp.concatenate([o0, o1], axis=1) + o_int
            upd = lax.dot_general(vc[:, p * dv2:(p + 1) * dv2], k_p[:, ls], _TN, preferred_element_type=F32)
            st_ref[p] = g[:, ls] * st + jnp.where(block_diag, upd, 0.0)
            if not fuse_out:
                o_ref[rows, p * dv2:(p + 1) * dv2] = o_p
            else:
                tot = of_ref[rows, p * dv2:(p + 1) * dv2] + o_p
                for hh in range(2):
                    hs = slice(p * dv2 + hh * dv, p * dv2 + (hh + 1) * dv)
                    oh = tot[:, hh * dv:(hh + 1) * dv]
                    on = oh * lax.rsqrt(jnp.mean(oh * oh, axis=-1, keepdims=True) + LN_EPS) * gain_ref[...]
                    o_ref[rows, hs] = (on * _silu(g_ref[rows, hs])).astype(o_ref.dtype)

    @pl.when(i == nt - 1)
    def _():
        sfin_ref[...] = st_ref[...]


def _gla_call(p, w2pad, gbias, s0, fuse, *, cols, bsz, seq, reverse):
    nq = GLA_HEADS * HEAD_DIM
    nv = s0.shape[1] * s0.shape[2]
    tile = _pick_tile(seq, GLA_TILE)
    nt = seq // tile
    tix = (lambda i: nt - 1 - i) if reverse else (lambda i: i)
    blk = lambda w, col: pl.BlockSpec((tile, w), lambda b, i: (b * nt + tix(i), col // w))
    const2 = lambda b, i: (0, 0)
    st_spec = pl.BlockSpec((None,) + s0.shape[1:], lambda b, i: (b, 0, 0, 0))
    in_specs = [blk(nq, cols['gl_q']), blk(nq, cols['gl_k']), blk(nv, cols['gl_v']), blk(V7X_LANES, cols['gl_z']),
                pl.BlockSpec(w2pad.shape, const2), pl.BlockSpec(gbias.shape, const2), st_spec]
    args = [p, p, p, p, w2pad, gbias, s0]
    if fuse is not None:
        o_other, gain = fuse
        in_specs += [blk(nv, 0), blk(nv, cols['gl_g']), pl.BlockSpec(gain.shape, const2)]
        args += [o_other, p, gain]
    out_dtype = BF16 if fuse is not None else F32
    return pl.pallas_call(
        functools.partial(_gla_kernel, reverse=reverse, nt=nt, fuse_out=fuse is not None),
        grid=(bsz, nt),
        in_specs=in_specs,
        out_specs=[blk(nv, 0), st_spec],
        out_shape=[jax.ShapeDtypeStruct((bsz * seq, nv), out_dtype), jax.ShapeDtypeStruct(s0.shape, F32)],
        scratch_shapes=[pltpu.VMEM(s0.shape[1:], F32)],
        compiler_params=_cparams(("parallel", "arbitrary")),
        name="gla_bwd" if reverse else "gla_fwd",
    )(*args)


def _gla_all(p32, pc32, w2, gb, gain, *, bsz, seq, n_ctx, d):
    cols = _p32_cols(d)
    nq = GLA_HEADS * HEAD_DIM
    dv = d // 4 // GLA_HEADS
    w2pad = [jnp.zeros((V7X_LANES, nq), F32).at[GLA_RANK * k:GLA_RANK * (k + 1)].set(w2[k]).astype(BF16) for k in range(2)]
    gbias = [gb[k].reshape(1, nq) for k in range(2)]
    s0 = jnp.zeros((bsz, GLA_HEADS // 2, 2 * dv, PAIR_W), F32)
    gain2 = gain.reshape(1, dv)
    oc_f, sc_f = _gla_call(pc32, w2pad[0], gbias[0], s0, None, cols=cols, bsz=bsz, seq=n_ctx, reverse=False)
    y_ctx, sc_b = _gla_call(pc32, w2pad[1], gbias[1], s0, (oc_f, gain2), cols=cols, bsz=bsz, seq=n_ctx, reverse=True)
    o_f, _ = _gla_call(p32, w2pad[0], gbias[0], sc_f, None, cols=cols, bsz=bsz, seq=seq, reverse=False)
    y_lat, _ = _gla_call(p32, w2pad[1], gbias[1], sc_b, (o_f, gain2), cols=cols, bsz=bsz, seq=seq, reverse=True)
    return y_lat, y_ctx


def _swiglu(x, wg, wu, wd):
    g = jnp.dot(x, wg, preferred_element_type=F32)
    u = jnp.dot(x, wu, preferred_element_type=F32)
    a = (_silu(g) * u).astype(BF16)
    return jnp.dot(a, wd, preferred_element_type=F32)


def _first_argmax(vals, idx, sentinel):
    m = jnp.max(vals, axis=0, keepdims=True)
    first = jnp.min(jnp.where(vals == m, idx, sentinel), axis=0, keepdims=True)
    return m, first


def _router_kernel(h_ref, wr_ref, rb_ref, su_ref, eidx_ref, wts_ref, rank_ref, cnt_ref, run_ref):
    i = pl.program_id(0)

    @pl.when(i == 0)
    def _():
        run_ref[...] = jnp.zeros_like(run_ref)

    t = h_ref.shape[0]
    gsz = N_EXPERTS // N_GROUPS
    logits = lax.dot_general(wr_ref[...], h_ref[...], _NT, preferred_element_type=F32)
    scores = jax.nn.sigmoid(logits)
    sel = scores + rb_ref[...]
    sub = lax.broadcasted_iota(jnp.int32, (gsz, t), 0)
    gscore = []
    for g in range(N_GROUPS):
        blk = sel[g * gsz:(g + 1) * gsz]
        m1, a1 = _first_argmax(blk, sub, gsz)
        m2 = jnp.max(jnp.where(sub == a1, -jnp.inf, blk), axis=0, keepdims=True)
        gscore.append(m1 + m2)
    gcur = jnp.concatenate(gscore, axis=0)
    gid = lax.broadcasted_iota(jnp.int32, (N_GROUPS, t), 0)
    gkeep = jnp.zeros((N_GROUPS, t), F32)
    for _ in range(TOPK_GROUPS):
        _, a = _first_argmax(gcur, gid, N_GROUPS)
        hit = gid == a
        gkeep = jnp.where(hit, 1.0, gkeep)
        gcur = jnp.where(hit, -jnp.inf, gcur)
    cur = jnp.concatenate(
        [jnp.where(gkeep[g:g + 1] > 0.0, sel[g * gsz:(g + 1) * gsz], -jnp.inf) for g in range(N_GROUPS)], axis=0)
    eid = lax.broadcasted_iota(jnp.int32, (N_EXPERTS, t), 0)
    chosen = jnp.zeros((N_EXPERTS, t), F32)
    hits, picks, wraw = [], [], []
    for _ in range(TOP_K):
        _, a = _first_argmax(cur, eid, N_EXPERTS)
        hit = eid == a
        hits.append(hit)
        picks.append(a)
        wraw.append(jnp.sum(jnp.where(hit, scores, 0.0), axis=0, keepdims=True))
        chosen = jnp.where(hit, 1.0, chosen)
        cur = jnp.where(hit, -jnp.inf, cur)
    wsum = wraw[0]
    for w in wraw[1:]:
        wsum = wsum + w
    eidx_ref[...] = jnp.concatenate(picks, axis=0)
    wts_ref[...] = jnp.concatenate([w / wsum * ROUTED_SCALE for w in wraw], axis=0)
    before = jnp.dot(chosen.astype(BF16), su_ref[...], preferred_element_type=F32) + run_ref[...][:, 0:1]
    rank_ref[...] = jnp.concatenate(
        [jnp.sum(jnp.where(hit, before, 0.0), axis=0, keepdims=True) for hit in hits], axis=0).astype(jnp.int32)
    run_ref[...] = run_ref[...] + jnp.sum(chosen, axis=1, keepdims=True)
    cnt_ref[...] = run_ref[...]


def _router_call(h, w_router, router_bias):
    n, d = h.shape
    t = _pick_tile(n, ROUTER_TILE)
    wr_t = w_router.T.astype(BF16)
    strict_upper = jnp.triu(jnp.ones((t, t), F32), 1).astype(BF16)
    const2 = lambda i: (0, 0)
    tok = lambda i: (0, i)
    eidx, wts, rank, cnt = pl.pallas_call(
        _router_kernel,
        grid=(n // t,),
        in_specs=[pl.BlockSpec((t, d), lambda i: (i, 0)), pl.BlockSpec((N_EXPERTS, d), const2),
                  pl.BlockSpec((N_EXPERTS, 1), const2), pl.BlockSpec((t, t), const2)],
        out_specs=[pl.BlockSpec((TOP_K, t), tok), pl.BlockSpec((TOP_K, t), tok), pl.BlockSpec((TOP_K, t), tok),
                   pl.BlockSpec((N_EXPERTS, V7X_LANES), const2)],
        out_shape=[jax.ShapeDtypeStruct((TOP_K, n), jnp.int32), jax.ShapeDtypeStruct((TOP_K, n), F32),
                   jax.ShapeDtypeStruct((TOP_K, n), jnp.int32), jax.ShapeDtypeStruct((N_EXPERTS, V7X_LANES), F32)],
        scratch_shapes=[pltpu.VMEM((N_EXPERTS, V7X_LANES), F32)],
        compiler_params=_cparams(("arbitrary",)),
        name="router",
    )(h, wr_t, router_bias.reshape(N_EXPERTS, 1).astype(F32), strict_upper)
    return eidx, wts, rank, cnt[:, 0].astype(jnp.int32)


def _dispatch_kernel(zoff_ref, nused_ref, dest_ref, hp_ref, xs_hbm, zbuf, sem, zsem):
    i = pl.program_id(0)
    t = dest_ref.shape[1]
    n_blocks = xs_hbm.shape[0] // MOE_TB

    def clear_block(off):
        return pltpu.make_async_copy(zbuf, xs_hbm.at[pl.ds(pl.multiple_of(off, MOE_TB), MOE_TB)], zsem)

    @pl.when(i == 0)
    def _():
        zbuf[...] = jnp.zeros_like(zbuf)
        for e in range(N_EXPERTS):
            clear_block(zoff_ref[e]).start()
        for e in range(N_EXPERTS):
            clear_block(0).wait()

        def clear_tail(b, carry):
            cp = clear_block(b * MOE_TB)
            cp.start()
            cp.wait()
            return carry

        lax.fori_loop(nused_ref[0], n_blocks, clear_tail, 0)

    def row_copy(j, slot):
        return pltpu.make_async_copy(hp_ref.at[pl.ds(j, 1)], xs_hbm.at[pl.ds(slot, 1)], sem)

    def issue(j, carry):
        for k in range(TOP_K):
            row_copy(j, dest_ref[k, j]).start(priority=k % 2)
        return carry

    def drain(j, carry):
        for k in range(TOP_K):
            row_copy(0, 0).wait()
        return carry

    lax.fori_loop(0, t, issue, 0)
    lax.fori_loop(0, t, drain, 0)


def _dispatch_call(zero_off, n_used, dest, h_packed, n_rows):
    n, c = h_packed.shape
    t = _pick_tile(n, DISPATCH_TILE)
    grid_spec = pltpu.PrefetchScalarGridSpec(
        num_scalar_prefetch=2,
        grid=(n // t,),
        in_specs=[pl.BlockSpec((TOP_K, t), lambda i, z, nu: (0, i), memory_space=pltpu.SMEM),
                  pl.BlockSpec((t, c), lambda i, z, nu: (i, 0))],
        out_specs=pl.BlockSpec(memory_space=pl.ANY),
        scratch_shapes=[pltpu.VMEM((MOE_TB, c), jnp.uint32), pltpu.SemaphoreType.DMA, pltpu.SemaphoreType.DMA],
    )
    return pl.pallas_call(
        _dispatch_kernel,
        grid_spec=grid_spec,
        out_shape=jax.ShapeDtypeStruct((n_rows, c), jnp.uint32),
        compiler_params=_cparams(("arbitrary",)),
        name="moe_dispatch",
    )(zero_off, n_used, dest, h_packed)


def _expert_kernel(be_ref, nused_ref, first_ref, slot_ref, nexte_ref, x_ref, oprev_ref, wg_hbm, wu_hbm, wd_hbm,
                   y_out, wg_f, wu_f, wd_f, wg_s, wu_s, wd_s, sem, ybuf, ssem, *, layer, dump_row):
    b = pl.program_id(0)

    def weight_copies(e, slot):
        return (pltpu.make_async_copy(wg_hbm.at[layer, e], wg_f.at[slot], sem.at[slot, 0]),
                pltpu.make_async_copy(wu_hbm.at[layer, e], wu_f.at[slot], sem.at[slot, 1]),
                pltpu.make_async_copy(wd_hbm.at[layer, e], wd_f.at[slot], sem.at[slot, 2]))

    @pl.when(b == 0)
    def _():
        for cp in weight_copies(be_ref[0], 0):
            cp.start()

    @pl.when(first_ref[b] == 1)
    def _():
        slot = slot_ref[b]
        for cp in weight_copies(be_ref[b], slot):
            cp.wait()

        @pl.when(nexte_ref[b] >= 0)
        def _():
            for cp in weight_copies(nexte_ref[b], 1 - slot):
                cp.start()

        wg_s[...] = wg_f[slot].astype(BF16)
        wu_s[...] = wu_f[slot].astype(BF16)
        wd_s[...] = wd_f[slot].astype(BF16)

    nu = nused_ref[0]
    tb = x_ref.shape[0]

    def scatter_copy(src_slot, r, dst_row):
        return pltpu.make_async_copy(ybuf.at[src_slot, pl.ds(r, 1)], y_out.at[pl.ds(dst_row, 1)], ssem.at[src_slot])

    def wait_scatter(src_slot):
        for r in range(tb):
            scatter_copy(src_slot, r, 0).wait()

    @pl.when(b == 0)
    def _():
        ybuf[...] = jnp.zeros_like(ybuf)

    @pl.when((b >= 1) & (b <= nu))
    def _():
        wait_scatter(b % 2)

    @pl.when(b < nu)
    def _():
        prev_slot = (b + 1) % 2

        def start_rows(lo, hi):
            for r in range(lo, hi):
                dst = jnp.where(b == 0, dump_row + r, oprev_ref[0, 0, r])
                scatter_copy(prev_slot, r, dst).start(priority=r % 2)

        n_pieces = 6
        cuts = [tb * p // n_pieces for p in range(n_pieces + 1)]
        x = _unpack_halves(x_ref[...])
        half_e = wg_s.shape[1] // 2
        acts = []
        for p in range(2):
            start_rows(cuts[p], cuts[p + 1])
            cs = slice(p * half_e, (p + 1) * half_e)
            g = jnp.dot(x, wg_s[:, cs], preferred_element_type=F32)
            u = jnp.dot(x, wu_s[:, cs], preferred_element_type=F32)
            acts.append((_silu(g) * u).astype(BF16))
        a = jnp.concatenate(acts, axis=1)
        quarter = wd_s.shape[1] // 4
        ys = []
        for p in range(4):
            start_rows(cuts[2 + p], cuts[3 + p])
            ys.append(jnp.dot(a, wd_s[:, p * quarter:(p + 1) * quarter], preferred_element_type=F32))
        ybuf[b % 2] = _pack_halves(jnp.concatenate(ys, axis=1))

    @pl.when(b == nu)
    def _():
        last_slot = (b + 1) % 2
        for r in range(tb):
            scatter_copy(last_slot, r, oprev_ref[0, 0, r]).start(priority=r % 2)
        wait_scatter(last_slot)


def _expert_call(block_e, n_used, run_first, run_slot, next_e, x_sorted, out_rows, n_out_rows, wg, wu, wd, layer):
    n_rows, c = x_sorted.shape
    _, _, d, de = wg.shape
    n_blocks = n_rows // MOE_TB
    xsel = lambda b, be, nu, *_: (jnp.minimum(b, nu[0] - 1), 0)
    osel = lambda b, *_: (jnp.clip(b - 1, 0, n_blocks - 1), 0, 0)
    hbm = pl.BlockSpec(memory_space=pl.ANY)
    grid_spec = pltpu.PrefetchScalarGridSpec(
        num_scalar_prefetch=5,
        grid=(n_blocks + 1,),
        in_specs=[pl.BlockSpec((MOE_TB, c), xsel), pl.BlockSpec((1, 1, MOE_TB), osel, memory_space=pltpu.SMEM),
                  hbm, hbm, hbm],
        out_specs=hbm,
        scratch_shapes=[pltpu.VMEM((2, d, de), F32), pltpu.VMEM((2, d, de), F32), pltpu.VMEM((2, de, d), F32),
                        pltpu.VMEM((d, de), BF16), pltpu.VMEM((d, de), BF16), pltpu.VMEM((de, d), BF16),
                        pltpu.SemaphoreType.DMA((2, 3)),
                        pltpu.VMEM((2, MOE_TB, c), jnp.uint32), pltpu.SemaphoreType.DMA((2,))],
    )
    return pl.pallas_call(
        functools.partial(_expert_kernel, layer=layer, dump_row=n_out_rows - MOE_TB),
        grid_spec=grid_spec,
        out_shape=jax.ShapeDtypeStruct((n_out_rows, c), jnp.uint32),
        compiler_params=_cparams(("arbitrary",)),
        name="experts",
    )(*(jnp.pad(a, (0, 1)) for a in (block_e,)), n_used, *(jnp.pad(a, (0, 1)) for a in (run_first, run_slot, next_e)),
      x_sorted, out_rows, wg, wu, wd)


def _combine_kernel(*refs, alpha):
    (w_ref, h_ref, x_ref, g2_ref, wsg_ref, wsu_ref, wsd_ref, lng_ref, lnb_ref), y_refs, o_ref = (
        refs[:9], refs[9:9 + TOP_K], refs[9 + TOP_K])
    acc = _swiglu(h_ref[...], wsg_ref[...], wsu_ref[...], wsd_ref[...])
    w = w_ref[...]
    half = acc.shape[1] // 2
    lo, hi = acc[:, :half], acc[:, half:]
    for k in range(TOP_K):
        yk = y_refs[k][...]
        wk = w[:, k:k + 1]
        lo = lo + wk * pltpu.bitcast(lax.shift_left(yk, jnp.uint32(16)), F32)
        hi = hi + wk * pltpu.bitcast(yk & jnp.uint32(0xFFFF0000), F32)
    acc = jnp.concatenate([lo, hi], axis=1)
    r = alpha * x_ref[...] + g2_ref[0] * acc
    o_ref[...] = _ln_rows(r) * lng_ref[...] + lnb_ref[...]


def _combine_call(wts_t, h, x2d, gate2, wsg, wsu, wsd, ln_g, ln_b, y_rows, *, row0, rows_per_group, alpha):
    m, d = x2d.shape
    n = h.shape[0]
    de = wsg.shape[1]
    t = _pick_tile(min(rows_per_group, m), COMBINE_TILE)
    assert row0 % t == 0 and rows_per_group % t == 0 and n % t == 0
    t0 = row0 // t
    const2 = lambda i: (0, 0)
    once = pl.Buffered(1)
    y_specs = [pl.BlockSpec((t, d // 2), functools.partial(lambda i, k: (k * (n // t) + t0 + i, 0), k=k))
               for k in range(TOP_K)]
    return pl.pallas_call(
        functools.partial(_combine_kernel, alpha=alpha),
        grid=(m // t,),
        in_specs=[pl.BlockSpec((t, TOP_K), lambda i: (t0 + i, 0)),
                  pl.BlockSpec((t, d), lambda i: (t0 + i, 0)),
                  pl.BlockSpec((t, d), lambda i: (i, 0)),
                  pl.BlockSpec((1, 1, d), lambda i: ((i * t) // rows_per_group, 0, 0)),
                  pl.BlockSpec((d, de), const2, pipeline_mode=once),
                  pl.BlockSpec((d, de), const2, pipeline_mode=once),
                  pl.BlockSpec((de, d), const2, pipeline_mode=once),
                  pl.BlockSpec((1, d), const2), pl.BlockSpec((1, d), const2)] + y_specs,
        out_specs=pl.BlockSpec((t, d), lambda i: (i, 0)),
        out_shape=jax.ShapeDtypeStruct((m, d), F32),
        compiler_params=_cparams(("parallel",)),
        name="moe_combine",
    )(wts_t, h, x2d, gate2, wsg, wsu, wsd, ln_g.reshape(1, d), ln_b.reshape(1, d), *([y_rows] * TOP_K))


def _moe_routed(h, h_packed, w_router, router_bias, w_eg, w_eu, w_ed, layer):
    n = h.shape[0]
    eidx, wts, rank, counts = _router_call(h, w_router, router_bias)
    padded = (counts + MOE_TB - 1) // MOE_TB * MOE_TB
    pad_end = jnp.cumsum(padded)
    pad_start = pad_end - padded
    n_blocks = (n * TOP_K + N_EXPERTS * (MOE_TB - 1) + MOE_TB - 1) // MOE_TB
    block_first = jnp.arange(n_blocks, dtype=jnp.int32) * MOE_TB
    block_e = jnp.minimum(jnp.sum((pad_end[None, :] <= block_first[:, None]).astype(jnp.int32), axis=1), N_EXPERTS - 1)
    n_used = (pad_end[-1:] // MOE_TB).astype(jnp.int32)
    zero_off = jnp.maximum(pad_end - MOE_TB, 0).astype(jnp.int32)
    expert_ids = jnp.arange(N_EXPERTS, dtype=jnp.int32)
    first_row = jnp.sum(jnp.where(eidx[:, :, None] == expert_ids, pad_start.astype(jnp.int32), 0), axis=-1)
    dest = first_row + rank
    x_sorted = _dispatch_call(zero_off, n_used, dest, h_packed, n_blocks * MOE_TB)
    blk = jnp.arange(n_blocks, dtype=jnp.int32)
    used = blk < n_used[0]
    run_first = ((block_e != jnp.concatenate([jnp.full((1,), -1, jnp.int32), block_e[:-1]])) & used).astype(jnp.int32)
    run_slot = ((jnp.cumsum(run_first) - 1) % 2).astype(jnp.int32)
    run_end = jnp.sum(jnp.where(block_e[:, None] == expert_ids, pad_end.astype(jnp.int32), 0), axis=1) // MOE_TB
    next_e = jnp.sum(jnp.where(blk[None, :] == run_end[:, None], block_e[None, :], 0), axis=1)
    next_e = jnp.where(used & (run_end < n_used[0]), next_e, -1).astype(jnp.int32)
    nk = n * TOP_K
    n_rows = n_blocks * MOE_TB
    n_pad = n_rows - nk
    pad_need = (padded - counts).astype(jnp.int32)
    need_end = jnp.cumsum(pad_need)
    j = jnp.arange(n_pad, dtype=jnp.int32)
    owner = jnp.sum((need_end[None, :] <= j[:, None]).astype(jnp.int32), axis=1)
    sel = owner[:, None] == expert_ids
    pick = lambda tab: jnp.sum(jnp.where(sel, tab.astype(jnp.int32), 0), axis=1)
    pad_slot = jnp.where(owner < N_EXPERTS,
                         pick(pad_start) + pick(counts) + j - (pick(need_end) - pick(pad_need)),
                         pad_end[-1].astype(jnp.int32) + j - need_end[-1])
    slots = jnp.concatenate([dest.reshape(-1), pad_slot])
    rows = jnp.arange(n_rows, dtype=jnp.int32)
    _, out_rows = lax.sort((slots, rows), num_keys=1)
    y_rows = _expert_call(block_e, n_used, run_first, run_slot, next_e, x_sorted,
                          out_rows.reshape(n_blocks, 1, MOE_TB), n_rows + MOE_TB, w_eg, w_eu, w_ed, layer)
    return wts.T, y_rows


def _pack_w_in(w, d):
    gw = d // 4
    cuts = np.cumsum([gw, gw, gw, gw, gw, gw, gw // 2, gw // 2, gw, gw, 2 * GLA_RANK, gw, gw // 4, gw // 4])[:-1].tolist()
    (na_q, na_k, na_v, sc_u, sc_b, sc_c, gl_q, gl_k, gl_v, gl_g, gl_z, sw_q, sw_k, sw_v) = jnp.split(w, cuts, axis=1)
    zpad = jnp.zeros((d, V7X_LANES - 2 * GLA_RANK), w.dtype)
    cols = [na_q, na_k, na_v, sc_u, sc_b, sc_c, gl_v, gl_g, sw_q, gl_q, gl_k, sw_k, sw_v, gl_z, zpad]
    packed = jnp.concatenate(cols, axis=1)
    pad = (-packed.shape[1]) % PROJ_TN
    packed = jnp.pad(packed, ((0, 0), (0, pad)))
    return packed.astype(BF16)


def _p32_cols(d):
    gw = d // 4
    cols, c = {}, 0
    for name, wdt in (('sc_u', gw), ('sc_b', gw), ('sc_c', gw), ('gl_v', gw), ('gl_g', gw), ('sw_q', gw),
                      ('gl_q', gw // 2), ('gl_k', gw // 2), ('sw_k', gw // 4), ('sw_v', gw // 4), ('gl_z', V7X_LANES)):
        cols[name] = c
        c += wdt
    return cols


def _token_mixers(p16, p32, pc16, pc32, rpb, conv_w, gla_w2, gla_b, gla_norm_g, sink, with_ctx_out, *,
                  bsz, seq, n_ctx, d):
    cols = _p32_cols(d)
    kh = min(NA_KH, seq // GRID_W)
    y_na = _na_call(p16, pc16, _na_bias_table(rpb, kh), bsz=bsz, seq=seq, n_ctx=n_ctx)
    conv_cols = dict(col_u=cols['sc_u'], col_b=cols['sc_b'], col_c=cols['sc_c'])
    y_sc = _conv_call(p32, conv_w, rows_per_seq=seq, **conv_cols)
    y_gl, yc_gl = _gla_all(p32, pc32, gla_w2, gla_b, gla_norm_g, bsz=bsz, seq=seq, n_ctx=n_ctx, d=d)
    tables = _rope_tables(seq)
    swa_cols = dict(col_q=cols['sw_q'], col_k=cols['sw_k'], col_v=cols['sw_v'])
    qr, kd, vd = _swa_prep_call(p32, tables, rows_per_seq=seq, rope=True, **swa_cols)
    ctx_tables = tuple(t[:n_ctx] for t in tables)
    qx, kxd, vxd = _swa_prep_call(pc32, ctx_tables, rows_per_seq=n_ctx, rope=False, **swa_cols)
    y_sw = _swa_call(sink, qr, kd, vd, kxd, vxd, bsz=bsz, seq=seq, n_ctx=n_ctx)
    y_lat = (y_na, y_sc, y_gl, y_sw)
    if not with_ctx_out:
        return y_lat, None
    yc_na, yc_sw = _ctx_attn_call(sink, pc16, qx, kxd, vxd, bsz=bsz, n_ctx=n_ctx)
    yc_sc = _conv_call(pc32, conv_w, rows_per_seq=n_ctx, **conv_cols)
    return y_lat, (yc_na, yc_sc, yc_gl, yc_sw)


def kernel(x, c, ctx, c_ctx, w_ada, b_ada, w_in, na_rpb, conv_w, gla_w2, gla_b, gla_norm_g, swa_sink, w_out,
           ln1_g, ln1_b, w_router, router_bias, w_exp_gate, w_exp_up, w_exp_down, w_sh_gate, w_sh_up, w_sh_down,
           ln2_g, ln2_b):
    bsz, seq, d = x.shape
    n_ctx = ctx.shape[1]
    nc = bsz * n_ctx
    depth = w_in.shape[0]
    alpha = (2 * depth) ** 0.25
    n16 = 3 * (d // 4)
    x2 = x.reshape(bsz * seq, d)
    hc2 = ctx.reshape(nc, d)
    c_rows = jnp.zeros((V7X_SUBLANES, d), F32).at[:bsz].set(c).at[bsz].set(c_ctx)
    for layer in range(depth):
        last = layer == depth - 1
        mod = _ada_call(c_rows, w_ada, b_ada[layer], layer)
        sh1, sc1, g1, sh2, sc2, g2 = [t[:bsz, None, :] for t in jnp.split(mod, 6, axis=-1)]
        sh1c, sc1c, g1c, sh2c, sc2c, g2c = [t[bsz:bsz + 1, None, :] for t in jnp.split(mod, 6, axis=-1)]
        w_p = _pack_w_in(w_in[layer], d)
        p16, p32 = _proj_call(x2, sh1, sc1, w_p, rows_per_group=seq, n16=n16)
        pc16, pc32 = _proj_call(hc2, sh1c, sc1c, w_p, rows_per_group=nc, n16=n16)
        y_lat, y_ctx = _token_mixers(p16, p32, pc16, pc32, na_rpb[layer], conv_w[layer], gla_w2[layer], gla_b[layer],
                                     gla_norm_g[layer], swa_sink[layer], not last, bsz=bsz, seq=seq, n_ctx=n_ctx, d=d)
        w_o = w_out[layer].astype(BF16)
        x2, h_lat, hp_lat = _outproj_call(y_lat, x2, g1, w_o, ln1_g[layer], ln1_b[layer], sh2, sc2,
                                          rows_per_group=seq, alpha=alpha)
        shared_w = (w_sh_gate[layer].astype(BF16), w_sh_up[layer].astype(BF16), w_sh_down[layer].astype(BF16))
        route_w = (w_router[layer], router_bias[layer], w_exp_gate, w_exp_up, w_exp_down, layer)
        if last:
            wts_t, y_rows = _moe_routed(h_lat, hp_lat, *route_w)
            x2 = _combine_call(wts_t, h_lat, x2, g2, *shared_w, ln2_g[layer], ln2_b[layer], y_rows,
                               row0=0, rows_per_group=seq, alpha=alpha)
        else:
            hc2, h_ctx, hp_ctx = _outproj_call(y_ctx, hc2, g1c, w_o, ln1_g[layer], ln1_b[layer], sh2c, sc2c,
                                               rows_per_group=nc, alpha=alpha)
            h_all = jnp.concatenate([h_ctx, h_lat], axis=0)
            hp_all = jnp.concatenate([hp_ctx, hp_lat], axis=0)
            wts_t, y_rows = _moe_routed(h_all, hp_all, *route_w)
            hc2 = _combine_call(wts_t, h_all, hc2, g2c, *shared_w, ln2_g[layer], ln2_b[layer], y_rows,
                                row0=0, rows_per_group=nc, alpha=alpha)
            x2 = _combine_call(wts_t, h_all, x2, g2, *shared_w, ln2_g[layer], ln2_b[layer], y_rows,
                               row0=nc, rows_per_group=seq, alpha=alpha)
    return x2.reshape(bsz, seq, d)
```

```python
import functools

import jax
import jax.numpy as jnp
import numpy as np
from jax import lax
from jax.experimental import pallas as pl
from jax.experimental.pallas import tpu as pltpu

GRID_W = 64
HEAD_DIM = 64
NA_KH = 8
NA_KW = 16
SC_KSIZE = 3
GLA_HEADS = 4
GLA_RANK = 16
GLA_TAU = 16.0
SWA_KV_HEADS = 2
SWA_BLOCK = 128
ROPE_BASE = 10000.0
N_EXPERTS = 64
TOP_K = 8
N_GROUPS = 8
TOPK_GROUPS = 4
ROUTED_SCALE = 2.5
LN_EPS = 1e-6

V7X_LANES = 128
V7X_SUBLANES = 8
V7X_VMEM_LIMIT_BYTES = 48 * 1024 * 1024

PROJ_TM = 1024
PROJ_TN = 512
OUT_TM = 512
MOE_TB = 256
ROUTER_TILE = 512
DISPATCH_TILE = 128
COMBINE_TILE = 256
ELEMWISE_TM = 512
NA_ROWS_PER_STEP = 4
CONV_HALO = 8
GLA_C = 64
GLA_TILE = 512
ADA_TN = 2048

F32 = jnp.float32
BF16 = jnp.bfloat16
PAIR_W = 2 * HEAD_DIM
ATTN_SCALE = HEAD_DIM ** -0.5
_NT = (((1,), (1,)), ((), ()))
_TN = (((0,), (0,)), ((), ()))


def _cparams(sem):
    return pltpu.CompilerParams(dimension_semantics=sem, vmem_limit_bytes=V7X_VMEM_LIMIT_BYTES)


def _pick_tile(m, preferred):
    t = preferred
    while t > 8 and m % t:
        t //= 2
    assert m % t == 0, (m, preferred)
    return t


def _ln_rows(x):
    mu = jnp.mean(x, axis=-1, keepdims=True)
    xc = x - mu
    var = jnp.mean(xc * xc, axis=-1, keepdims=True)
    return xc * lax.rsqrt(var + LN_EPS)


def _silu(x):
    return x * jax.nn.sigmoid(x)


def _ada_kernel(c_ref, w_ref, b_ref, o_ref):
    a = _silu(c_ref[...]).astype(BF16)
    o_ref[...] = jnp.dot(a, w_ref[...].astype(BF16), preferred_element_type=F32) + b_ref[...]


def _ada_call(c_rows, w_ada, b_ada, layer):
    r, d = c_rows.shape
    n = w_ada.shape[2]
    tn = _pick_tile(n, ADA_TN)
    return pl.pallas_call(
        _ada_kernel,
        grid=(n // tn,),
        in_specs=[pl.BlockSpec((r, d), lambda j: (0, 0)), pl.BlockSpec((None, d, tn), lambda j: (layer, 0, j)),
                  pl.BlockSpec((1, tn), lambda j: (0, j))],
        out_specs=pl.BlockSpec((r, tn), lambda j: (0, j)),
        out_shape=jax.ShapeDtypeStruct((r, n), F32),
        compiler_params=_cparams(("parallel",)),
        name="ada_mod",
    )(c_rows, w_ada, b_ada.reshape(1, n))


def _proj_kernel(x_ref, sh_ref, sc_ref, w_ref, o16_ref, o32_ref, xn_ref, *, nb16):
    j = pl.program_id(1)

    @pl.when(j == 0)
    def _():
        y = _ln_rows(x_ref[...]) * (1.0 + sc_ref[0]) + sh_ref[0]
        xn_ref[...] = y.astype(BF16)

    acc = jnp.dot(xn_ref[...], w_ref[...], preferred_element_type=F32)

    @pl.when(j < nb16)
    def _():
        o16_ref[...] = acc.astype(BF16)

    @pl.when(j >= nb16)
    def _():
        o32_ref[...] = acc


def _proj_call(x2d, shift, scale, w_packed, *, rows_per_group, n16):
    m, d = x2d.shape
    ntot = w_packed.shape[1]
    tm = min(PROJ_TM, rows_per_group)
    assert m % tm == 0 and rows_per_group % tm == 0 and ntot % PROJ_TN == 0 and n16 % PROJ_TN == 0
    nb16 = n16 // PROJ_TN
    nb = ntot // PROJ_TN
    grp = lambda i, j: ((i * tm) // rows_per_group, 0, 0)
    return pl.pallas_call(
        functools.partial(_proj_kernel, nb16=nb16),
        grid=(m // tm, nb),
        in_specs=[
            pl.BlockSpec((tm, d), lambda i, j: (i, 0)),
            pl.BlockSpec((1, 1, d), grp),
            pl.BlockSpec((1, 1, d), grp),
            pl.BlockSpec((d, PROJ_TN), lambda i, j: (0, j)),
        ],
        out_specs=[
            pl.BlockSpec((tm, PROJ_TN), lambda i, j: (i, jnp.minimum(j, nb16 - 1))),
            pl.BlockSpec((tm, PROJ_TN), lambda i, j: (i, jnp.maximum(j - nb16, 0))),
        ],
        out_shape=[jax.ShapeDtypeStruct((m, n16), BF16), jax.ShapeDtypeStruct((m, ntot - n16), F32)],
        scratch_shapes=[pltpu.VMEM((tm, d), BF16)],
        compiler_params=_cparams(("parallel", "arbitrary")),
        name="proj",
    )(x2d, shift, scale, w_packed)


def _pack_halves(h):
    c = h.shape[1] // 2
    lo = pltpu.bitcast(h[:, :c].astype(BF16).astype(F32), jnp.uint32)
    hi = pltpu.bitcast(h[:, c:].astype(BF16).astype(F32), jnp.uint32)
    return lax.shift_right_logical(lo, jnp.uint32(16)) | (hi & jnp.uint32(0xFFFF0000))


def _unpack_halves(w):
    lo = pltpu.bitcast(lax.shift_left(w, jnp.uint32(16)), F32)
    hi = pltpu.bitcast(w & jnp.uint32(0xFFFF0000), F32)
    return jnp.concatenate([lo, hi], axis=1).astype(BF16)


def _outproj_kernel(y0_ref, y1_ref, y2_ref, y3_ref, x_ref, g1_ref, w_ref, lng_ref, lnb_ref,
                    sh2_ref, sc2_ref, xo_ref, h_ref, hp_ref, *, alpha):
    gw = y0_ref.shape[1]
    acc = jnp.dot(y0_ref[...], w_ref[0:gw, :], preferred_element_type=F32)
    acc += jnp.dot(y1_ref[...], w_ref[gw:2 * gw, :], preferred_element_type=F32)
    acc += jnp.dot(y2_ref[...], w_ref[2 * gw:3 * gw, :], preferred_element_type=F32)
    acc += jnp.dot(y3_ref[...], w_ref[3 * gw:4 * gw, :], preferred_element_type=F32)
    r = alpha * x_ref[...] + g1_ref[0] * acc
    xn = _ln_rows(r) * lng_ref[...] + lnb_ref[...]
    xo_ref[...] = xn
    h = _ln_rows(xn) * (1.0 + sc2_ref[0]) + sh2_ref[0]
    h_ref[...] = h.astype(BF16)
    hp_ref[...] = _pack_halves(h)


def _outproj_call(ys, x2d, gate1, w_out_bf16, ln_g, ln_b, shift2, scale2, *, rows_per_group, alpha):
    m, d = x2d.shape
    gw = d // 4
    tm = min(OUT_TM, rows_per_group)
    assert m % tm == 0 and rows_per_group % tm == 0
    grp = lambda i: ((i * tm) // rows_per_group, 0, 0)
    row = lambda i: (i, 0)
    const2 = lambda i: (0, 0)
    return pl.pallas_call(
        functools.partial(_outproj_kernel, alpha=alpha),
        grid=(m // tm,),
        in_specs=[pl.BlockSpec((tm, gw), row)] * 4 + [
            pl.BlockSpec((tm, d), row),
            pl.BlockSpec((1, 1, d), grp),
            pl.BlockSpec((d, d), const2, pipeline_mode=pl.Buffered(1)),
            pl.BlockSpec((1, d), const2),
            pl.BlockSpec((1, d), const2),
            pl.BlockSpec((1, 1, d), grp),
            pl.BlockSpec((1, 1, d), grp),
        ],
        out_specs=[pl.BlockSpec((tm, d), row), pl.BlockSpec((tm, d), row), pl.BlockSpec((tm, d // 2), row)],
        out_shape=[jax.ShapeDtypeStruct((m, d), F32), jax.ShapeDtypeStruct((m, d), BF16),
                   jax.ShapeDtypeStruct((m, d // 2), jnp.uint32)],
        compiler_params=_cparams(("parallel",)),
        name="outproj",
    )(*ys, x2d, gate1, w_out_bf16, ln_g.reshape(1, d), ln_b.reshape(1, d), shift2, scale2)


def _stack_pair(q2):
    lane = lax.broadcasted_iota(jnp.int32, q2.shape, 1)
    lo = jnp.where(lane < HEAD_DIM, q2, 0.0)
    hi = jnp.where(lane >= HEAD_DIM, q2, 0.0)
    return jnp.concatenate([lo, hi], axis=0).astype(BF16)


def _unstack_pair(o):
    n = o.shape[0] // 2
    lane = lax.broadcasted_iota(jnp.int32, (n, o.shape[1]), 1)
    return jnp.where(lane < HEAD_DIM, o[:n], o[n:])


def _pair_softmax_av(qs, ks, vs, biases, masks, sink_col):
    ss = []
    for k, bia, msk in zip(ks, biases, masks):
        s = lax.dot_general(qs, k, _NT, preferred_element_type=F32)
        if bia is not None:
            s = s + bia
        if msk is not None:
            s = jnp.where(msk, s, -jnp.inf)
        ss.append(s)
    m = jnp.max(ss[0], axis=-1, keepdims=True)
    for s in ss[1:]:
        m = jnp.maximum(m, jnp.max(s, axis=-1, keepdims=True))
    if sink_col is not None:
        m = jnp.maximum(m, sink_col)
        l = jnp.exp(sink_col - m)
    else:
        l = jnp.zeros_like(m)
    o = None
    for s, v in zip(ss, vs):
        e = jnp.exp(s - m)
        l = l + jnp.sum(e, axis=-1, keepdims=True)
        pv = jnp.dot(e.astype(BF16), v, preferred_element_type=F32)
        o = pv if o is None else o + pv
    return o / l


def _na_kernel(q_ref, k_ref, v_ref, kc_ref, vc_ref, *rest, rows, kh):
    bias_refs, o_ref = rest[:-1], rest[-1]
    nwin = kh * GRID_W
    for i, bias_ref in enumerate(bias_refs):
        r = pl.program_id(1) * len(bias_refs) + i
        rs = jnp.clip(r - kh // 2, 0, rows - kh)
        start = pl.multiple_of(rs * GRID_W, GRID_W)
        qrows = slice(i * GRID_W, (i + 1) * GRID_W)
        for p in range(q_ref.shape[1] // PAIR_W):
            sl = slice(p * PAIR_W, (p + 1) * PAIR_W)
            qs = _stack_pair(q_ref[qrows, sl].astype(F32) * ATTN_SCALE)
            kw = k_ref[pl.ds(start, nwin), sl]
            vw = v_ref[pl.ds(start, nwin), sl]
            o = _pair_softmax_av(qs, [kw, kc_ref[:, sl]], [vw, vc_ref[:, sl]], [bias_ref[p], None], [None, None], None)
            o_ref[qrows, sl] = _unstack_pair(o).astype(o_ref.dtype)


def _na_bias_table(rpb, kh):
    nh = rpb.shape[0]
    c = jnp.arange(GRID_W)
    cstart = jnp.clip(c - NA_KW // 2, 0, GRID_W - NA_KW)
    valid = (c[None, :] >= cstart[:, None]) & (c[None, :] < cstart[:, None] + NA_KW)
    coff = jnp.clip(c[None, :] - c[:, None], 1 - NA_KW, NA_KW - 1) + NA_KW - 1
    roff = jnp.arange(kh)[None, :] - jnp.arange(kh)[:, None] + NA_KH - 1
    pick_r = (roff[:, :, None] == jnp.arange(rpb.shape[1])).astype(F32)
    pick_c = (coff[:, :, None] == jnp.arange(rpb.shape[2])).astype(F32)
    bias = jnp.einsum('hab,dia,ckb->hdick', rpb.astype(F32), pick_r, pick_c, precision=lax.Precision.HIGHEST)
    bias = jnp.where(valid[None, None, None], bias, -jnp.inf)
    bias = bias.transpose(1, 0, 3, 2, 4)
    return bias.reshape(kh, nh // 2, 2 * GRID_W, kh * GRID_W)


def _na_call(p16, pc16, bias_tab, *, bsz, seq, n_ctx):
    gw = p16.shape[1] // 3
    rows = seq // GRID_W
    kh = bias_tab.shape[0]
    nr = NA_ROWS_PER_STEP if rows % NA_ROWS_PER_STEP == 0 else 1
    steps = rows // nr

    def delta(i):
        def index_map(b, s):
            r = s * nr + i
            return (r - jnp.clip(r - kh // 2, 0, rows - kh), 0, 0, 0)
        return index_map

    once = pl.Buffered(1)
    return pl.pallas_call(
        functools.partial(_na_kernel, rows=rows, kh=kh),
        grid=(bsz, steps),
        in_specs=[
            pl.BlockSpec((nr * GRID_W, gw), lambda b, s: (b * steps + s, 0)),
            pl.BlockSpec((seq, gw), lambda b, s: (b, 1), pipeline_mode=once),
            pl.BlockSpec((seq, gw), lambda b, s: (b, 2), pipeline_mode=once),
            pl.BlockSpec((n_ctx, gw), lambda b, s: (b, 1)),
            pl.BlockSpec((n_ctx, gw), lambda b, s: (b, 2)),
        ] + [pl.BlockSpec((None,) + bias_tab.shape[1:], delta(i)) for i in range(nr)],
        out_specs=pl.BlockSpec((nr * GRID_W, gw), lambda b, s: (b * steps + s, 0)),
        out_shape=jax.ShapeDtypeStruct((bsz * seq, gw), BF16),
        compiler_params=_cparams(("parallel", "arbitrary")),
        name="na_attn",
    )(p16, p16, p16, pc16, pc16, *([bias_tab] * nr))


def _dup_heads(t):
    lane = lax.broadcasted_iota(jnp.int32, t.shape, 1)
    sw = pltpu.roll(t, HEAD_DIM, 1)
    return jnp.concatenate([jnp.where(lane < HEAD_DIM, t, sw), jnp.where(lane < HEAD_DIM, sw, t)], axis=1)


def _rope(t, cos, sa, sb):
    q = HEAD_DIM // 4
    return t * cos + pltpu.roll(t, q, 1) * sa + pltpu.roll(t, V7X_LANES - q, 1) * sb


def _swa_prep_kernel(q_ref, k_ref, v_ref, cos_ref, sa_ref, sb_ref, qo_ref, ko_ref, vo_ref, *, rope):
    k = k_ref[...]
    if rope:
        cos, sa, sb = cos_ref[...], sa_ref[...], sb_ref[...]
        k = _rope(k, cos, sa, sb)
    for p in range(q_ref.shape[1] // PAIR_W):
        sl = slice(p * PAIR_W, (p + 1) * PAIR_W)
        q = q_ref[:, sl]
        if rope:
            q = _rope(q, cos, sa, sb)
        qo_ref[:, sl] = (q * ATTN_SCALE).astype(BF16)
    ko_ref[...] = _dup_heads(k).astype(BF16)
    vo_ref[...] = _dup_heads(v_ref[...]).astype(BF16)


def _rope_tables(seq):
    t = jnp.arange(seq)
    row = (t // GRID_W).astype(F32)
    col = (t % GRID_W).astype(F32)
    quarter = HEAD_DIM // 4
    inv = ROPE_BASE ** (-2.0 * jnp.arange(quarter, dtype=F32) / (HEAD_DIM // 2))
    ang_r = row[:, None] * inv[None, :]
    ang_c = col[:, None] * inv[None, :]
    zero = jnp.zeros_like(ang_r)
    cos_h = jnp.concatenate([jnp.cos(ang_r)] * 2 + [jnp.cos(ang_c)] * 2, axis=1)
    sa_h = jnp.concatenate([zero, jnp.sin(ang_r), zero, jnp.sin(ang_c)], axis=1)
    sb_h = jnp.concatenate([-jnp.sin(ang_r), zero, -jnp.sin(ang_c), zero], axis=1)
    two = lambda a: jnp.concatenate([a, a], axis=1)
    return two(cos_h), two(sa_h), two(sb_h)


def _swa_prep_call(p32, tables, *, col_q, col_k, col_v, rows_per_seq, rope):
    m = p32.shape[0]
    gw = 4 * PAIR_W
    tm = _pick_tile(rows_per_seq, ELEMWISE_TM)
    nseq_tiles = rows_per_seq // tm
    tab = lambda i: (i % nseq_tiles, 0)
    return pl.pallas_call(
        functools.partial(_swa_prep_kernel, rope=rope),
        grid=(m // tm,),
        in_specs=[
            pl.BlockSpec((tm, gw), lambda i: (i, col_q // gw)),
            pl.BlockSpec((tm, PAIR_W), lambda i: (i, col_k // PAIR_W)),
            pl.BlockSpec((tm, PAIR_W), lambda i: (i, col_v // PAIR_W)),
            pl.BlockSpec((tm, PAIR_W), tab),
            pl.BlockSpec((tm, PAIR_W), tab),
            pl.BlockSpec((tm, PAIR_W), tab),
        ],
        out_specs=[pl.BlockSpec((tm, gw), lambda i: (i, 0)), pl.BlockSpec((tm, 2 * PAIR_W), lambda i: (i, 0)),
                   pl.BlockSpec((tm, 2 * PAIR_W), lambda i: (i, 0))],
        out_shape=[jax.ShapeDtypeStruct((m, gw), BF16), jax.ShapeDtypeStruct((m, 2 * PAIR_W), BF16),
                   jax.ShapeDtypeStruct((m, 2 * PAIR_W), BF16)],
        compiler_params=_cparams(("parallel",)),
        name="swa_prep",
    )(p32, p32, p32, *tables)


def _sink_col(sink_ref, p, n):
    row = lax.broadcasted_iota(jnp.int32, (2 * n, 1), 0)
    return jnp.where(row < n, sink_ref[2 * p], sink_ref[2 * p + 1])


def _swa_kernel(sink_ref, q_ref, kp_ref, kc_ref, kn_ref, vp_ref, vc_ref, vn_ref, kx_ref, vx_ref, o_ref, *, nblk):
    n = pl.program_id(1)
    blk = q_ref.shape[0]
    qi = lax.broadcasted_iota(jnp.int32, (2 * blk, blk), 0) % blk
    kj = lax.broadcasted_iota(jnp.int32, (2 * blk, blk), 1)
    m_prev = kj >= qi + jnp.where(n > 0, 0, blk)
    m_next = kj <= qi - jnp.where(n < nblk - 1, 0, blk)
    npairs = q_ref.shape[1] // PAIR_W
    for p in range(npairs):
        sl = slice(p * PAIR_W, (p + 1) * PAIR_W)
        g = p // (npairs // SWA_KV_HEADS)
        gs = slice(g * PAIR_W, (g + 1) * PAIR_W)
        qs = _stack_pair(q_ref[:, sl].astype(F32))
        o = _pair_softmax_av(
            qs, [kp_ref[:, gs], kc_ref[:, gs], kn_ref[:, gs], kx_ref[:, gs]],
            [vp_ref[:, gs], vc_ref[:, gs], vn_ref[:, gs], vx_ref[:, gs]],
            [None] * 4, [m_prev, None, m_next, None], _sink_col(sink_ref, p, blk))
        o_ref[:, sl] = _unstack_pair(o).astype(o_ref.dtype)


def _swa_call(sink, qr, kd, vd, kxd, vxd, *, bsz, seq, n_ctx):
    gw = qr.shape[1]
    kw = kd.shape[1]
    nblk = seq // SWA_BLOCK
    cur = lambda b, n: (b * nblk + n, 0)
    prev = lambda b, n: (b * nblk + jnp.maximum(n - 1, 0), 0)
    nxt = lambda b, n: (b * nblk + jnp.minimum(n + 1, nblk - 1), 0)
    cx = lambda b, n: (b, 0)
    return pl.pallas_call(
        functools.partial(_swa_kernel, nblk=nblk),
        grid=(bsz, nblk),
        in_specs=[pl.BlockSpec(memory_space=pltpu.SMEM),
                  pl.BlockSpec((SWA_BLOCK, gw), cur),
                  pl.BlockSpec((SWA_BLOCK, kw), prev), pl.BlockSpec((SWA_BLOCK, kw), cur), pl.BlockSpec((SWA_BLOCK, kw), nxt),
                  pl.BlockSpec((SWA_BLOCK, kw), prev), pl.BlockSpec((SWA_BLOCK, kw), cur), pl.BlockSpec((SWA_BLOCK, kw), nxt),
                  pl.BlockSpec((n_ctx, kw), cx), pl.BlockSpec((n_ctx, kw), cx)],
        out_specs=pl.BlockSpec((SWA_BLOCK, gw), cur),
        out_shape=jax.ShapeDtypeStruct((bsz * seq, gw), BF16),
        compiler_params=_cparams(("parallel", "arbitrary")),
        name="swa_attn",
    )(sink, qr, kd, kd, kd, vd, vd, vd, kxd, vxd)


def _ctx_attn_kernel(sink_ref, qa_ref, ka_ref, va_ref, qd_ref, kd_ref, vd_ref, oa_ref, od_ref):
    n = qa_ref.shape[0]
    npairs = qa_ref.shape[1] // PAIR_W
    for p in range(npairs):
        sl = slice(p * PAIR_W, (p + 1) * PAIR_W)
        qs = _stack_pair(qa_ref[:, sl].astype(F32) * ATTN_SCALE)
        o = _pair_softmax_av(qs, [ka_ref[:, sl]], [va_ref[:, sl]], [None], [None], None)
        oa_ref[:, sl] = _unstack_pair(o).astype(oa_ref.dtype)
    for p in range(npairs):
        sl = slice(p * PAIR_W, (p + 1) * PAIR_W)
        g = p // (npairs // SWA_KV_HEADS)
        gs = slice(g * PAIR_W, (g + 1) * PAIR_W)
        qs = _stack_pair(qd_ref[:, sl].astype(F32))
        o = _pair_softmax_av(qs, [kd_ref[:, gs]], [vd_ref[:, gs]], [None], [None], _sink_col(sink_ref, p, n))
        od_ref[:, sl] = _unstack_pair(o).astype(od_ref.dtype)


def _ctx_attn_call(sink, pc16, qx, kxd, vxd, *, bsz, n_ctx):
    gw = qx.shape[1]
    kw = kxd.shape[1]
    return pl.pallas_call(
        _ctx_attn_kernel,
        grid=(bsz,),
        in_specs=[pl.BlockSpec(memory_space=pltpu.SMEM),
                  pl.BlockSpec((n_ctx, gw), lambda b: (b, 0)), pl.BlockSpec((n_ctx, gw), lambda b: (b, 1)),
                  pl.BlockSpec((n_ctx, gw), lambda b: (b, 2)), pl.BlockSpec((n_ctx, gw), lambda b: (b, 0)),
                  pl.BlockSpec((n_ctx, kw), lambda b: (b, 0)), pl.BlockSpec((n_ctx, kw), lambda b: (b, 0))],
        out_specs=[pl.BlockSpec((n_ctx, gw), lambda b: (b, 0)), pl.BlockSpec((n_ctx, gw), lambda b: (b, 0))],
        out_shape=[jax.ShapeDtypeStruct((bsz * n_ctx, gw), BF16)] * 2,
        compiler_params=_cparams(("parallel",)),
        name="ctx_attn",
    )(sink, pc16, pc16, pc16, qx, kxd, vxd)


def _conv_kernel(u_ref, b_ref, c_ref, up_ref, cp_ref, un_ref, cn_ref, w_ref, o_ref, *, tiles_per_seq):
    i = pl.program_id(0)
    tm = u_ref.shape[0]
    pos = i % tiles_per_seq
    keep_prev = jnp.where(pos == 0, 0.0, 1.0)
    keep_next = jnp.where(pos == tiles_per_seq - 1, 0.0, 1.0)
    z = c_ref[...] * u_ref[...]
    z_prev = (cp_ref[...] * up_ref[...])[CONV_HALO - 1:CONV_HALO, :] * keep_prev
    z_next = (cn_ref[...] * un_ref[...])[0:1, :] * keep_next
    row = lax.broadcasted_iota(jnp.int32, z.shape, 0)
    zm1 = jnp.where(row == 0, z_prev, pltpu.roll(z, 1, 0))
    zp1 = jnp.where(row == tm - 1, z_next, pltpu.roll(z, tm - 1, 0))
    w = w_ref[...]
    o_ref[...] = (b_ref[...] * (w[0:1] * zm1 + w[1:2] * z + w[2:3] * zp1)).astype(o_ref.dtype)


def _conv_call(p32, conv_w, *, col_u, col_b, col_c, rows_per_seq):
    m = p32.shape[0]
    gw = conv_w.shape[1]
    tm = _pick_tile(rows_per_seq, ELEMWISE_TM)
    hb =tm // CONV_HALO
    n_halo = m // CONV_HALO
    cur = lambda col: (lambda i: (i, col // gw))
    prv = lambda col: (lambda i: (jnp.maximum(i * hb - 1, 0), col // gw))
    nxt = lambda col: (lambda i: (jnp.minimum((i + 1) * hb, n_halo - 1), col // gw))
    return pl.pallas_call(
        functools.partial(_conv_kernel, tiles_per_seq=rows_per_seq // tm),
        grid=(m // tm,),
        in_specs=[pl.BlockSpec((tm, gw), cur(col_u)), pl.BlockSpec((tm, gw), cur(col_b)), pl.BlockSpec((tm, gw), cur(col_c)),
                  pl.BlockSpec((CONV_HALO, gw), prv(col_u)), pl.BlockSpec((CONV_HALO, gw), prv(col_c)),
                  pl.BlockSpec((CONV_HALO, gw), nxt(col_u)), pl.BlockSpec((CONV_HALO, gw), nxt(col_c)),
                  pl.BlockSpec((SC_KSIZE, gw), lambda i: (0, 0))],
        out_specs=pl.BlockSpec((tm, gw), lambda i: (i, 0)),
        out_shape=jax.ShapeDtypeStruct((m, gw), BF16),
        compiler_params=_cparams(("parallel",)),
        name="short_conv",
    )(p32, p32, p32, p32, p32, p32, p32, conv_w)


def _gla_kernel(*refs, reverse, nt, fuse_out):
    if fuse_out:
        (q_ref, k_ref, v_ref, z_ref, w2_ref, gb_ref, s0_ref, of_ref, g_ref, gain_ref, o_ref, sfin_ref, st_ref) = refs
    else:
        (q_ref, k_ref, v_ref, z_ref, w2_ref, gb_ref, s0_ref, o_ref, sfin_ref, st_ref) = refs
    i = pl.program_id(1)

    @pl.when(i == 0)
    def _():
        st_ref[...] = s0_ref[...]

    tile = q_ref.shape[0]
    c_len = min(GLA_C, tile)
    dk2 = PAIR_W
    dv2 = v_ref.shape[1] // (q_ref.shape[1] // dk2)
    dv = dv2 // 2
    u = jnp.dot(z_ref[...].astype(BF16), w2_ref[...], preferred_element_type=F32) + gb_ref[...]
    la = (jnp.minimum(u, 0.0) - jnp.log1p(jnp.exp(-jnp.abs(u)))) * (1.0 / GLA_TAU)

    r_i = lax.broadcasted_iota(jnp.int32, (c_len, c_len), 0)
    c_i = lax.broadcasted_iota(jnp.int32, (c_len, c_len), 1)
    tri = (r_i <= c_i) if reverse else (r_i >= c_i)
    tri_bf = jnp.where(tri, 1.0, 0.0).astype(BF16)
    r2 = lax.broadcasted_iota(jnp.int32, (2 * c_len, c_len), 0) % c_len
    c2 = lax.broadcasted_iota(jnp.int32, (2 * c_len, c_len), 1)
    tri2 = (r2 <= c2) if reverse else (r2 >= c2)
    bd_r = lax.broadcasted_iota(jnp.int32, (dv2, dk2), 0) // dv
    bd_c = lax.broadcasted_iota(jnp.int32, (dv2, dk2), 1) // HEAD_DIM
    block_diag = bd_r == bd_c
    last_row = 0 if reverse else c_len - 1
    mid_row = c_len // 2

    n_chunks = tile // c_len
    order = range(n_chunks - 1, -1, -1) if reverse else range(n_chunks)
    for c in order:
        rows = slice(c * c_len, (c + 1) * c_len)
        la_c = la[rows]
        la_hi = la_c.astype(BF16)
        la_lo = (la_c - la_hi.astype(F32)).astype(BF16)
        cum = (jnp.dot(tri_bf, la_hi, preferred_element_type=F32)
               + jnp.dot(tri_bf, la_lo, preferred_element_type=F32))
        last = cum[last_row:last_row + 1]
        cmid = cum[mid_row:mid_row + 1]
        qc = q_ref[rows, :] * ATTN_SCALE
        kc = k_ref[rows, :]
        vc = v_ref[rows, :].astype(BF16)
        q_in = (qc * jnp.exp(cum)).astype(BF16)
        q_t = qc * jnp.exp(cum - cmid)
        k_t = (kc * jnp.exp(cmid - cum)).astype(BF16)
        k_p = (kc * jnp.exp(last - cum)).astype(BF16)
        g = jnp.exp(last)
        for p in range(q_ref.shape[1] // dk2):
            ls = slice(p * dk2, (p + 1) * dk2)
            a = lax.dot_general(_stack_pair(q_t[:, ls]), k_t[:, ls], _NT, preferred_element_type=F32)
            a = jnp.where(tri2, a, 0.0).astype(BF16)
            o0 = jnp.dot(a[:c_len], vc[:, p * dv2:p * dv2 + dv], preferred_element_type=F32)
            o1 = jnp.dot(a[c_len:], vc[:, p * dv2 + dv:(p + 1) * dv2], preferred_element_type=F32)
            st = st_ref[p]
            o_int = lax.dot_general(q_in[:, ls], st.astype(BF16), _NT, preferred_element_type=F32)
            o_p = jnp.concatenate([o0, o1], axis=1) + o_int
            upd = lax.dot_general(vc[:, p * dv2:(p + 1) * dv2], k_p[:, ls], _TN, preferred_element_type=F32)
            st_ref[p] = g[:, ls] * st + jnp.where(block_diag, upd, 0.0)
            if not fuse_out:
                o_ref[rows, p * dv2:(p + 1) * dv2] = o_p
            else:
                tot = of_ref[rows, p * dv2:(p + 1) * dv2] + o_p
                for hh in range(2):
                    hs = slice(p * dv2 + hh * dv, p * dv2 + (hh + 1) * dv)
                    oh = tot[:, hh * dv:(hh + 1) * dv]
                    on = oh * lax.rsqrt(jnp.mean(oh * oh, axis=-1, keepdims=True) + LN_EPS) * gain_ref[...]
                    o_ref[rows, hs] = (on * _silu(g_ref[rows, hs])).astype(o_ref.dtype)

    @pl.when(i == nt - 1)
    def _():
        sfin_ref[...] = st_ref[...]


def _gla_call(p, w2pad, gbias, s0, fuse, *, cols, bsz, seq, reverse):
    nq = GLA_HEADS * HEAD_DIM
    nv = s0.shape[1] * s0.shape[2]
    tile = _pick_tile(seq, GLA_TILE)
    nt = seq // tile
    tix = (lambda i: nt - 1 - i) if reverse else (lambda i: i)
    blk = lambda w, col: pl.BlockSpec((tile, w), lambda b, i: (b * nt + tix(i), col // w))
    const2 = lambda b, i: (0, 0)
    st_spec = pl.BlockSpec((None,) + s0.shape[1:], lambda b, i: (b, 0, 0, 0))
    in_specs = [blk(nq, cols['gl_q']), blk(nq, cols['gl_k']), blk(nv, cols['gl_v']), blk(V7X_LANES, cols['gl_z']),
                pl.BlockSpec(w2pad.shape, const2), pl.BlockSpec(gbias.shape, const2), st_spec]
    args = [p, p, p, p, w2pad, gbias, s0]
    if fuse is not None:
        o_other, gain = fuse
        in_specs += [blk(nv, 0), blk(nv, cols['gl_g']), pl.BlockSpec(gain.shape, const2)]
        args += [o_other, p, gain]
    out_dtype = BF16 if fuse is not None else F32
    return pl.pallas_call(
        functools.partial(_gla_kernel, reverse=reverse, nt=nt, fuse_out=fuse is not None),
        grid=(bsz, nt),
        in_specs=in_specs,
        out_specs=[blk(nv, 0), st_spec],
        out_shape=[jax.ShapeDtypeStruct((bsz * seq, nv), out_dtype), jax.ShapeDtypeStruct(s0.shape, F32)],
        scratch_shapes=[pltpu.VMEM(s0.shape[1:], F32)],
        compiler_params=_cparams(("parallel", "arbitrary")),
        name="gla_bwd" if reverse else "gla_fwd",
    )(*args)


def _gla_all(p32, pc32, w2, gb, gain, *, bsz, seq, n_ctx, d):
    cols = _p32_cols(d)
    nq = GLA_HEADS * HEAD_DIM
    dv = d // 4 // GLA_HEADS
    w2pad = [jnp.zeros((V7X_LANES, nq), F32).at[GLA_RANK * k:GLA_RANK * (k + 1)].set(w2[k]).astype(BF16) for k in range(2)]
    gbias = [gb[k].reshape(1, nq) for k in range(2)]
    s0 = jnp.zeros((bsz, GLA_HEADS // 2, 2 * dv, PAIR_W), F32)
    gain2 = gain.reshape(1, dv)
    oc_f, sc_f = _gla_call(pc32, w2pad[0], gbias[0], s0, None, cols=cols, bsz=bsz, seq=n_ctx, reverse=False)
    y_ctx, sc_b = _gla_call(pc32, w2pad[1], gbias[1], s0, (oc_f, gain2), cols=cols, bsz=bsz, seq=n_ctx, reverse=True)
    o_f, _ = _gla_call(p32, w2pad[0], gbias[0], sc_f, None, cols=cols, bsz=bsz, seq=seq, reverse=False)
    y_lat, _ = _gla_call(p32, w2pad[1], gbias[1], sc_b, (o_f, gain2), cols=cols, bsz=bsz, seq=seq, reverse=True)
    return y_lat, y_ctx


def _swiglu(x, wg, wu, wd):
    g = jnp.dot(x, wg, preferred_element_type=F32)
    u = jnp.dot(x, wu, preferred_element_type=F32)
    a = (_silu(g) * u).astype(BF16)
    return jnp.dot(a, wd, preferred_element_type=F32)


def _first_argmax(vals, idx, sentinel):
    m = jnp.max(vals, axis=0, keepdims=True)
    first = jnp.min(jnp.where(vals == m, idx, sentinel), axis=0, keepdims=True)
    return m, first


def _router_kernel(h_ref, wr_ref, rb_ref, su_ref, eidx_ref, wts_ref, rank_ref, cnt_ref, run_ref):
    i = pl.program_id(0)

    @pl.when(i == 0)
    def _():
        run_ref[...] = jnp.zeros_like(run_ref)

    t = h_ref.shape[0]
    gsz = N_EXPERTS // N_GROUPS
    logits = lax.dot_general(wr_ref[...], h_ref[...], _NT, preferred_element_type=F32)
    scores = jax.nn.sigmoid(logits)
    sel = scores + rb_ref[...]
    sub = lax.broadcasted_iota(jnp.int32, (gsz, t), 0)
    gscore = []
    for g in range(N_GROUPS):
        blk = sel[g * gsz:(g + 1) * gsz]
        m1, a1 = _first_argmax(blk, sub, gsz)
        m2 = jnp.max(jnp.where(sub == a1, -jnp.inf, blk), axis=0, keepdims=True)
        gscore.append(m1 + m2)
    gcur = jnp.concatenate(gscore, axis=0)
    gid = lax.broadcasted_iota(jnp.int32, (N_GROUPS, t), 0)
    gkeep = jnp.zeros((N_GROUPS, t), F32)
    for _ in range(TOPK_GROUPS):
        _, a = _first_argmax(gcur, gid, N_GROUPS)
        hit = gid == a
        gkeep = jnp.where(hit, 1.0, gkeep)
        gcur = jnp.where(hit, -jnp.inf, gcur)
    cur = jnp.concatenate(
        [jnp.where(gkeep[g:g + 1] > 0.0, sel[g * gsz:(g + 1) * gsz], -jnp.inf) for g in range(N_GROUPS)], axis=0)
    eid = lax.broadcasted_iota(jnp.int32, (N_EXPERTS, t), 0)
    chosen = jnp.zeros((N_EXPERTS, t), F32)
    hits, picks, wraw = [], [], []
    for _ in range(TOP_K):
        _, a = _first_argmax(cur, eid, N_EXPERTS)
        hit = eid == a
        hits.append(hit)
        picks.append(a)
        wraw.append(jnp.sum(jnp.where(hit, scores, 0.0), axis=0, keepdims=True))
        chosen = jnp.where(hit, 1.0, chosen)
        cur = jnp.where(hit, -jnp.inf, cur)
    wsum = wraw[0]
    for w in wraw[1:]:
        wsum = wsum + w
    eidx_ref[...] = jnp.concatenate(picks, axis=0)
    wts_ref[...] = jnp.concatenate([w / wsum * ROUTED_SCALE for w in wraw], axis=0)
    before = jnp.dot(chosen.astype(BF16), su_ref[...], preferred_element_type=F32) + run_ref[...][:, 0:1]
    rank_ref[...] = jnp.concatenate(
        [jnp.sum(jnp.where(hit, before, 0.0), axis=0, keepdims=True) for hit in hits], axis=0).astype(jnp.int32)
    run_ref[...] = run_ref[...] + jnp.sum(chosen, axis=1, keepdims=True)
    cnt_ref[...] = run_ref[...]


def _router_call(h, w_router, router_bias):
    n, d = h.shape
    t = _pick_tile(n, ROUTER_TILE)
    wr_t = w_router.T.astype(BF16)
    strict_upper = jnp.triu(jnp.ones((t, t), F32), 1).astype(BF16)
    const2 = lambda i: (0, 0)
    tok = lambda i: (0, i)
    eidx, wts, rank, cnt = pl.pallas_call(
        _router_kernel,
        grid=(n // t,),
        in_specs=[pl.BlockSpec((t, d), lambda i: (i, 0)), pl.BlockSpec((N_EXPERTS, d), const2),
                  pl.BlockSpec((N_EXPERTS, 1), const2), pl.BlockSpec((t, t), const2)],
        out_specs=[pl.BlockSpec((TOP_K, t), tok), pl.BlockSpec((TOP_K, t), tok), pl.BlockSpec((TOP_K, t), tok),
                   pl.BlockSpec((N_EXPERTS, V7X_LANES), const2)],
        out_shape=[jax.ShapeDtypeStruct((TOP_K, n), jnp.int32), jax.ShapeDtypeStruct((TOP_K, n), F32),
                   jax.ShapeDtypeStruct((TOP_K, n), jnp.int32), jax.ShapeDtypeStruct((N_EXPERTS, V7X_LANES), F32)],
        scratch_shapes=[pltpu.VMEM((N_EXPERTS, V7X_LANES), F32)],
        compiler_params=_cparams(("arbitrary",)),
        name="router",
    )(h, wr_t, router_bias.reshape(N_EXPERTS, 1).astype(F32), strict_upper)
    return eidx, wts, rank, cnt[:, 0].astype(jnp.int32)


def _dispatch_kernel(zoff_ref, nused_ref, dest_ref, hp_ref, xs_hbm, zbuf, sem, zsem):
    i = pl.program_id(0)
    t = dest_ref.shape[1]
    n_blocks = xs_hbm.shape[0] // MOE_TB

    def clear_block(off):
        return pltpu.make_async_copy(zbuf, xs_hbm.at[pl.ds(pl.multiple_of(off, MOE_TB), MOE_TB)], zsem)

    @pl.when(i == 0)
    def _():
        zbuf[...] = jnp.zeros_like(zbuf)
        for e in range(N_EXPERTS):
            clear_block(zoff_ref[e]).start()
        for e in range(N_EXPERTS):
            clear_block(0).wait()

        def clear_tail(b, carry):
            cp = clear_block(b * MOE_TB)
            cp.start()
            cp.wait()
            return carry

        lax.fori_loop(nused_ref[0], n_blocks, clear_tail, 0)

    def row_copy(j, slot):
        return pltpu.make_async_copy(hp_ref.at[pl.ds(j, 1)], xs_hbm.at[pl.ds(slot, 1)], sem)

    for j in range(t):
        for k in range(TOP_K):
            row_copy(j, dest_ref[k, j]).start(priority=k % 2)
    for _ in range(t * TOP_K):
        row_copy(0, 0).wait()


def _dispatch_call(zero_off, n_used, dest, h_packed, n_rows):
    n, c = h_packed.shape
    t = _pick_tile(n, DISPATCH_TILE)
    grid_spec = pltpu.PrefetchScalarGridSpec(
        num_scalar_prefetch=2,
        grid=(n // t,),
        in_specs=[pl.BlockSpec((TOP_K, t), lambda i, z, nu: (0, i), memory_space=pltpu.SMEM),
                  pl.BlockSpec((t, c), lambda i, z, nu: (i, 0))],
        out_specs=pl.BlockSpec(memory_space=pl.ANY),
        scratch_shapes=[pltpu.VMEM((MOE_TB, c), jnp.uint32), pltpu.SemaphoreType.DMA, pltpu.SemaphoreType.DMA],
    )
    return pl.pallas_call(
        _dispatch_kernel,
        grid_spec=grid_spec,
        out_shape=jax.ShapeDtypeStruct((n_rows, c), jnp.uint32),
        compiler_params=_cparams(("arbitrary",)),
        name="moe_dispatch",
    )(zero_off, n_used, dest, h_packed)


def _expert_kernel(be_ref, nused_ref, first_ref, slot_ref, nexte_ref, x_ref, oprev_ref, wg_hbm, wu_hbm, wd_hbm,
                   y_out, wg_f, wu_f, wd_f, wg_s, wu_s, wd_s, sem, ybuf, ssem, *, layer, dump_row):
    b = pl.program_id(0)

    def weight_copies(e, slot):
        return (pltpu.make_async_copy(wg_hbm.at[layer, e], wg_f.at[slot], sem.at[slot, 0]),
                pltpu.make_async_copy(wu_hbm.at[layer, e], wu_f.at[slot], sem.at[slot, 1]),
                pltpu.make_async_copy(wd_hbm.at[layer, e], wd_f.at[slot], sem.at[slot, 2]))

    @pl.when(b == 0)
    def _():
        for cp in weight_copies(be_ref[0], 0):
            cp.start()

    @pl.when(first_ref[b] == 1)
    def _():
        slot = slot_ref[b]
        for cp in weight_copies(be_ref[b], slot):
            cp.wait()

        @pl.when(nexte_ref[b] >= 0)
        def _():
            for cp in weight_copies(nexte_ref[b], 1 - slot):
                cp.start()

        wg_s[...] = wg_f[slot].astype(BF16)
        wu_s[...] = wu_f[slot].astype(BF16)
        wd_s[...] = wd_f[slot].astype(BF16)

    nu = nused_ref[0]
    tb = x_ref.shape[0]

    def scatter_copy(src_slot, r, dst_row):
        return pltpu.make_async_copy(ybuf.at[src_slot, pl.ds(r, 1)], y_out.at[pl.ds(dst_row, 1)], ssem.at[src_slot])

    def wait_scatter(src_slot):
        for r in range(tb):
            scatter_copy(src_slot, r, 0).wait()

    @pl.when(b == 0)
    def _():
        ybuf[...] = jnp.zeros_like(ybuf)

    @pl.when((b >= 1) & (b <= nu))
    def _():
        wait_scatter(b % 2)

    @pl.when(b < nu)
    def _():
        prev_slot = (b + 1) % 2

        def start_rows(lo, hi):
            for r in range(lo, hi):
                dst = jnp.where(b == 0, dump_row + r, oprev_ref[0, 0, r])
                scatter_copy(prev_slot, r, dst).start(priority=r % 2)

        n_pieces = 6
        cuts = [tb * p // n_pieces for p in range(n_pieces + 1)]
        x = _unpack_halves(x_ref[...])
        half_e = wg_s.shape[1] // 2
        acts = []
        for p in range(2):
            start_rows(cuts[p], cuts[p + 1])
            cs = slice(p * half_e, (p + 1) * half_e)
            g = jnp.dot(x, wg_s[:, cs], preferred_element_type=F32)
            u = jnp.dot(x, wu_s[:, cs], preferred_element_type=F32)
            acts.append((_silu(g) * u).astype(BF16))
        a = jnp.concatenate(acts, axis=1)
        quarter = wd_s.shape[1] // 4
        ys = []
        for p in range(4):
            start_rows(cuts[2 + p], cuts[3 + p])
            ys.append(jnp.dot(a, wd_s[:, p * quarter:(p + 1) * quarter], preferred_element_type=F32))
        ybuf[b % 2] = _pack_halves(jnp.concatenate(ys, axis=1))

    @pl.when(b == nu)
    def _():
        last_slot = (b + 1) % 2
        for r in range(tb):
            scatter_copy(last_slot, r, oprev_ref[0, 0, r]).start(priority=r % 2)
        wait_scatter(last_slot)


def _expert_call(block_e, n_used, run_first, run_slot, next_e, x_sorted, out_rows, n_out_rows, wg, wu, wd, layer):
    n_rows, c = x_sorted.shape
    _, _, d, de = wg.shape
    n_blocks = n_rows // MOE_TB
    xsel = lambda b, be, nu, *_: (jnp.minimum(b, nu[0] - 1), 0)
    osel = lambda b, *_: (jnp.clip(b - 1, 0, n_blocks - 1), 0, 0)
    hbm = pl.BlockSpec(memory_space=pl.ANY)
    grid_spec = pltpu.PrefetchScalarGridSpec(
        num_scalar_prefetch=5,
        grid=(n_blocks + 1,),
        in_specs=[pl.BlockSpec((MOE_TB, c), xsel), pl.BlockSpec((1, 1, MOE_TB), osel, memory_space=pltpu.SMEM),
                  hbm, hbm, hbm],
        out_specs=hbm,
        scratch_shapes=[pltpu.VMEM((2, d, de), F32), pltpu.VMEM((2, d, de), F32), pltpu.VMEM((2, de, d), F32),
                        pltpu.VMEM((d, de), BF16), pltpu.VMEM((d, de), BF16), pltpu.VMEM((de, d), BF16),
                        pltpu.SemaphoreType.DMA((2, 3)),
                        pltpu.VMEM((2, MOE_TB, c), jnp.uint32), pltpu.SemaphoreType.DMA((2,))],
    )
    return pl.pallas_call(
        functools.partial(_expert_kernel, layer=layer, dump_row=n_out_rows - MOE_TB),
        grid_spec=grid_spec,
        out_shape=jax.ShapeDtypeStruct((n_out_rows, c), jnp.uint32),
        compiler_params=_cparams(("arbitrary",)),
        name="experts",
    )(*(jnp.pad(a, (0, 1)) for a in (block_e,)), n_used, *(jnp.pad(a, (0, 1)) for a in (run_first, run_slot, next_e)),
      x_sorted, out_rows, wg, wu, wd)


def _combine_kernel(*refs, alpha):
    (w_ref, h_ref, x_ref, g2_ref, wsg_ref, wsu_ref, wsd_ref, lng_ref, lnb_ref), y_refs, o_ref = (
        refs[:9], refs[9:9 + TOP_K], refs[9 + TOP_K])
    acc = _swiglu(h_ref[...], wsg_ref[...], wsu_ref[...], wsd_ref[...])
    w = w_ref[...]
    half = acc.shape[1] // 2
    lo, hi = acc[:, :half], acc[:, half:]
    for k in range(TOP_K):
        yk = y_refs[k][...]
        wk = w[:, k:k + 1]
        lo = lo + wk * pltpu.bitcast(lax.shift_left(yk, jnp.uint32(16)), F32)
        hi = hi + wk * pltpu.bitcast(yk & jnp.uint32(0xFFFF0000), F32)
    acc = jnp.concatenate([lo, hi], axis=1)
    r = alpha * x_ref[...] + g2_ref[0] * acc
    o_ref[...] = _ln_rows(r) * lng_ref[...] + lnb_ref[...]


def _combine_call(wts_t, h, x2d, gate2, wsg, wsu, wsd, ln_g, ln_b, y_rows, *, row0, rows_per_group, alpha):
    m, d = x2d.shape
    n = h.shape[0]
    de = wsg.shape[1]
    t = _pick_tile(min(rows_per_group, m), COMBINE_TILE)
    assert row0 % t == 0 and rows_per_group % t == 0 and n % t == 0
    t0 = row0 // t
    const2 = lambda i: (0, 0)
    once = pl.Buffered(1)
    y_specs = [pl.BlockSpec((t, d // 2), functools.partial(lambda i, k: (k * (n // t) + t0 + i, 0), k=k))
               for k in range(TOP_K)]
    return pl.pallas_call(
        functools.partial(_combine_kernel, alpha=alpha),
        grid=(m // t,),
        in_specs=[pl.BlockSpec((t, TOP_K), lambda i: (t0 + i, 0)),
                  pl.BlockSpec((t, d), lambda i: (t0 + i, 0)),
                  pl.BlockSpec((t, d), lambda i: (i, 0)),
                  pl.BlockSpec((1, 1, d), lambda i: ((i * t) // rows_per_group, 0, 0)),
                  pl.BlockSpec((d, de), const2, pipeline_mode=once),
                  pl.BlockSpec((d, de), const2, pipeline_mode=once),
                  pl.BlockSpec((de, d), const2, pipeline_mode=once),
                  pl.BlockSpec((1, d), const2), pl.BlockSpec((1, d), const2)] + y_specs,
        out_specs=pl.BlockSpec((t, d), lambda i: (i, 0)),
        out_shape=jax.ShapeDtypeStruct((m, d), F32),
        compiler_params=_cparams(("parallel",)),
        name="moe_combine",
    )(wts_t, h, x2d, gate2, wsg, wsu, wsd, ln_g.reshape(1, d), ln_b.reshape(1, d), *([y_rows] * TOP_K))


def _moe_routed(h, h_packed, w_router, router_bias, w_eg, w_eu, w_ed, layer):
    n = h.shape[0]
    eidx, wts, rank, counts = _router_call(h, w_router, router_bias)
    padded = (counts + MOE_TB - 1) // MOE_TB * MOE_TB
    pad_end = jnp.cumsum(padded)
    pad_start = pad_end - padded
    n_blocks = (n * TOP_K + N_EXPERTS * (MOE_TB - 1) + MOE_TB - 1) // MOE_TB
    block_first = jnp.arange(n_blocks, dtype=jnp.int32) * MOE_TB
    block_e = jnp.minimum(jnp.sum((pad_end[None, :] <= block_first[:, None]).astype(jnp.int32), axis=1), N_EXPERTS - 1)
    n_used = (pad_end[-1:] // MOE_TB).astype(jnp.int32)
    zero_off = jnp.maximum(pad_end - MOE_TB, 0).astype(jnp.int32)
    expert_ids = jnp.arange(N_EXPERTS, dtype=jnp.int32)
    first_row = jnp.sum(jnp.where(eidx[:, :, None] == expert_ids, pad_start.astype(jnp.int32), 0), axis=-1)
    dest = first_row + rank
    x_sorted = _dispatch_call(zero_off, n_used, dest, h_packed, n_blocks * MOE_TB)
    blk = jnp.arange(n_blocks, dtype=jnp.int32)
    used = blk < n_used[0]
    run_first = ((block_e != jnp.concatenate([jnp.full((1,), -1, jnp.int32), block_e[:-1]])) & used).astype(jnp.int32)
    run_slot = ((jnp.cumsum(run_first) - 1) % 2).astype(jnp.int32)
    run_end = jnp.sum(jnp.where(block_e[:, None] == expert_ids, pad_end.astype(jnp.int32), 0), axis=1) // MOE_TB
    next_e = jnp.sum(jnp.where(blk[None, :] == run_end[:, None], block_e[None, :], 0), axis=1)
    next_e = jnp.where(used & (run_end < n_used[0]), next_e, -1).astype(jnp.int32)
    nk = n * TOP_K
    n_rows = n_blocks * MOE_TB
    n_pad = n_rows - nk
    pad_need = (padded - counts).astype(jnp.int32)
    need_end = jnp.cumsum(pad_need)
    j = jnp.arange(n_pad, dtype=jnp.int32)
    owner = jnp.sum((need_end[None, :] <= j[:, None]).astype(jnp.int32), axis=1)
    sel = owner[:, None] == expert_ids
    pick = lambda tab: jnp.sum(jnp.where(sel, tab.astype(jnp.int32), 0), axis=1)
    pad_slot = jnp.where(owner < N_EXPERTS,
                         pick(pad_start) + pick(counts) + j - (pick(need_end) - pick(pad_need)),
                         pad_end[-1].astype(jnp.int32) + j - need_end[-1])
    slots = jnp.concatenate([dest.reshape(-1), pad_slot])
    rows = jnp.arange(n_rows, dtype=jnp.int32)
    _, out_rows = lax.sort((slots, rows), num_keys=1)
    y_rows = _expert_call(block_e, n_used, run_first, run_slot, next_e, x_sorted,
                          out_rows.reshape(n_blocks, 1, MOE_TB), n_rows + MOE_TB, w_eg, w_eu, w_ed, layer)
    return wts.T, y_rows


def _pack_w_in(w, d):
    gw = d // 4
    cuts = np.cumsum([gw, gw, gw, gw, gw, gw, gw // 2, gw // 2, gw, gw, 2 * GLA_RANK, gw, gw // 4, gw // 4])[:-1].tolist()
    (na_q, na_k, na_v, sc_u, sc_b, sc_c, gl_q, gl_k, gl_v, gl_g, gl_z, sw_q, sw_k, sw_v) = jnp.split(w, cuts, axis=1)
    zpad = jnp.zeros((d, V7X_LANES - 2 * GLA_RANK), w.dtype)
    cols = [na_q, na_k, na_v, sc_u, sc_b, sc_c, gl_v, gl_g, sw_q, gl_q, gl_k, sw_k, sw_v, gl_z, zpad]
    packed = jnp.concatenate(cols, axis=1)
    pad = (-packed.shape[1]) % PROJ_TN
    packed = jnp.pad(packed, ((0, 0), (0, pad)))
    return packed.astype(BF16)


def _p32_cols(d):
    gw = d // 4
    cols, c = {}, 0
    for name, wdt in (('sc_u', gw), ('sc_b', gw), ('sc_c', gw), ('gl_v', gw), ('gl_g', gw), ('sw_q', gw),
                      ('gl_q', gw // 2), ('gl_k', gw // 2), ('sw_k', gw // 4), ('sw_v', gw // 4), ('gl_z', V7X_LANES)):
        cols[name] = c
        c += wdt
    return cols


def _token_mixers(p16, p32, pc16, pc32, rpb, conv_w, gla_w2, gla_b, gla_norm_g, sink, with_ctx_out, *,
                  bsz, seq, n_ctx, d):
    cols = _p32_cols(d)
    kh = min(NA_KH, seq // GRID_W)
    y_na = _na_call(p16, pc16, _na_bias_table(rpb, kh), bsz=bsz, seq=seq, n_ctx=n_ctx)
    conv_cols = dict(col_u=cols['sc_u'], col_b=cols['sc_b'], col_c=cols['sc_c'])
    y_sc = _conv_call(p32, conv_w, rows_per_seq=seq, **conv_cols)
    y_gl, yc_gl = _gla_all(p32, pc32, gla_w2, gla_b, gla_norm_g, bsz=bsz, seq=seq, n_ctx=n_ctx, d=d)
    tables = _rope_tables(seq)
    swa_cols = dict(col_q=cols['sw_q'], col_k=cols['sw_k'], col_v=cols['sw_v'])
    qr, kd, vd = _swa_prep_call(p32, tables, rows_per_seq=seq, rope=True, **swa_cols)
    ctx_tables = tuple(t[:n_ctx] for t in tables)
    qx, kxd, vxd = _swa_prep_call(pc32, ctx_tables, rows_per_seq=n_ctx, rope=False, **swa_cols)
    y_sw = _swa_call(sink, qr, kd, vd, kxd, vxd, bsz=bsz, seq=seq, n_ctx=n_ctx)
    y_lat = (y_na, y_sc, y_gl, y_sw)
    if not with_ctx_out:
        return y_lat, None
    yc_na, yc_sw = _ctx_attn_call(sink, pc16, qx, kxd, vxd, bsz=bsz, n_ctx=n_ctx)
    yc_sc = _conv_call(pc32, conv_w, rows_per_seq=n_ctx, **conv_cols)
    return y_lat, (yc_na, yc_sc, yc_gl, yc_sw)


def kernel(x, c, ctx, c_ctx, w_ada, b_ada, w_in, na_rpb, conv_w, gla_w2, gla_b, gla_norm_g, swa_sink, w_out,
           ln1_g, ln1_b, w_router, router_bias, w_exp_gate, w_exp_up, w_exp_down, w_sh_gate, w_sh_up, w_sh_down,
           ln2_g, ln2_b):
    bsz, seq, d = x.shape
    n_ctx = ctx.shape[1]
    nc = bsz * n_ctx
    depth = w_in.shape[0]
    alpha = (2 * depth) ** 0.25
    n16 = 3 * (d // 4)
    x2 = x.reshape(bsz * seq, d)
    hc2 = ctx.reshape(nc, d)
    c_rows = jnp.zeros((V7X_SUBLANES, d), F32).at[:bsz].set(c).at[bsz].set(c_ctx)
    for layer in range(depth):
        last = layer == depth - 1
        mod = _ada_call(c_rows, w_ada, b_ada[layer], layer)
        sh1, sc1, g1, sh2, sc2, g2 = [t[:bsz, None, :] for t in jnp.split(mod, 6, axis=-1)]
        sh1c, sc1c, g1c, sh2c, sc2c, g2c = [t[bsz:bsz + 1, None, :] for t in jnp.split(mod, 6, axis=-1)]
        w_p = _pack_w_in(w_in[layer], d)
        p16, p32 = _proj_call(x2, sh1, sc1, w_p, rows_per_group=seq, n16=n16)
        pc16, pc32 = _proj_call(hc2, sh1c, sc1c, w_p, rows_per_group=nc, n16=n16)
        y_lat, y_ctx = _token_mixers(p16, p32, pc16, pc32, na_rpb[layer], conv_w[layer], gla_w2[layer], gla_b[layer],
                                     gla_norm_g[layer], swa_sink[layer], not last, bsz=bsz, seq=seq, n_ctx=n_ctx, d=d)
        w_o = w_out[layer].astype(BF16)
        x2, h_lat, hp_lat = _outproj_call(y_lat, x2, g1, w_o, ln1_g[layer], ln1_b[layer], sh2, sc2,
                                          rows_per_group=seq, alpha=alpha)
        shared_w = (w_sh_gate[layer].astype(BF16), w_sh_up[layer].astype(BF16), w_sh_down[layer].astype(BF16))
        route_w = (w_router[layer], router_bias[layer], w_exp_gate, w_exp_up, w_exp_down, layer)
        if last:
            wts_t, y_rows = _moe_routed(h_lat, hp_lat, *route_w)
            x2 = _combine_call(wts_t, h_lat, x2, g2, *shared_w, ln2_g[layer], ln2_b[layer], y_rows,
                               row0=0, rows_per_group=seq, alpha=alpha)
        else:
            hc2, h_ctx, hp_ctx = _outproj_call(y_ctx, hc2, g1c, w_o, ln1_g[layer], ln1_b[layer], sh2c, sc2c,
                                               rows_per_group=nc, alpha=alpha)
            h_all = jnp.concatenate([h_ctx, h_lat], axis=0)
            hp_all = jnp.concatenate([hp_ctx, hp_lat], axis=0)
            wts_t, y_rows = _moe_routed(h_all, hp_all, *route_w)
            hc2 = _combine_call(wts_t, h_all, hc2, g2c, *shared_w, ln2_g[layer], ln2_b[layer], y_rows,
                                row0=0, rows_per_group=nc, alpha=alpha)
            x2 = _combine_call(wts_t, h_all, x2, g2, *shared_w, ln2_g[layer], ln2_b[layer], y_rows,
                               row0=nc, rows_per_group=seq, alpha=alpha)
    return x2.reshape(bsz, seq, d)
```

```python
import functools

import jax
import jax.numpy as jnp
import numpy as np
from jax import lax
from jax.experimental import pallas as pl
from jax.experimental.pallas import tpu as pltpu

GRID_W = 64
HEAD_DIM = 64
NA_KH = 8
NA_KW = 16
SC_KSIZE = 3
GLA_HEADS = 4
GLA_RANK = 16
GLA_TAU = 16.0
SWA_KV_HEADS = 2
SWA_BLOCK = 128
ROPE_BASE = 10000.0
N_EXPERTS = 64
TOP_K = 8
N_GROUPS = 8
TOPK_GROUPS = 4
ROUTED_SCALE = 2.5
LN_EPS = 1e-6

V7X_LANES = 128
V7X_SUBLANES = 8
V7X_VMEM_LIMIT_BYTES = 48 * 1024 * 1024

PROJ_TM = 1024
PROJ_TN = 512
OUT_TM = 512
MOE_TB = 256
ROUTER_TILE = 512
DISPATCH_TILE = 256
COMBINE_TILE = 256
ELEMWISE_TM = 512
NA_ROWS_PER_STEP = 4
CONV_HALO = 8
GLA_C = 64
GLA_TILE = 512
ADA_TN = 2048

F32 = jnp.float32
BF16 = jnp.bfloat16
PAIR_W = 2 * HEAD_DIM
ATTN_SCALE = HEAD_DIM ** -0.5
_NT = (((1,), (1,)), ((), ()))
_TN = (((0,), (0,)), ((), ()))


def _cparams(sem):
    return pltpu.CompilerParams(dimension_semantics=sem, vmem_limit_bytes=V7X_VMEM_LIMIT_BYTES)


def _pick_tile(m, preferred):
    t = preferred
    while t > 8 and m % t:
        t //= 2
    assert m % t == 0, (m, preferred)
    return t


def _ln_rows(x):
    mu = jnp.mean(x, axis=-1, keepdims=True)
    xc = x - mu
    var = jnp.mean(xc * xc, axis=-1, keepdims=True)
    return xc * lax.rsqrt(var + LN_EPS)


def _silu(x):
    return x * jax.nn.sigmoid(x)


def _ada_kernel(c_ref, w_ref, b_ref, o_ref):
    a = _silu(c_ref[...]).astype(BF16)
    o_ref[...] = jnp.dot(a, w_ref[...].astype(BF16), preferred_element_type=F32) + b_ref[...]


def _ada_call(c_rows, w_ada, b_ada, layer):
    r, d = c_rows.shape
    n = w_ada.shape[2]
    tn = _pick_tile(n, ADA_TN)
    return pl.pallas_call(
        _ada_kernel,
        grid=(n // tn,),
        in_specs=[pl.BlockSpec((r, d), lambda j: (0, 0)), pl.BlockSpec((None, d, tn), lambda j: (layer, 0, j)),
                  pl.BlockSpec((1, tn), lambda j: (0, j))],
        out_specs=pl.BlockSpec((r, tn), lambda j: (0, j)),
        out_shape=jax.ShapeDtypeStruct((r, n), F32),
        compiler_params=_cparams(("parallel",)),
        name="ada_mod",
    )(c_rows, w_ada, b_ada.reshape(1, n))


def _proj_kernel(x_ref, sh_ref, sc_ref, w_ref, o16_ref, o32_ref, xn_ref, *, nb16):
    j = pl.program_id(1)

    @pl.when(j == 0)
    def _():
        y = _ln_rows(x_ref[...]) * (1.0 + sc_ref[0]) + sh_ref[0]
        xn_ref[...] = y.astype(BF16)

    acc = jnp.dot(xn_ref[...], w_ref[...], preferred_element_type=F32)

    @pl.when(j < nb16)
    def _():
        o16_ref[...] = acc.astype(BF16)

    @pl.when(j >= nb16)
    def _():
        o32_ref[...] = acc


def _proj_call(x2d, shift, scale, w_packed, *, rows_per_group, n16):
    m, d = x2d.shape
    ntot = w_packed.shape[1]
    tm = min(PROJ_TM, rows_per_group)
    assert m % tm == 0 and rows_per_group % tm == 0 and ntot % PROJ_TN == 0 and n16 % PROJ_TN == 0
    nb16 = n16 // PROJ_TN
    nb = ntot // PROJ_TN
    grp = lambda i, j: ((i * tm) // rows_per_group, 0, 0)
    return pl.pallas_call(
        functools.partial(_proj_kernel, nb16=nb16),
        grid=(m // tm, nb),
        in_specs=[
            pl.BlockSpec((tm, d), lambda i, j: (i, 0)),
            pl.BlockSpec((1, 1, d), grp),
            pl.BlockSpec((1, 1, d), grp),
            pl.BlockSpec((d, PROJ_TN), lambda i, j: (0, j)),
        ],
        out_specs=[
            pl.BlockSpec((tm, PROJ_TN), lambda i, j: (i, jnp.minimum(j, nb16 - 1))),
            pl.BlockSpec((tm, PROJ_TN), lambda i, j: (i, jnp.maximum(j - nb16, 0))),
        ],
        out_shape=[jax.ShapeDtypeStruct((m, n16), BF16), jax.ShapeDtypeStruct((m, ntot - n16), F32)],
        scratch_shapes=[pltpu.VMEM((tm, d), BF16)],
        compiler_params=_cparams(("parallel", "arbitrary")),
        name="proj",
    )(x2d, shift, scale, w_packed)


def _pack_halves(h):
    c = h.shape[1] // 2
    lo = pltpu.bitcast(h[:, :c].astype(BF16).astype(F32), jnp.uint32)
    hi = pltpu.bitcast(h[:, c:].astype(BF16).astype(F32), jnp.uint32)
    return lax.shift_right_logical(lo, jnp.uint32(16)) | (hi & jnp.uint32(0xFFFF0000))


def _unpack_halves(w):
    lo = pltpu.bitcast(lax.shift_left(w, jnp.uint32(16)), F32)
    hi = pltpu.bitcast(w & jnp.uint32(0xFFFF0000), F32)
    return jnp.concatenate([lo, hi], axis=1).astype(BF16)


def _outproj_kernel(y0_ref, y1_ref, y2_ref, y3_ref, x_ref, g1_ref, w_ref, lng_ref, lnb_ref,
                    sh2_ref, sc2_ref, xo_ref, h_ref, hp_ref, *, alpha):
    gw = y0_ref.shape[1]
    acc = jnp.dot(y0_ref[...], w_ref[0:gw, :], preferred_element_type=F32)
    acc += jnp.dot(y1_ref[...], w_ref[gw:2 * gw, :], preferred_element_type=F32)
    acc += jnp.dot(y2_ref[...], w_ref[2 * gw:3 * gw, :], preferred_element_type=F32)
    acc += jnp.dot(y3_ref[...], w_ref[3 * gw:4 * gw, :], preferred_element_type=F32)
    r = alpha * x_ref[...] + g1_ref[0] * acc
    xn = _ln_rows(r) * lng_ref[...] + lnb_ref[...]
    xo_ref[...] = xn
    h = _ln_rows(xn) * (1.0 + sc2_ref[0]) + sh2_ref[0]
    h_ref[...] = h.astype(BF16)
    hp_ref[...] = _pack_halves(h)


def _outproj_call(ys, x2d, gate1, w_out_bf16, ln_g, ln_b, shift2, scale2, *, rows_per_group, alpha):
    m, d = x2d.shape
    gw = d // 4
    tm = min(OUT_TM, rows_per_group)
    assert m % tm == 0 and rows_per_group % tm == 0
    grp = lambda i: ((i * tm) // rows_per_group, 0, 0)
    row = lambda i: (i, 0)
    const2 = lambda i: (0, 0)
    return pl.pallas_call(
        functools.partial(_outproj_kernel, alpha=alpha),
        grid=(m // tm,),
        in_specs=[pl.BlockSpec((tm, gw), row)] * 4 + [
            pl.BlockSpec((tm, d), row),
            pl.BlockSpec((1, 1, d), grp),
            pl.BlockSpec((d, d), const2, pipeline_mode=pl.Buffered(1)),
            pl.BlockSpec((1, d), const2),
            pl.BlockSpec((1, d), const2),
            pl.BlockSpec((1, 1, d), grp),
            pl.BlockSpec((1, 1, d), grp),
        ],
        out_specs=[pl.BlockSpec((tm, d), row), pl.BlockSpec((tm, d), row), pl.BlockSpec((tm, d // 2), row)],
        out_shape=[jax.ShapeDtypeStruct((m, d), F32), jax.ShapeDtypeStruct((m, d), BF16),
                   jax.ShapeDtypeStruct((m, d // 2), jnp.uint32)],
        compiler_params=_cparams(("parallel",)),
        name="outproj",
    )(*ys, x2d, gate1, w_out_bf16, ln_g.reshape(1, d), ln_b.reshape(1, d), shift2, scale2)


def _stack_pair(q2):
    lane = lax.broadcasted_iota(jnp.int32, q2.shape, 1)
    lo = jnp.where(lane < HEAD_DIM, q2, 0.0)
    hi = jnp.where(lane >= HEAD_DIM, q2, 0.0)
    return jnp.concatenate([lo, hi], axis=0).astype(BF16)


def _unstack_pair(o):
    n = o.shape[0] // 2
    lane = lax.broadcasted_iota(jnp.int32, (n, o.shape[1]), 1)
    return jnp.where(lane < HEAD_DIM, o[:n], o[n:])


def _pair_softmax_av(qs, ks, vs, biases, masks, sink_col):
    ss = []
    for k, bia, msk in zip(ks, biases, masks):
        s = lax.dot_general(qs, k, _NT, preferred_element_type=F32)
        if bia is not None:
            s = s + bia
        if msk is not None:
            s = jnp.where(msk, s, -jnp.inf)
        ss.append(s)
    m = jnp.max(ss[0], axis=-1, keepdims=True)
    for s in ss[1:]:
        m = jnp.maximum(m, jnp.max(s, axis=-1, keepdims=True))
    if sink_col is not None:
        m = jnp.maximum(m, sink_col)
        l = jnp.exp(sink_col - m)
    else:
        l = jnp.zeros_like(m)
    o = None
    for s, v in zip(ss, vs):
        e = jnp.exp(s - m)
        l = l + jnp.sum(e, axis=-1, keepdims=True)
        pv = jnp.dot(e.astype(BF16), v, preferred_element_type=F32)
        o = pv if o is None else o + pv
    return o / l


def _na_kernel(q_ref, k_ref, v_ref, kc_ref, vc_ref, *rest, rows, kh):
    bias_refs, o_ref = rest[:-1], rest[-1]
    nwin = kh * GRID_W
    for i, bias_ref in enumerate(bias_refs):
        r = pl.program_id(1) * len(bias_refs) + i
        rs = jnp.clip(r - kh // 2, 0, rows - kh)
        start = pl.multiple_of(rs * GRID_W, GRID_W)
        qrows = slice(i * GRID_W, (i + 1) * GRID_W)
        for p in range(q_ref.shape[1] // PAIR_W):
            sl = slice(p * PAIR_W, (p + 1) * PAIR_W)
            qs = _stack_pair(q_ref[qrows, sl].astype(F32) * ATTN_SCALE)
            kw = k_ref[pl.ds(start, nwin), sl]
            vw = v_ref[pl.ds(start, nwin), sl]
            o = _pair_softmax_av(qs, [kw, kc_ref[:, sl]], [vw, vc_ref[:, sl]], [bias_ref[p], None], [None, None], None)
            o_ref[qrows, sl] = _unstack_pair(o).astype(o_ref.dtype)


def _na_bias_table(rpb, kh):
    nh = rpb.shape[0]
    c = jnp.arange(GRID_W)
    cstart = jnp.clip(c - NA_KW // 2, 0, GRID_W - NA_KW)
    valid = (c[None, :] >= cstart[:, None]) & (c[None, :] < cstart[:, None] + NA_KW)
    coff = jnp.clip(c[None, :] - c[:, None], 1 - NA_KW, NA_KW - 1) + NA_KW - 1
    roff = jnp.arange(kh)[None, :] - jnp.arange(kh)[:, None] + NA_KH - 1
    pick_r = (roff[:, :, None] == jnp.arange(rpb.shape[1])).astype(F32)
    pick_c = (coff[:, :, None] == jnp.arange(rpb.shape[2])).astype(F32)
    bias = jnp.einsum('hab,dia,ckb->hdick', rpb.astype(F32), pick_r, pick_c, precision=lax.Precision.HIGHEST)
    bias = jnp.where(valid[None, None, None], bias, -jnp.inf)
    bias = bias.transpose(1, 0, 3, 2, 4)
    return bias.reshape(kh, nh // 2, 2 * GRID_W, kh * GRID_W)


def _na_call(p16, pc16, bias_tab, *, bsz, seq, n_ctx):
    gw = p16.shape[1] // 3
    rows = seq // GRID_W
    kh = bias_tab.shape[0]
    nr = NA_ROWS_PER_STEP if rows % NA_ROWS_PER_STEP == 0 else 1
    steps = rows // nr

    def delta(i):
        def index_map(b, s):
            r = s * nr + i
            return (r - jnp.clip(r - kh // 2, 0, rows - kh), 0, 0, 0)
        return index_map

    once = pl.Buffered(1)
    return pl.pallas_call(
        functools.partial(_na_kernel, rows=rows, kh=kh),
        grid=(bsz, steps),
        in_specs=[
            pl.BlockSpec((nr * GRID_W, gw), lambda b, s: (b * steps + s, 0)),
            pl.BlockSpec((seq, gw), lambda b, s: (b, 1), pipeline_mode=once),
            pl.BlockSpec((seq, gw), lambda b, s: (b, 2), pipeline_mode=once),
            pl.BlockSpec((n_ctx, gw), lambda b, s: (b, 1)),
            pl.BlockSpec((n_ctx, gw), lambda b, s: (b, 2)),
        ] + [pl.BlockSpec((None,) + bias_tab.shape[1:], delta(i)) for i in range(nr)],
        out_specs=pl.BlockSpec((nr * GRID_W, gw), lambda b, s: (b * steps + s, 0)),
        out_shape=jax.ShapeDtypeStruct((bsz * seq, gw), BF16),
        compiler_params=_cparams(("parallel", "arbitrary")),
        name="na_attn",
    )(p16, p16, p16, pc16, pc16, *([bias_tab] * nr))


def _dup_heads(t):
    lane = lax.broadcasted_iota(jnp.int32, t.shape, 1)
    sw = pltpu.roll(t, HEAD_DIM, 1)
    return jnp.concatenate([jnp.where(lane < HEAD_DIM, t, sw), jnp.where(lane < HEAD_DIM, sw, t)], axis=1)


def _rope(t, cos, sa, sb):
    q = HEAD_DIM // 4
    return t * cos + pltpu.roll(t, q, 1) * sa + pltpu.roll(t, V7X_LANES - q, 1) * sb


def _swa_prep_kernel(q_ref, k_ref, v_ref, cos_ref, sa_ref, sb_ref, qo_ref, ko_ref, vo_ref, *, rope):
    k = k_ref[...]
    if rope:
        cos, sa, sb = cos_ref[...], sa_ref[...], sb_ref[...]
        k = _rope(k, cos, sa, sb)
    for p in range(q_ref.shape[1] // PAIR_W):
        sl = slice(p * PAIR_W, (p + 1) * PAIR_W)
        q = q_ref[:, sl]
        if rope:
            q = _rope(q, cos, sa, sb)
        qo_ref[:, sl] = (q * ATTN_SCALE).astype(BF16)
    ko_ref[...] = _dup_heads(k).astype(BF16)
    vo_ref[...] = _dup_heads(v_ref[...]).astype(BF16)


def _rope_tables(seq):
    t = jnp.arange(seq)
    row = (t // GRID_W).astype(F32)
    col = (t % GRID_W).astype(F32)
    quarter = HEAD_DIM // 4
    inv = ROPE_BASE ** (-2.0 * jnp.arange(quarter, dtype=F32) / (HEAD_DIM // 2))
    ang_r = row[:, None] * inv[None, :]
    ang_c = col[:, None] * inv[None, :]
    zero = jnp.zeros_like(ang_r)
    cos_h = jnp.concatenate([jnp.cos(ang_r)] * 2 + [jnp.cos(ang_c)] * 2, axis=1)
    sa_h = jnp.concatenate([zero, jnp.sin(ang_r), zero, jnp.sin(ang_c)], axis=1)
    sb_h = jnp.concatenate([-jnp.sin(ang_r), zero, -jnp.sin(ang_c), zero], axis=1)
    two = lambda a: jnp.concatenate([a, a], axis=1)
    return two(cos_h), two(sa_h), two(sb_h)


def _swa_prep_call(p32, tables, *, col_q, col_k, col_v, rows_per_seq, rope):
    m = p32.shape[0]
    gw = 4 * PAIR_W
    tm = _pick_tile(rows_per_seq, ELEMWISE_TM)
    nseq_tiles = rows_per_seq // tm
    tab = lambda i: (i % nseq_tiles, 0)
    return pl.pallas_call(
        functools.partial(_swa_prep_kernel, rope=rope),
        grid=(m // tm,),
        in_specs=[
            pl.BlockSpec((tm, gw), lambda i: (i, col_q // gw)),
            pl.BlockSpec((tm, PAIR_W), lambda i: (i, col_k // PAIR_W)),
            pl.BlockSpec((tm, PAIR_W), lambda i: (i, col_v // PAIR_W)),
            pl.BlockSpec((tm, PAIR_W), tab),
            pl.BlockSpec((tm, PAIR_W), tab),
            pl.BlockSpec((tm, PAIR_W), tab),
        ],
        out_specs=[pl.BlockSpec((tm, gw), lambda i: (i, 0)), pl.BlockSpec((tm, 2 * PAIR_W), lambda i: (i, 0)),
                   pl.BlockSpec((tm, 2 * PAIR_W), lambda i: (i, 0))],
        out_shape=[jax.ShapeDtypeStruct((m, gw), BF16), jax.ShapeDtypeStruct((m, 2 * PAIR_W), BF16),
                   jax.ShapeDtypeStruct((m, 2 * PAIR_W), BF16)],
        compiler_params=_cparams(("parallel",)),
        name="swa_prep",
    )(p32, p32, p32, *tables)


def _sink_col(sink_ref, p, n):
    row = lax.broadcasted_iota(jnp.int32, (2 * n, 1), 0)
    return jnp.where(row < n, sink_ref[2 * p], sink_ref[2 * p + 1])


def _swa_kernel(sink_ref, q_ref, kp_ref, kc_ref, kn_ref, vp_ref, vc_ref, vn_ref, kx_ref, vx_ref, o_ref, *, nblk):
    n = pl.program_id(1)
    blk = q_ref.shape[0]
    qi = lax.broadcasted_iota(jnp.int32, (2 * blk, blk), 0) % blk
    kj = lax.broadcasted_iota(jnp.int32, (2 * blk, blk), 1)
    m_prev = kj >= qi + jnp.where(n > 0, 0, blk)
    m_next = kj <= qi - jnp.where(n < nblk - 1, 0, blk)
    npairs = q_ref.shape[1] // PAIR_W
    for p in range(npairs):
        sl = slice(p * PAIR_W, (p + 1) * PAIR_W)
        g = p // (npairs // SWA_KV_HEADS)
        gs = slice(g * PAIR_W, (g + 1) * PAIR_W)
        qs = _stack_pair(q_ref[:, sl].astype(F32))
        o = _pair_softmax_av(
            qs, [kp_ref[:, gs], kc_ref[:, gs], kn_ref[:, gs], kx_ref[:, gs]],
            [vp_ref[:, gs], vc_ref[:, gs], vn_ref[:, gs], vx_ref[:, gs]],
            [None] * 4, [m_prev, None, m_next, None], _sink_col(sink_ref, p, blk))
        o_ref[:, sl] = _unstack_pair(o).astype(o_ref.dtype)


def _swa_call(sink, qr, kd, vd, kxd, vxd, *, bsz, seq, n_ctx):
    gw = qr.shape[1]
    kw = kd.shape[1]
    nblk = seq // SWA_BLOCK
    cur = lambda b, n: (b * nblk + n, 0)
    prev = lambda b, n: (b * nblk + jnp.maximum(n - 1, 0), 0)
    nxt = lambda b, n: (b * nblk + jnp.minimum(n + 1, nblk - 1), 0)
    cx = lambda b, n: (b, 0)
    return pl.pallas_call(
        functools.partial(_swa_kernel, nblk=nblk),
        grid=(bsz, nblk),
        in_specs=[pl.BlockSpec(memory_space=pltpu.SMEM),
                  pl.BlockSpec((SWA_BLOCK, gw), cur),
                  pl.BlockSpec((SWA_BLOCK, kw), prev), pl.BlockSpec((SWA_BLOCK, kw), cur), pl.BlockSpec((SWA_BLOCK, kw), nxt),
                  pl.BlockSpec((SWA_BLOCK, kw), prev), pl.BlockSpec((SWA_BLOCK, kw), cur), pl.BlockSpec((SWA_BLOCK, kw), nxt),
                  pl.BlockSpec((n_ctx, kw), cx), pl.BlockSpec((n_ctx, kw), cx)],
        out_specs=pl.BlockSpec((SWA_BLOCK, gw), cur),
        out_shape=jax.ShapeDtypeStruct((bsz * seq, gw), BF16),
        compiler_params=_cparams(("parallel", "arbitrary")),
        name="swa_attn",
    )(sink, qr, kd, kd, kd, vd, vd, vd, kxd, vxd)


def _ctx_attn_kernel(sink_ref, qa_ref, ka_ref, va_ref, qd_ref, kd_ref, vd_ref, oa_ref, od_ref):
    n = qa_ref.shape[0]
    npairs = qa_ref.shape[1] // PAIR_W
    for p in range(npairs):
        sl = slice(p * PAIR_W, (p + 1) * PAIR_W)
        qs = _stack_pair(qa_ref[:, sl].astype(F32) * ATTN_SCALE)
        o = _pair_softmax_av(qs, [ka_ref[:, sl]], [va_ref[:, sl]], [None], [None], None)
        oa_ref[:, sl] = _unstack_pair(o).astype(oa_ref.dtype)
    for p in range(npairs):
        sl = slice(p * PAIR_W, (p + 1) * PAIR_W)
        g = p // (npairs // SWA_KV_HEADS)
        gs = slice(g * PAIR_W, (g + 1) * PAIR_W)
        qs = _stack_pair(qd_ref[:, sl].astype(F32))
        o = _pair_softmax_av(qs, [kd_ref[:, gs]], [vd_ref[:, gs]], [None], [None], _sink_col(sink_ref, p, n))
        od_ref[:, sl] = _unstack_pair(o).astype(od_ref.dtype)


def _ctx_attn_call(sink, pc16, qx, kxd, vxd, *, bsz, n_ctx):
    gw = qx.shape[1]
    kw = kxd.shape[1]
    return pl.pallas_call(
        _ctx_attn_kernel,
        grid=(bsz,),
        in_specs=[pl.BlockSpec(memory_space=pltpu.SMEM),
                  pl.BlockSpec((n_ctx, gw), lambda b: (b, 0)), pl.BlockSpec((n_ctx, gw), lambda b: (b, 1)),
                  pl.BlockSpec((n_ctx, gw), lambda b: (b, 2)), pl.BlockSpec((n_ctx, gw), lambda b: (b, 0)),
                  pl.BlockSpec((n_ctx, kw), lambda b: (b, 0)), pl.BlockSpec((n_ctx, kw), lambda b: (b, 0))],
        out_specs=[pl.BlockSpec((n_ctx, gw), lambda b: (b, 0)), pl.BlockSpec((n_ctx, gw), lambda b: (b, 0))],
        out_shape=[jax.ShapeDtypeStruct((bsz * n_ctx, gw), BF16)] * 2,
        compiler_params=_cparams(("parallel",)),
        name="ctx_attn",
    )(sink, pc16, pc16, pc16, qx, kxd, vxd)


def _conv_kernel(u_ref, b_ref, c_ref, up_ref, cp_ref, un_ref, cn_ref, w_ref, o_ref, *, tiles_per_seq):
    i = pl.program_id(0)
    tm = u_ref.shape[0]
    pos = i % tiles_per_seq
    keep_prev = jnp.where(pos == 0, 0.0, 1.0)
    keep_next = jnp.where(pos == tiles_per_seq - 1, 0.0, 1.0)
    z = c_ref[...] * u_ref[...]
    z_prev = (cp_ref[...] * up_ref[...])[CONV_HALO - 1:CONV_HALO, :] * keep_prev
    z_next = (cn_ref[...] * un_ref[...])[0:1, :] * keep_next
    row = lax.broadcasted_iota(jnp.int32, z.shape, 0)
    zm1 = jnp.where(row == 0, z_prev, pltpu.roll(z, 1, 0))
    zp1 = jnp.where(row == tm - 1, z_next, pltpu.roll(z, tm - 1, 0))
    w = w_ref[...]
    o_ref[...] = (b_ref[...] * (w[0:1] * zm1 + w[1:2] * z + w[2:3] * zp1)).astype(o_ref.dtype)


def _conv_call(p32, conv_w, *, col_u, col_b, col_c, rows_per_seq):
    m = p32.shape[0]
    gw = conv_w.shape[1]
    tm = _pick_tile(rows_per_seq, ELEMWISE_TM)
    hb =tm // CONV_HALO
    n_halo = m // CONV_HALO
    cur = lambda col: (lambda i: (i, col // gw))
    prv = lambda col: (lambda i: (jnp.maximum(i * hb - 1, 0), col // gw))
    nxt = lambda col: (lambda i: (jnp.minimum((i + 1) * hb, n_halo - 1), col // gw))
    return pl.pallas_call(
        functools.partial(_conv_kernel, tiles_per_seq=rows_per_seq // tm),
        grid=(m // tm,),
        in_specs=[pl.BlockSpec((tm, gw), cur(col_u)), pl.BlockSpec((tm, gw), cur(col_b)), pl.BlockSpec((tm, gw), cur(col_c)),
                  pl.BlockSpec((CONV_HALO, gw), prv(col_u)), pl.BlockSpec((CONV_HALO, gw), prv(col_c)),
                  pl.BlockSpec((CONV_HALO, gw), nxt(col_u)), pl.BlockSpec((CONV_HALO, gw), nxt(col_c)),
                  pl.BlockSpec((SC_KSIZE, gw), lambda i: (0, 0))],
        out_specs=pl.BlockSpec((tm, gw), lambda i: (i, 0)),
        out_shape=jax.ShapeDtypeStruct((m, gw), BF16),
        compiler_params=_cparams(("parallel",)),
        name="short_conv",
    )(p32, p32, p32, p32, p32, p32, p32, conv_w)


def _gla_kernel(*refs, reverse, nt, fuse_out):
    if fuse_out:
        (q_ref, k_ref, v_ref, z_ref, w2_ref, gb_ref, s0_ref, of_ref, g_ref, gain_ref, o_ref, sfin_ref, st_ref) = refs
    else:
        (q_ref, k_ref, v_ref, z_ref, w2_ref, gb_ref, s0_ref, o_ref, sfin_ref, st_ref) = refs
    i = pl.program_id(1)

    @pl.when(i == 0)
    def _():
        st_ref[...] = s0_ref[...]

    tile = q_ref.shape[0]
    c_len = min(GLA_C, tile)
    dk2 = PAIR_W
    dv2 = v_ref.shape[1] // (q_ref.shape[1] // dk2)
    dv = dv2 // 2
    u = jnp.dot(z_ref[...].astype(BF16), w2_ref[...], preferred_element_type=F32) + gb_ref[...]
    la = (jnp.minimum(u, 0.0) - jnp.log1p(jnp.exp(-jnp.abs(u)))) * (1.0 / GLA_TAU)

    r_i = lax.broadcasted_iota(jnp.int32, (c_len, c_len), 0)
    c_i = lax.broadcasted_iota(jnp.int32, (c_len, c_len), 1)
    tri = (r_i <= c_i) if reverse else (r_i >= c_i)
    tri_bf = jnp.where(tri, 1.0, 0.0).astype(BF16)
    r2 = lax.broadcasted_iota(jnp.int32, (2 * c_len, c_len), 0) % c_len
    c2 = lax.broadcasted_iota(jnp.int32, (2 * c_len, c_len), 1)
    tri2 = (r2 <= c2) if reverse else (r2 >= c2)
    bd_r = lax.broadcasted_iota(jnp.int32, (dv2, dk2), 0) // dv
    bd_c = lax.broadcasted_iota(jnp.int32, (dv2, dk2), 1) // HEAD_DIM
    block_diag = bd_r == bd_c
    last_row = 0 if reverse else c_len - 1
    mid_row = c_len // 2

    n_chunks = tile // c_len
    order = range(n_chunks - 1, -1, -1) if reverse else range(n_chunks)
    for c in order:
        rows = slice(c * c_len, (c + 1) * c_len)
        la_c = la[rows]
        la_hi = la_c.astype(BF16)
        la_lo = (la_c - la_hi.astype(F32)).astype(BF16)
        cum = (jnp.dot(tri_bf, la_hi, preferred_element_type=F32)
               + jnp.dot(tri_bf, la_lo, preferred_element_type=F32))
        last = cum[last_row:last_row + 1]
        cmid = cum[mid_row:mid_row + 1]
        qc = q_ref[rows, :] * ATTN_SCALE
        kc = k_ref[rows, :]
        vc = v_ref[rows, :].astype(BF16)
        q_in = (qc * jnp.exp(cum)).astype(BF16)
        q_t = qc * jnp.exp(cum - cmid)
        k_t = (kc * jnp.exp(cmid - cum)).astype(BF16)
        k_p = (kc * jnp.exp(last - cum)).astype(BF16)
        g = jnp.exp(last)
        for p in range(q_ref.shape[1] // dk2):
            ls = slice(p * dk2, (p + 1) * dk2)
            a = lax.dot_general(_stack_pair(q_t[:, ls]), k_t[:, ls], _NT, preferred_element_type=F32)
            a = jnp.where(tri2, a, 0.0).astype(BF16)
            o0 = jnp.dot(a[:c_len], vc[:, p * dv2:p * dv2 + dv], preferred_element_type=F32)
            o1 = jnp.dot(a[c_len:], vc[:, p * dv2 + dv:(p + 1) * dv2], preferred_element_type=F32)
            st = st_ref[p]
            o_int = lax.dot_general(q_in[:, ls], st.astype(BF16), _NT, preferred_element_type=F32)
            o_p = jnp.concatenate([o0, o1], axis=1) + o_int
            upd = lax.dot_general(vc[:, p * dv2:(p + 1) * dv2], k_p[:, ls], _TN, preferred_element_type=F32)
            st_ref[p] = g[:, ls] * st + jnp.where(block_diag, upd, 0.0)
            if not fuse_out:
                o_ref[rows, p * dv2:(p + 1) * dv2] = o_p
            else:
                tot = of_ref[rows, p * dv2:(p + 1) * dv2] + o_p
                for hh in range(2):
                    hs = slice(p * dv2 + hh * dv, p * dv2 + (hh + 1) * dv)
                    oh = tot[:, hh * dv:(hh + 1) * dv]
                    on = oh * lax.rsqrt(jnp.mean(oh * oh, axis=-1, keepdims=True) + LN_EPS) * gain_ref[...]
                    o_ref[rows, hs] = (on * _silu(g_ref[rows, hs])).astype(o_ref.dtype)

    @pl.when(i == nt - 1)
    def _():
        sfin_ref[...] = st_ref[...]


def _gla_call(p, w2pad, gbias, s0, fuse, *, cols, bsz, seq, reverse):
    nq = GLA_HEADS * HEAD_DIM
    nv = s0.shape[1] * s0.shape[2]
    tile = _pick_tile(seq, GLA_TILE)
    nt = seq // tile
    tix = (lambda i: nt - 1 - i) if reverse else (lambda i: i)
    blk = lambda w, col: pl.BlockSpec((tile, w), lambda b, i: (b * nt + tix(i), col // w))
    const2 = lambda b, i: (0, 0)
    st_spec = pl.BlockSpec((None,) + s0.shape[1:], lambda b, i: (b, 0, 0, 0))
    in_specs = [blk(nq, cols['gl_q']), blk(nq, cols['gl_k']), blk(nv, cols['gl_v']), blk(V7X_LANES, cols['gl_z']),
                pl.BlockSpec(w2pad.shape, const2), pl.BlockSpec(gbias.shape, const2), st_spec]
    args = [p, p, p, p, w2pad, gbias, s0]
    if fuse is not None:
        o_other, gain = fuse
        in_specs += [blk(nv, 0), blk(nv, cols['gl_g']), pl.BlockSpec(gain.shape, const2)]
        args += [o_other, p, gain]
    out_dtype = BF16 if fuse is not None else F32
    return pl.pallas_call(
        functools.partial(_gla_kernel, reverse=reverse, nt=nt, fuse_out=fuse is not None),
        grid=(bsz, nt),
        in_specs=in_specs,
        out_specs=[blk(nv, 0), st_spec],
        out_shape=[jax.ShapeDtypeStruct((bsz * seq, nv), out_dtype), jax.ShapeDtypeStruct(s0.shape, F32)],
        scratch_shapes=[pltpu.VMEM(s0.shape[1:], F32)],
        compiler_params=_cparams(("parallel", "arbitrary")),
        name="gla_bwd" if reverse else "gla_fwd",
    )(*args)


def _gla_all(p32, pc32, w2, gb, gain, *, bsz, seq, n_ctx, d):
    cols = _p32_cols(d)
    nq = GLA_HEADS * HEAD_DIM
    dv = d // 4 // GLA_HEADS
    w2pad = [jnp.zeros((V7X_LANES, nq), F32).at[GLA_RANK * k:GLA_RANK * (k + 1)].set(w2[k]).astype(BF16) for k in range(2)]
    gbias = [gb[k].reshape(1, nq) for k in range(2)]
    s0 = jnp.zeros((bsz, GLA_HEADS // 2, 2 * dv, PAIR_W), F32)
    gain2 = gain.reshape(1, dv)
    oc_f, sc_f = _gla_call(pc32, w2pad[0], gbias[0], s0, None, cols=cols, bsz=bsz, seq=n_ctx, reverse=False)
    y_ctx, sc_b = _gla_call(pc32, w2pad[1], gbias[1], s0, (oc_f, gain2), cols=cols, bsz=bsz, seq=n_ctx, reverse=True)
    o_f, _ = _gla_call(p32, w2pad[0], gbias[0], sc_f, None, cols=cols, bsz=bsz, seq=seq, reverse=False)
    y_lat, _ = _gla_call(p32, w2pad[1], gbias[1], sc_b, (o_f, gain2), cols=cols, bsz=bsz, seq=seq, reverse=True)
    return y_lat, y_ctx


def _swiglu(x, wg, wu, wd):
    g = jnp.dot(x, wg, preferred_element_type=F32)
    u = jnp.dot(x, wu, preferred_element_type=F32)
    a = (_silu(g) * u).astype(BF16)
    return jnp.dot(a, wd, preferred_element_type=F32)


def _first_argmax(vals, idx, sentinel):
    m = jnp.max(vals, axis=0, keepdims=True)
    first = jnp.min(jnp.where(vals == m, idx, sentinel), axis=0, keepdims=True)
    return m, first


def _router_kernel(h_ref, wr_ref, rb_ref, su_ref, eidx_ref, wts_ref, rank_ref, cnt_ref, run_ref):
    i = pl.program_id(0)

    @pl.when(i == 0)
    def _():
        run_ref[...] = jnp.zeros_like(run_ref)

    t = h_ref.shape[0]
    gsz = N_EXPERTS // N_GROUPS
    logits = lax.dot_general(wr_ref[...], h_ref[...], _NT, preferred_element_type=F32)
    scores = jax.nn.sigmoid(logits)
    sel = scores + rb_ref[...]
    sub = lax.broadcasted_iota(jnp.int32, (gsz, t), 0)
    gscore = []
    for g in range(N_GROUPS):
        blk = sel[g * gsz:(g + 1) * gsz]
        m1, a1 = _first_argmax(blk, sub, gsz)
        m2 = jnp.max(jnp.where(sub == a1, -jnp.inf, blk), axis=0, keepdims=True)
        gscore.append(m1 + m2)
    gcur = jnp.concatenate(gscore, axis=0)
    gid = lax.broadcasted_iota(jnp.int32, (N_GROUPS, t), 0)
    gkeep = jnp.zeros((N_GROUPS, t), F32)
    for _ in range(TOPK_GROUPS):
        _, a = _first_argmax(gcur, gid, N_GROUPS)
        hit = gid == a
        gkeep = jnp.where(hit, 1.0, gkeep)
        gcur = jnp.where(hit, -jnp.inf, gcur)
    cur = jnp.concatenate(
        [jnp.where(gkeep[g:g + 1] > 0.0, sel[g * gsz:(g + 1) * gsz], -jnp.inf) for g in range(N_GROUPS)], axis=0)
    eid = lax.broadcasted_iota(jnp.int32, (N_EXPERTS, t), 0)
    chosen = jnp.zeros((N_EXPERTS, t), F32)
    hits, picks, wraw = [], [], []
    for _ in range(TOP_K):
        _, a = _first_argmax(cur, eid, N_EXPERTS)
        hit = eid == a
        hits.append(hit)
        picks.append(a)
        wraw.append(jnp.sum(jnp.where(hit, scores, 0.0), axis=0, keepdims=True))
        chosen = jnp.where(hit, 1.0, chosen)
        cur = jnp.where(hit, -jnp.inf, cur)
    wsum = wraw[0]
    for w in wraw[1:]:
        wsum = wsum + w
    eidx_ref[...] = jnp.concatenate(picks, axis=0)
    wts_ref[...] = jnp.concatenate([w / wsum * ROUTED_SCALE for w in wraw], axis=0)
    before = jnp.dot(chosen.astype(BF16), su_ref[...], preferred_element_type=F32) + run_ref[...][:, 0:1]
    rank_ref[...] = jnp.concatenate(
        [jnp.sum(jnp.where(hit, before, 0.0), axis=0, keepdims=True) for hit in hits], axis=0).astype(jnp.int32)
    run_ref[...] = run_ref[...] + jnp.sum(chosen, axis=1, keepdims=True)
    cnt_ref[...] = run_ref[...]


def _router_call(h, w_router, router_bias):
    n, d = h.shape
    t = _pick_tile(n, ROUTER_TILE)
    wr_t = w_router.T.astype(BF16)
    strict_upper = jnp.triu(jnp.ones((t, t), F32), 1).astype(BF16)
    const2 = lambda i: (0, 0)
    tok = lambda i: (0, i)
    eidx, wts, rank, cnt = pl.pallas_call(
        _router_kernel,
        grid=(n // t,),
        in_specs=[pl.BlockSpec((t, d), lambda i: (i, 0)), pl.BlockSpec((N_EXPERTS, d), const2),
                  pl.BlockSpec((N_EXPERTS, 1), const2), pl.BlockSpec((t, t), const2)],
        out_specs=[pl.BlockSpec((TOP_K, t), tok), pl.BlockSpec((TOP_K, t), tok), pl.BlockSpec((TOP_K, t), tok),
                   pl.BlockSpec((N_EXPERTS, V7X_LANES), const2)],
        out_shape=[jax.ShapeDtypeStruct((TOP_K, n), jnp.int32), jax.ShapeDtypeStruct((TOP_K, n), F32),
                   jax.ShapeDtypeStruct((TOP_K, n), jnp.int32), jax.ShapeDtypeStruct((N_EXPERTS, V7X_LANES), F32)],
        scratch_shapes=[pltpu.VMEM((N_EXPERTS, V7X_LANES), F32)],
        compiler_params=_cparams(("arbitrary",)),
        name="router",
    )(h, wr_t, router_bias.reshape(N_EXPERTS, 1).astype(F32), strict_upper)
    return eidx, wts, rank, cnt[:, 0].astype(jnp.int32)


def _dispatch_kernel(zoff_ref, nused_ref, dest_ref, hp_ref, xs_hbm, zbuf, sem, zsem):
    i = pl.program_id(0)
    t = dest_ref.shape[1]
    n_blocks = xs_hbm.shape[0] // MOE_TB

    def clear_block(off):
        return pltpu.make_async_copy(zbuf, xs_hbm.at[pl.ds(pl.multiple_of(off, MOE_TB), MOE_TB)], zsem)

    @pl.when(i == 0)
    def _():
        zbuf[...] = jnp.zeros_like(zbuf)
        for e in range(N_EXPERTS):
            clear_block(zoff_ref[e]).start()
        for e in range(N_EXPERTS):
            clear_block(0).wait()

        def clear_tail(b, carry):
            cp = clear_block(b * MOE_TB)
            cp.start()
            cp.wait()
            return carry

        lax.fori_loop(nused_ref[0], n_blocks, clear_tail, 0)

    def row_copy(j, slot):
        return pltpu.make_async_copy(hp_ref.at[pl.ds(j, 1)], xs_hbm.at[pl.ds(slot, 1)], sem)

    for j in range(t):
        for k in range(TOP_K):
            row_copy(j, dest_ref[k, j]).start(priority=k % 2)
    for _ in range(t * TOP_K):
        row_copy(0, 0).wait()


def _dispatch_call(zero_off, n_used, dest, h_packed, n_rows):
    n, c = h_packed.shape
    t = _pick_tile(n, DISPATCH_TILE)
    grid_spec = pltpu.PrefetchScalarGridSpec(
        num_scalar_prefetch=2,
        grid=(n // t,),
        in_specs=[pl.BlockSpec((TOP_K, t), lambda i, z, nu: (0, i), memory_space=pltpu.SMEM),
                  pl.BlockSpec((t, c), lambda i, z, nu: (i, 0))],
        out_specs=pl.BlockSpec(memory_space=pl.ANY),
        scratch_shapes=[pltpu.VMEM((MOE_TB, c), jnp.uint32), pltpu.SemaphoreType.DMA, pltpu.SemaphoreType.DMA],
    )
    return pl.pallas_call(
        _dispatch_kernel,
        grid_spec=grid_spec,
        out_shape=jax.ShapeDtypeStruct((n_rows, c), jnp.uint32),
        compiler_params=_cparams(("arbitrary",)),
        name="moe_dispatch",
    )(zero_off, n_used, dest, h_packed)


def _expert_kernel(be_ref, nused_ref, first_ref, slot_ref, nexte_ref, x_ref, oprev_ref, wg_hbm, wu_hbm, wd_hbm,
                   y_out, wg_f, wu_f, wd_f, wg_s, wu_s, wd_s, sem, ybuf, ssem, *, layer, dump_row):
    b = pl.program_id(0)

    def weight_copies(e, slot):
        return (pltpu.make_async_copy(wg_hbm.at[layer, e], wg_f.at[slot], sem.at[slot, 0]),
                pltpu.make_async_copy(wu_hbm.at[layer, e], wu_f.at[slot], sem.at[slot, 1]),
                pltpu.make_async_copy(wd_hbm.at[layer, e], wd_f.at[slot], sem.at[slot, 2]))

    @pl.when(b == 0)
    def _():
        for cp in weight_copies(be_ref[0], 0):
            cp.start()

    @pl.when(first_ref[b] == 1)
    def _():
        slot = slot_ref[b]
        for cp in weight_copies(be_ref[b], slot):
            cp.wait()

        @pl.when(nexte_ref[b] >= 0)
        def _():
            for cp in weight_copies(nexte_ref[b], 1 - slot):
                cp.start()

        wg_s[...] = wg_f[slot].astype(BF16)
        wu_s[...] = wu_f[slot].astype(BF16)
        wd_s[...] = wd_f[slot].astype(BF16)

    nu = nused_ref[0]
    tb = x_ref.shape[0]

    def scatter_copy(src_slot, r, dst_row):
        return pltpu.make_async_copy(ybuf.at[src_slot, pl.ds(r, 1)], y_out.at[pl.ds(dst_row, 1)], ssem.at[src_slot])

    def wait_scatter(src_slot):
        for r in range(tb):
            scatter_copy(src_slot, r, 0).wait()

    @pl.when(b == 0)
    def _():
        ybuf[...] = jnp.zeros_like(ybuf)

    @pl.when((b >= 1) & (b <= nu))
    def _():
        wait_scatter(b % 2)

    @pl.when(b < nu)
    def _():
        prev_slot = (b + 1) % 2

        def start_rows(lo, hi):
            for r in range(lo, hi):
                dst = jnp.where(b == 0, dump_row + r, oprev_ref[0, 0, r])
                scatter_copy(prev_slot, r, dst).start(priority=r % 2)

        n_pieces = 6
        cuts = [tb * p // n_pieces for p in range(n_pieces + 1)]
        x = _unpack_halves(x_ref[...])
        half_e = wg_s.shape[1] // 2
        acts = []
        for p in range(2):
            start_rows(cuts[p], cuts[p + 1])
            cs = slice(p * half_e, (p + 1) * half_e)
            g = jnp.dot(x, wg_s[:, cs], preferred_element_type=F32)
            u = jnp.dot(x, wu_s[:, cs], preferred_element_type=F32)
            acts.append((_silu(g) * u).astype(BF16))
        a = jnp.concatenate(acts, axis=1)
        quarter = wd_s.shape[1] // 4
        ys = []
        for p in range(4):
            start_rows(cuts[2 + p], cuts[3 + p])
            ys.append(jnp.dot(a, wd_s[:, p * quarter:(p + 1) * quarter], preferred_element_type=F32))
        ybuf[b % 2] = _pack_halves(jnp.concatenate(ys, axis=1))

    @pl.when(b == nu)
    def _():
        last_slot = (b + 1) % 2
        for r in range(tb):
            scatter_copy(last_slot, r, oprev_ref[0, 0, r]).start(priority=r % 2)
        wait_scatter(last_slot)


def _expert_call(block_e, n_used, run_first, run_slot, next_e, x_sorted, out_rows, n_out_rows, wg, wu, wd, layer):
    n_rows, c = x_sorted.shape
    _, _, d, de = wg.shape
    n_blocks = n_rows // MOE_TB
    xsel = lambda b, be, nu, *_: (jnp.minimum(b, nu[0] - 1), 0)
    osel = lambda b, *_: (jnp.clip(b - 1, 0, n_blocks - 1), 0, 0)
    hbm = pl.BlockSpec(memory_space=pl.ANY)
    grid_spec = pltpu.PrefetchScalarGridSpec(
        num_scalar_prefetch=5,
        grid=(n_blocks + 1,),
        in_specs=[pl.BlockSpec((MOE_TB, c), xsel), pl.BlockSpec((1, 1, MOE_TB), osel, memory_space=pltpu.SMEM),
                  hbm, hbm, hbm],
        out_specs=hbm,
        scratch_shapes=[pltpu.VMEM((2, d, de), F32), pltpu.VMEM((2, d, de), F32), pltpu.VMEM((2, de, d), F32),
                        pltpu.VMEM((d, de), BF16), pltpu.VMEM((d, de), BF16), pltpu.VMEM((de, d), BF16),
                        pltpu.SemaphoreType.DMA((2, 3)),
                        pltpu.VMEM((2, MOE_TB, c), jnp.uint32), pltpu.SemaphoreType.DMA((2,))],
    )
    return pl.pallas_call(
        functools.partial(_expert_kernel, layer=layer, dump_row=n_out_rows - MOE_TB),
        grid_spec=grid_spec,
        out_shape=jax.ShapeDtypeStruct((n_out_rows, c), jnp.uint32),
        compiler_params=_cparams(("arbitrary",)),
        name="experts",
    )(*(jnp.pad(a, (0, 1)) for a in (block_e,)), n_used, *(jnp.pad(a, (0, 1)) for a in (run_first, run_slot, next_e)),
      x_sorted, out_rows, wg, wu, wd)


def _combine_kernel(*refs, alpha):
    (w_ref, h_ref, x_ref, g2_ref, wsg_ref, wsu_ref, wsd_ref, lng_ref, lnb_ref), y_refs, o_ref = (
        refs[:9], refs[9:9 + TOP_K], refs[9 + TOP_K])
    acc = _swiglu(h_ref[...], wsg_ref[...], wsu_ref[...], wsd_ref[...])
    w = w_ref[...]
    half = acc.shape[1] // 2
    lo, hi = acc[:, :half], acc[:, half:]
    for k in range(TOP_K):
        yk = y_refs[k][...]
        wk = w[:, k:k + 1]
        lo = lo + wk * pltpu.bitcast(lax.shift_left(yk, jnp.uint32(16)), F32)
        hi = hi + wk * pltpu.bitcast(yk & jnp.uint32(0xFFFF0000), F32)
    acc = jnp.concatenate([lo, hi], axis=1)
    r = alpha * x_ref[...] + g2_ref[0] * acc
    o_ref[...] = _ln_rows(r) * lng_ref[...] + lnb_ref[...]


def _combine_call(wts_t, h, x2d, gate2, wsg, wsu, wsd, ln_g, ln_b, y_rows, *, row0, rows_per_group, alpha):
    m, d = x2d.shape
    n = h.shape[0]
    de = wsg.shape[1]
    t = _pick_tile(min(rows_per_group, m), COMBINE_TILE)
    assert row0 % t == 0 and rows_per_group % t == 0 and n % t == 0
    t0 = row0 // t
    const2 = lambda i: (0, 0)
    once = pl.Buffered(1)
    y_specs = [pl.BlockSpec((t, d // 2), functools.partial(lambda i, k: (k * (n // t) + t0 + i, 0), k=k))
               for k in range(TOP_K)]
    return pl.pallas_call(
        functools.partial(_combine_kernel, alpha=alpha),
        grid=(m // t,),
        in_specs=[pl.BlockSpec((t, TOP_K), lambda i: (t0 + i, 0)),
                  pl.BlockSpec((t, d), lambda i: (t0 + i, 0)),
                  pl.BlockSpec((t, d), lambda i: (i, 0)),
                  pl.BlockSpec((1, 1, d), lambda i: ((i * t) // rows_per_group, 0, 0)),
                  pl.BlockSpec((d, de), const2, pipeline_mode=once),
                  pl.BlockSpec((d, de), const2, pipeline_mode=once),
                  pl.BlockSpec((de, d), const2, pipeline_mode=once),
                  pl.BlockSpec((1, d), const2), pl.BlockSpec((1, d), const2)] + y_specs,
        out_specs=pl.BlockSpec((t, d), lambda i: (i, 0)),
        out_shape=jax.ShapeDtypeStruct((m, d), F32),
        compiler_params=_cparams(("parallel",)),
        name="moe_combine",
    )(wts_t, h, x2d, gate2, wsg, wsu, wsd, ln_g.reshape(1, d), ln_b.reshape(1, d), *([y_rows] * TOP_K))


def _moe_routed(h, h_packed, w_router, router_bias, w_eg, w_eu, w_ed, layer):
    n = h.shape[0]
    eidx, wts, rank, counts = _router_call(h, w_router, router_bias)
    padded = (counts + MOE_TB - 1) // MOE_TB * MOE_TB
    pad_end = jnp.cumsum(padded)
    pad_start = pad_end - padded
    n_blocks = (n * TOP_K + N_EXPERTS * (MOE_TB - 1) + MOE_TB - 1) // MOE_TB
    block_first = jnp.arange(n_blocks, dtype=jnp.int32) * MOE_TB
    block_e = jnp.minimum(jnp.sum((pad_end[None, :] <= block_first[:, None]).astype(jnp.int32), axis=1), N_EXPERTS - 1)
    n_used = (pad_end[-1:] // MOE_TB).astype(jnp.int32)
    zero_off = jnp.maximum(pad_end - MOE_TB, 0).astype(jnp.int32)
    expert_ids = jnp.arange(N_EXPERTS, dtype=jnp.int32)
    first_row = jnp.sum(jnp.where(eidx[:, :, None] == expert_ids, pad_start.astype(jnp.int32), 0), axis=-1)
    dest = first_row + rank
    x_sorted = _dispatch_call(zero_off, n_used, dest, h_packed, n_blocks * MOE_TB)
    blk = jnp.arange(n_blocks, dtype=jnp.int32)
    used = blk < n_used[0]
    run_first = ((block_e != jnp.concatenate([jnp.full((1,), -1, jnp.int32), block_e[:-1]])) & used).astype(jnp.int32)
    run_slot = ((jnp.cumsum(run_first) - 1) % 2).astype(jnp.int32)
    run_end = jnp.sum(jnp.where(block_e[:, None] == expert_ids, pad_end.astype(jnp.int32), 0), axis=1) // MOE_TB
    next_e = jnp.sum(jnp.where(blk[None, :] == run_end[:, None], block_e[None, :], 0), axis=1)
    next_e = jnp.where(used & (run_end < n_used[0]), next_e, -1).astype(jnp.int32)
    nk = n * TOP_K
    n_rows = n_blocks * MOE_TB
    n_pad = n_rows - nk
    pad_need = (padded - counts).astype(jnp.int32)
    need_end = jnp.cumsum(pad_need)
    j = jnp.arange(n_pad, dtype=jnp.int32)
    owner = jnp.sum((need_end[None, :] <= j[:, None]).astype(jnp.int32), axis=1)
    sel = owner[:, None] == expert_ids
    pick = lambda tab: jnp.sum(jnp.where(sel, tab.astype(jnp.int32), 0), axis=1)
    pad_slot = jnp.where(owner < N_EXPERTS,
                         pick(pad_start) + pick(counts) + j - (pick(need_end) - pick(pad_need)),
                         pad_end[-1].astype(jnp.int32) + j - need_end[-1])
    slots = jnp.concatenate([dest.reshape(-1), pad_slot])
    rows = jnp.arange(n_rows, dtype=jnp.int32)
    _, out_rows = lax.sort((slots, rows), num_keys=1)
    y_rows = _expert_call(block_e, n_used, run_first, run_slot, next_e, x_sorted,
                          out_rows.reshape(n_blocks, 1, MOE_TB), n_rows + MOE_TB, w_eg, w_eu, w_ed, layer)
    return wts.T, y_rows


def _pack_w_in(w, d):
    gw = d // 4
    cuts = np.cumsum([gw, gw, gw, gw, gw, gw, gw // 2, gw // 2, gw, gw, 2 * GLA_RANK, gw, gw // 4, gw // 4])[:-1].tolist()
    (na_q, na_k, na_v, sc_u, sc_b, sc_c, gl_q, gl_k, gl_v, gl_g, gl_z, sw_q, sw_k, sw_v) = jnp.split(w, cuts, axis=1)
    zpad = jnp.zeros((d, V7X_LANES - 2 * GLA_RANK), w.dtype)
    cols = [na_q, na_k, na_v, sc_u, sc_b, sc_c, gl_v, gl_g, sw_q, gl_q, gl_k, sw_k, sw_v, gl_z, zpad]
    packed = jnp.concatenate(cols, axis=1)
    pad = (-packed.shape[1]) % PROJ_TN
    packed = jnp.pad(packed, ((0, 0), (0, pad)))
    return packed.astype(BF16)


def _p32_cols(d):
    gw = d // 4
    cols, c = {}, 0
    for name, wdt in (('sc_u', gw), ('sc_b', gw), ('sc_c', gw), ('gl_v', gw), ('gl_g', gw), ('sw_q', gw),
                      ('gl_q', gw // 2), ('gl_k', gw // 2), ('sw_k', gw // 4), ('sw_v', gw // 4), ('gl_z', V7X_LANES)):
        cols[name] = c
        c += wdt
    return cols


def _token_mixers(p16, p32, pc16, pc32, rpb, conv_w, gla_w2, gla_b, gla_norm_g, sink, with_ctx_out, *,
                  bsz, seq, n_ctx, d):
    cols = _p32_cols(d)
    kh = min(NA_KH, seq // GRID_W)
    y_na = _na_call(p16, pc16, _na_bias_table(rpb, kh), bsz=bsz, seq=seq, n_ctx=n_ctx)
    conv_cols = dict(col_u=cols['sc_u'], col_b=cols['sc_b'], col_c=cols['sc_c'])
    y_sc = _conv_call(p32, conv_w, rows_per_seq=seq, **conv_cols)
    y_gl, yc_gl = _gla_all(p32, pc32, gla_w2, gla_b, gla_norm_g, bsz=bsz, seq=seq, n_ctx=n_ctx, d=d)
    tables = _rope_tables(seq)
    swa_cols = dict(col_q=cols['sw_q'], col_k=cols['sw_k'], col_v=cols['sw_v'])
    qr, kd, vd = _swa_prep_call(p32, tables, rows_per_seq=seq, rope=True, **swa_cols)
    ctx_tables = tuple(t[:n_ctx] for t in tables)
    qx, kxd, vxd = _swa_prep_call(pc32, ctx_tables, rows_per_seq=n_ctx, rope=False, **swa_cols)
    y_sw = _swa_call(sink, qr, kd, vd, kxd, vxd, bsz=bsz, seq=seq, n_ctx=n_ctx)
    y_lat = (y_na, y_sc, y_gl, y_sw)
    if not with_ctx_out:
        return y_lat, None
    yc_na, yc_sw = _ctx_attn_call(sink, pc16, qx, kxd, vxd, bsz=bsz, n_ctx=n_ctx)
    yc_sc = _conv_call(pc32, conv_w, rows_per_seq=n_ctx, **conv_cols)
    return y_lat, (yc_na, yc_sc, yc_gl, yc_sw)


def kernel(x, c, ctx, c_ctx, w_ada, b_ada, w_in, na_rpb, conv_w, gla_w2, gla_b, gla_norm_g, swa_sink, w_out,
           ln1_g, ln1_b, w_router, router_bias, w_exp_gate, w_exp_up, w_exp_down, w_sh_gate, w_sh_up, w_sh_down,
           ln2_g, ln2_b):
    bsz, seq, d = x.shape
    n_ctx = ctx.shape[1]
    nc = bsz * n_ctx
    depth = w_in.shape[0]
    alpha = (2 * depth) ** 0.25
    n16 = 3 * (d // 4)
    x2 = x.reshape(bsz * seq, d)
    hc2 = ctx.reshape(nc, d)
    c_rows = jnp.zeros((V7X_SUBLANES, d), F32).at[:bsz].set(c).at[bsz].set(c_ctx)
    for layer in range(depth):
        last = layer == depth - 1
        mod = _ada_call(c_rows, w_ada, b_ada[layer], layer)
        sh1, sc1, g1, sh2, sc2, g2 = [t[:bsz, None, :] for t in jnp.split(mod, 6, axis=-1)]
        sh1c, sc1c, g1c, sh2c, sc2c, g2c = [t[bsz:bsz + 1, None, :] for t in jnp.split(mod, 6, axis=-1)]
        w_p = _pack_w_in(w_in[layer], d)
        p16, p32 = _proj_call(x2, sh1, sc1, w_p, rows_per_group=seq, n16=n16)
        pc16, pc32 = _proj_call(hc2, sh1c, sc1c, w_p, rows_per_group=nc, n16=n16)
        y_lat, y_ctx = _token_mixers(p16, p32, pc16, pc32, na_rpb[layer], conv_w[layer], gla_w2[layer], gla_b[layer],
                                     gla_norm_g[layer], swa_sink[layer], not last, bsz=bsz, seq=seq, n_ctx=n_ctx, d=d)
        w_o = w_out[layer].astype(BF16)
        x2, h_lat, hp_lat = _outproj_call(y_lat, x2, g1, w_o, ln1_g[layer], ln1_b[layer], sh2, sc2,
                                          rows_per_group=seq, alpha=alpha)
        shared_w = (w_sh_gate[layer].astype(BF16), w_sh_up[layer].astype(BF16), w_sh_down[layer].astype(BF16))
        route_w = (w_router[layer], router_bias[layer], w_exp_gate, w_exp_up, w_exp_down, layer)
        if last:
            wts_t, y_rows = _moe_routed(h_lat, hp_lat, *route_w)
            x2 = _combine_call(wts_t, h_lat, x2, g2, *shared_w, ln2_g[layer], ln2_b[layer], y_rows,
                               row0=0, rows_per_group=seq, alpha=alpha)
        else:
            hc2, h_ctx, hp_ctx = _outproj_call(y_ctx, hc2, g1c, w_o, ln1_g[layer], ln1_b[layer], sh2c, sc2c,
                                               rows_per_group=nc, alpha=alpha)
            h_all = jnp.concatenate([h_ctx, h_lat], axis=0)
            hp_all = jnp.concatenate([hp_ctx, hp_lat], axis=0)
            wts_t, y_rows = _moe_routed(h_all, hp_all, *route_w)
            hc2 = _combine_call(wts_t, h_all, hc2, g2c, *shared_w, ln2_g[layer], ln2_b[layer], y_rows,
                                row0=0, rows_per_group=nc, alpha=alpha)
            x2 = _combine_call(wts_t, h_all, x2, g2, *shared_w, ln2_g[layer], ln2_b[layer], y_rows,
                               row0=nc, rows_per_group=seq, alpha=alpha)
    return x2.reshape(bsz, seq, d)
```
